```python
import math
import jax
import jax.numpy as jnp
from jax import lax
import numpy as np

D_MODEL = 4096
BATCH = 4
SEQ = 2048
DEPTH = 2

CTX_LEN = 256
GRID_W = 64
CHUNK = 64
EPS = 1e-6

D_MIX = D_MODEL
A_WIDTH = D_MIX // 4
A_DK = 128
A_HEADS = A_WIDTH // A_DK
B_WIDTH = D_MIX // 2
B_HEADDIM = 64
B_HEADS = B_WIDTH // B_HEADDIM
B_GROUPS = 8
B_RPG = B_HEADS // B_GROUPS
B_STATE = 128
B_CONV = 5
B_CONV_CH = B_WIDTH + 2 * B_GROUPS * B_STATE
C_WIDTH = D_MIX - A_WIDTH - B_WIDTH
C_DK = 128
C_HEADS = C_WIDTH // C_DK
IN_SIZES = (A_WIDTH,) * 5 + (B_WIDTH, B_CONV_CH, 2 * B_HEADS) + (C_WIDTH,) * 4 + (4 * C_HEADS,)
D_IN = sum(IN_SIZES)
N_EXPERTS = 64
N_EXPERT_GROUPS = 8
TOPK_GROUPS = 4
TOP_K = 8
D_EXPERT = 256
D_SHARED = 256
ROUTED_SCALE = 2.5

kernel_name = "hybrid_hgrn2_ssd_mlstm_moe_dit"


def rmsnorm(x, g):
    xf = x.astype(jnp.float32)
    y = xf * lax.rsqrt(jnp.mean(xf * xf, axis=-1, keepdims=True) + EPS)
    return (y * g.astype(jnp.float32)).astype(x.dtype)


def head_rmsnorm(y, g):
    y = y * lax.rsqrt(jnp.mean(y * y, axis=-1, keepdims=True) + EPS)
    return y * g.astype(jnp.float32).reshape(y.shape[-2:])


def modulate(x, g, shift, scale):
    return rmsnorm(x, g) * (1 + scale) + shift


def split_cols(u, sizes):
    offsets, acc = [], 0
    for s in sizes[:-1]:
        acc += s
        offsets.append(acc)
    return jnp.split(u, offsets, axis=-1)


def to_scan_order(a, layer, rows):
    if layer % 2 == 0:
        return a
    b, _, d = a.shape
    return a.reshape(b, rows, GRID_W, d).transpose(0, 2, 1, 3).reshape(b, rows * GRID_W, d)


def from_scan_order(a, layer, rows):
    if layer % 2 == 0:
        return a
    b, _, d = a.shape
    return a.reshape(b, GRID_W, rows, d).transpose(0, 2, 1, 3).reshape(b, rows * GRID_W, d)


def both(a):
    return jnp.broadcast_to(a[None], (2,) + a.shape)


def _lower_tri():
    return jnp.tril(jnp.ones((CHUNK, CHUNK), dtype=bool))


def chunked_scan(body, state, xs):
    z, length = xs[0].shape[:2]
    n = length // CHUNK
    xs_c = tuple(jnp.moveaxis(a.reshape((z, n, CHUNK) + a.shape[2:]), 1, 0) for a in xs)
    state, ys = lax.scan(body, state, xs_c)
    return jnp.moveaxis(ys, 0, 1).reshape((z, length) + ys.shape[3:]), state


def _reverse_dir1(p):
    p = jnp.concatenate([p[:1], jnp.flip(p[1:], axis=2)], axis=0)
    return p.reshape((-1,) + p.shape[2:])


def _merge_dirs(y, n_batch):
    y = y.reshape((2, n_batch) + y.shape[1:])
    return y[0] + jnp.flip(y[1], axis=1)


def bidirectional_scan(body, xs, state0, n_ctx):
    n_batch = xs[0].shape[1]
    ctx_xs = tuple(_reverse_dir1(a[:, :, :n_ctx]) for a in xs)
    lat_xs = tuple(_reverse_dir1(a[:, :, n_ctx:]) for a in xs)
    y_ctx, state_ctx = chunked_scan(body, state0, ctx_xs)
    y_lat, _ = chunked_scan(body, state_ctx, lat_xs)
    return jnp.concatenate([_merge_dirs(y_ctx, n_batch), _merge_dirs(y_lat, n_batch)], axis=1)


def _hgrn2_chunk(s, xs):
    q, k, v, g = xs
    gc = jnp.cumsum(g, axis=1)
    diff = gc[:, :, None] - gc[:, None, :]
    w = jnp.exp(jnp.where(_lower_tri()[None, :, :, None, None], diff, -jnp.inf))
    scores = jnp.einsum('zihd,zjhd,zijhd->zijh', q, k, w)
    y = jnp.einsum('zijh,zjhe->zihe', scores, v) + jnp.einsum('zihd,zhde->zihe', q * jnp.exp(gc), s)
    g_last = gc[:, -1]
    s = jnp.exp(g_last)[..., None] * s + jnp.einsum('zjhd,zjhe->zhde', k * jnp.exp(g_last[:, None] - gc), v)
    return s, y


def hgrn2_mixer(q, f_fwd, f_bwd, i, g, lb, norm_w, n_ctx):
    nb, length, _ = q.shape
    f32 = jnp.float32
    shp = (nb, length, A_HEADS, A_DK)
    qh = jax.nn.silu(q.astype(f32)).reshape(shp) * A_DK ** -0.5
    zf = jnp.stack([f_fwd, f_bwd]).astype(f32).reshape((2,) + shp)
    lbh = lb.astype(f32).reshape(2, 1, 1, A_HEADS, A_DK)
    log_f = jnp.logaddexp(jnp.log(lbh), jnp.log1p(-lbh) + jax.nn.log_sigmoid(zf))
    k = (1 - lbh) * jax.nn.sigmoid(-zf)
    v = i.astype(f32).reshape(shp)
    state0 = jnp.zeros((2 * nb, A_HEADS, A_DK, A_DK), f32)
    y = bidirectional_scan(_hgrn2_chunk, (both(qh), k, both(v), log_f), state0, n_ctx)
    y = head_rmsnorm(y, norm_w) * jax.nn.silu(g.astype(f32)).reshape(shp)
    return y.reshape(nb, length, A_WIDTH)


def dwconv_centred(a, w, b):
    k = w.shape[0]
    out = lax.conv_general_dilated(a, w.astype(a.dtype)[:, None, :], window_strides=(1,),
                                   padding=((k // 2, k // 2),), dimension_numbers=('NWC', 'WIO', 'NWC'),
                                   feature_group_count=a.shape[-1])
    return out + b.astype(a.dtype)


def _ssd_chunk(s, xs):
    xdt, bm, cm, da = xs
    cum = jnp.cumsum(da, axis=1)
    decay = jnp.exp(jnp.where(_lower_tri()[None, :, :, None, None], cum[:, :, None] - cum[:, None, :], -jnp.inf))
    cb = jnp.einsum('zign,zjgn->zijg', cm, bm)
    y = jnp.einsum('zijg,zijgr,zjgrp->zigrp', cb, decay, xdt) \
        + jnp.exp(cum)[..., None] * jnp.einsum('zign,zgrnp->zigrp', cm, s)
    c_last = cum[:, -1]
    s = jnp.exp(c_last)[..., None, None] * s \
        + jnp.einsum('zjgn,zjgr,zjgrp->zgrnp', bm, jnp.exp(c_last[:, None] - cum), xdt)
    return s, y


def ssd_mixer(z, xbc, dt, conv_w, conv_b, dt_bias, a_log, d_skip, norm_w, n_ctx):
    nb, length, _ = z.shape
    f32 = jnp.float32
    xbc = xbc.astype(f32)
    xbc = jax.nn.silu(jnp.concatenate([dwconv_centred(xbc[:, :n_ctx], conv_w, conv_b),
                                       dwconv_centred(xbc[:, n_ctx:], conv_w, conv_b)], axis=1))
    xs, bm, cm = jnp.split(xbc, [B_WIDTH, B_WIDTH + B_GROUPS * B_STATE], axis=-1)
    xh = xs.reshape(nb, length, B_GROUPS, B_RPG, B_HEADDIM)
    bm = bm.reshape(nb, length, B_GROUPS, B_STATE)
    cm = cm.reshape(nb, length, B_GROUPS, B_STATE)
    dt = jnp.moveaxis(dt.astype(f32).reshape(nb, length, 2, B_HEADS), 2, 0)
    dt = jax.nn.softplus(dt + dt_bias.astype(f32)[:, None, None, :])
    a = -jnp.exp(a_log.astype(f32))[:, None, None, :]
    da = (dt * a).reshape(2, nb, length, B_GROUPS, B_RPG)
    xdt = xh[None] * dt.reshape(2, nb, length, B_GROUPS, B_RPG)[..., None]
    state0 = jnp.zeros((2 * nb, B_GROUPS, B_RPG, B_STATE, B_HEADDIM), f32)
    y = bidirectional_scan(_ssd_chunk, (xdt, both(bm), both(cm), da), state0, n_ctx)
    y = y + d_skip.astype(f32).reshape(B_GROUPS, B_RPG)[..., None] * xh
    y = y.reshape(nb, length, B_WIDTH) * jax.nn.silu(z.astype(f32))
    y = head_rmsnorm(y.reshape(nb, length, B_GROUPS, B_WIDTH // B_GROUPS), norm_w)
    return y.reshape(nb, length, B_WIDTH)


def _mlstm_chunk(state, xs):
    cmat, nvec, m0 = state
    q, k, v, ig, lf = xs
    b = jnp.cumsum(lf, axis=1)
    dmat = jnp.where(_lower_tri()[None, :, :, None], b[:, :, None] - b[:, None, :] + ig[:, None, :], -jnp.inf)
    m_inter = b + m0[:, None]
    m_i = jnp.maximum(m_inter, jnp.max(dmat, axis=2))
    w_intra = jnp.exp(dmat - m_i[:, :, None])
    w_inter = jnp.exp(m_inter - m_i)
    s = jnp.einsum('zihd,zjhd->zijh', q, k) * w_intra
    num = jnp.einsum('zijh,zjhe->zihe', s, v) + w_inter[..., None] * jnp.einsum('zihd,zhde->zihe', q, cmat)
    den = jnp.sum(s, axis=2) + w_inter * jnp.einsum('zihd,zhd->zih', q, nvec)
    h = num / jnp.maximum(jnp.abs(den), jnp.exp(-m_i))[..., None]
    b_last = b[:, -1]
    log_wj = b_last[:, None] - b + ig
    m_new = jnp.maximum(b_last + m0, jnp.max(log_wj, axis=1))
    w0 = jnp.exp(b_last + m0 - m_new)
    wj = jnp.exp(log_wj - m_new[:, None])
    cmat = w0[..., None, None] * cmat + jnp.einsum('zjh,zjhd,zjhe->zhde', wj, k, v)
    nvec = w0[..., None] * nvec + jnp.einsum('zjh,zjhd->zhd', wj, k)
    return (cmat, nvec, m_new), h


def mlstm_mixer(q, k, v, o, gates, i_bias, f_bias, norm_w, n_ctx):
    nb, length, _ = q.shape
    f32 = jnp.float32
    shp = (nb, length, C_HEADS, C_DK)
    qh = q.astype(f32).reshape(shp) * C_DK ** -0.5
    kh = k.astype(f32).reshape(shp)
    vh = v.astype(f32).reshape(shp)
    gates = gates.astype(f32).reshape(nb, length, 2, 2, C_HEADS)
    ig = jnp.moveaxis(gates[:, :, 0], 2, 0) + i_bias.astype(f32)[:, None, None, :]
    lf = jax.nn.log_sigmoid(jnp.moveaxis(gates[:, :, 1], 2, 0) + f_bias.astype(f32)[:, None, None, :])
    state0 = (jnp.zeros((2 * nb, C_HEADS, C_DK, C_DK), f32), jnp.zeros((2 * nb, C_HEADS, C_DK), f32),
              jnp.zeros((2 * nb, C_HEADS), f32))
    h = bidirectional_scan(_mlstm_chunk, (both(qh), both(kh), both(vh), ig, lf), state0, n_ctx)
    h = h * jax.nn.sigmoid(o.astype(f32)).reshape(shp)
    return head_rmsnorm(h, norm_w).reshape(nb, length, C_WIDTH)


def token_mix(h, w_in, lb, hgrn_norm, conv_w, conv_b, dt_bias, a_log, d_skip, ssm_norm,
              i_bias, f_bias, mlstm_norm, n_ctx):
    u = h @ w_in
    (a_q, a_f_fwd, a_f_bwd, a_i, a_g, b_z, b_xbc, b_dt,
     c_q, c_k, c_v, c_o, c_gates) = split_cols(u, IN_SIZES)
    y_a = hgrn2_mixer(a_q, a_f_fwd, a_f_bwd, a_i, a_g, lb, hgrn_norm, n_ctx)
    y_b = ssd_mixer(b_z, b_xbc, b_dt, conv_w, conv_b, dt_bias, a_log, d_skip, ssm_norm, n_ctx)
    y_c = mlstm_mixer(c_q, c_k, c_v, c_o, c_gates, i_bias, f_bias, mlstm_norm, n_ctx)
    return jnp.concatenate([y_a, y_b, y_c], axis=-1).astype(h.dtype)


def moe_ffn(h, router_w, router_bias, w_gate, w_up, w_down, sh_gate, sh_up, sh_down):
    shp = h.shape
    t = h.reshape(-1, shp[-1])
    scores = jax.nn.sigmoid((t @ router_w).astype(jnp.float32))
    biased = scores + router_bias.astype(jnp.float32)
    grp = biased.reshape(-1, N_EXPERT_GROUPS, N_EXPERTS // N_EXPERT_GROUPS)
    grp_score = jnp.sum(lax.top_k(grp, 2)[0], axis=-1)
    _, gidx = lax.top_k(grp_score, TOPK_GROUPS)
    gmask = jnp.sum(jax.nn.one_hot(gidx, N_EXPERT_GROUPS, dtype=jnp.float32), axis=1) > 0
    emask = jnp.repeat(gmask, N_EXPERTS // N_EXPERT_GROUPS, axis=1)
    _, eidx = lax.top_k(jnp.where(emask, biased, -jnp.inf), TOP_K)
    w = jnp.take_along_axis(scores, eidx, axis=1)
    w = w / jnp.sum(w, axis=-1, keepdims=True) * ROUTED_SCALE
    gates = jnp.sum(jax.nn.one_hot(eidx, N_EXPERTS, dtype=jnp.float32) * w[..., None], axis=1).astype(t.dtype)
    act = jax.nn.silu(jnp.einsum('td,edf->tef', t, w_gate)) * jnp.einsum('td,edf->tef', t, w_up)
    routed = jnp.einsum('tef,efd->td', act * gates[..., None], w_down)
    shared = (jax.nn.silu(t @ sh_gate) * (t @ sh_up)) @ sh_down
    return (routed + shared).reshape(shp)


def setup_inputs(seed: int = 0) -> dict:
    key = jax.random.key(seed)
    ks = jax.random.split(key, 32)
    f32 = jnp.float32

    def nrm(k, shape, s):
        return jax.random.normal(k, shape, f32) * s

    dt0 = jnp.exp(jax.random.uniform(ks[15], (DEPTH, 2, B_HEADS), f32, math.log(1e-3), math.log(1e-1)))
    return {
        'x': nrm(ks[0], (BATCH, SEQ, D_MODEL), 1.0),
        'c': nrm(ks[1], (BATCH, D_MODEL), 1.0),
        'ctx': nrm(ks[2], (BATCH, CTX_LEN, D_MODEL), 1.0),
        'c_ctx': nrm(ks[3], (D_MODEL,), 1.0),
        'ada_w': nrm(ks[4], (DEPTH, D_MODEL, 6 * D_MODEL), 0.5 * D_MODEL ** -0.5),
        'ada_b': nrm(ks[5], (DEPTH, 6 * D_MODEL), 0.01),
        'norm_mix': 1.0 + nrm(ks[6], (DEPTH, D_MODEL), 0.02),
        'norm_ffn': 1.0 + nrm(ks[7], (DEPTH, D_MODEL), 0.02),
        'norm_final': 1.0 + nrm(ks[8], (D_MODEL,), 0.02),
        'w_in': nrm(ks[9], (DEPTH, D_MODEL, D_IN), D_MODEL ** -0.5),
        'w_out': nrm(ks[10], (DEPTH, D_MIX, D_MODEL), D_MIX ** -0.5),
        'hgrn_lb': nrm(ks[11], (DEPTH, 2, A_WIDTH), 1.0),
        'hgrn_norm': 1.0 + nrm(ks[12], (DEPTH, A_WIDTH), 0.02),
        'ssm_conv_w': nrm(ks[13], (DEPTH, B_CONV, B_CONV_CH), B_CONV ** -0.5),
        'ssm_conv_b': nrm(ks[14], (DEPTH, B_CONV_CH), 0.02),
        'ssm_dt_bias': dt0 + jnp.log(-jnp.expm1(-dt0)),
        'ssm_a_log': jnp.log(jax.random.uniform(ks[16], (DEPTH, 2, B_HEADS), f32, 1.0, 16.0)),
        'ssm_d': 1.0 + nrm(ks[17], (DEPTH, B_HEADS), 0.1),
        'ssm_norm': 1.0 + nrm(ks[18], (DEPTH, B_WIDTH), 0.02),
        'mlstm_i_bias': nrm(ks[19], (DEPTH, 2, C_HEADS), 0.1),
        'mlstm_f_bias': jnp.linspace(3.0, 6.0, C_HEADS, dtype=f32) + nrm(ks[20], (DEPTH, 2, C_HEADS), 0.1),
        'mlstm_norm': 1.0 + nrm(ks[21], (DEPTH, C_WIDTH), 0.02),
        'router_w': nrm(ks[22], (DEPTH, D_MODEL, N_EXPERTS), D_MODEL ** -0.5),
        'router_bias': nrm(ks[23], (DEPTH, N_EXPERTS), 0.01),
        'moe_w_gate': nrm(ks[24], (DEPTH, N_EXPERTS, D_MODEL, D_EXPERT), D_MODEL ** -0.5),
        'moe_w_up': nrm(ks[25], (DEPTH, N_EXPERTS, D_MODEL, D_EXPERT), D_MODEL ** -0.5),
        'moe_w_down': nrm(ks[26], (DEPTH, N_EXPERTS, D_EXPERT, D_MODEL), D_EXPERT ** -0.5),
        'shared_w_gate': nrm(ks[27], (DEPTH, D_MODEL, D_SHARED), D_MODEL ** -0.5),
        'shared_w_up': nrm(ks[28], (DEPTH, D_MODEL, D_SHARED), D_MODEL ** -0.5),
        'shared_w_down': nrm(ks[29], (DEPTH, D_SHARED, D_MODEL), D_SHARED ** -0.5),
    }


def reference(x, c, ctx, c_ctx, ada_w, ada_b, norm_mix, norm_ffn, norm_final, w_in, w_out,
              hgrn_lb, hgrn_norm, ssm_conv_w, ssm_conv_b, ssm_dt_bias, ssm_a_log, ssm_d, ssm_norm,
              mlstm_i_bias, mlstm_f_bias, mlstm_norm, router_w, router_bias,
              moe_w_gate, moe_w_up, moe_w_down, shared_w_gate, shared_w_up, shared_w_down):
    n_ctx = ctx.shape[1]
    rows = x.shape[1] // GRID_W
    lb_all = jnp.cumsum(jax.nn.softmax(hgrn_lb.astype(jnp.float32), axis=0), axis=0)
    lb_all = lb_all - lb_all[0]
    silu_c = jax.nn.silu(c)
    silu_cc = jax.nn.silu(c_ctx)
    x_lat, x_ctx = x, ctx
    for l in range(DEPTH):
        last = l == DEPTH - 1
        mod_l = jnp.split((silu_c @ ada_w[l] + ada_b[l])[:, None, :], 6, axis=-1)
        mod_c = jnp.split(silu_cc @ ada_w[l] + ada_b[l], 6, axis=-1)
        h_lat = to_scan_order(modulate(x_lat, norm_mix[l], mod_l[0], mod_l[1]), l, rows)
        h_ctx = modulate(x_ctx, norm_mix[l], mod_c[0], mod_c[1])
        y = token_mix(jnp.concatenate([h_ctx, h_lat], axis=1), w_in[l], lb_all[l], hgrn_norm[l],
                      ssm_conv_w[l], ssm_conv_b[l], ssm_dt_bias[l], ssm_a_log[l], ssm_d[l], ssm_norm[l],
                      mlstm_i_bias[l], mlstm_f_bias[l], mlstm_norm[l], n_ctx)
        y_lat = from_scan_order(y[:, n_ctx:], l, rows)
        x_lat = x_lat + mod_l[2] * (y_lat @ w_out[l])
        x_lat = x_lat + mod_l[5] * moe_ffn(modulate(x_lat, norm_ffn[l], mod_l[3], mod_l[4]),
                                           router_w[l], router_bias[l], moe_w_gate[l], moe_w_up[l],
                                           moe_w_down[l], shared_w_gate[l], shared_w_up[l], shared_w_down[l])
        if not last:
            x_ctx = x_ctx + mod_c[2] * (y[:, :n_ctx] @ w_out[l])
            x_ctx = x_ctx + mod_c[5] * moe_ffn(modulate(x_ctx, norm_ffn[l], mod_c[3], mod_c[4]),
                                               router_w[l], router_bias[l], moe_w_gate[l], moe_w_up[l],
                                               moe_w_down[l], shared_w_gate[l], shared_w_up[l], shared_w_down[l])
    return rmsnorm(x_lat, norm_final)
```

```python
import functools

import jax
import jax.numpy as jnp
from jax import lax
from jax.experimental import pallas as pl
from jax.experimental.pallas import tpu as pltpu

F32 = jnp.float32
BF16 = jnp.bfloat16
HIGHEST = lax.Precision.HIGHEST
NEG_INF = float("-inf")

D_MODEL = 4096
GRID_W = 64
CHUNK = 64
EPS = 1e-6
A_WIDTH = 1024
A_DK = 128
A_HEADS = 8
B_WIDTH = 2048
B_HEADDIM = 64
B_HEADS = 32
B_GROUPS = 8
B_RPG = 4
B_STATE = 128
B_CONV = 5
B_CONV_CH = 4096
C_WIDTH = 1024
C_DK = 128
C_HEADS = 8
N_EXPERTS = 64
N_EXPERT_GROUPS = 8
TOPK_GROUPS = 4
TOP_K = 8
D_EXPERT = 256
D_SHARED = 256
ROUTED_SCALE = 2.5

U_AB = 11264
U_DT0 = 11264
U_C0 = 11328
U_G0 = 15424
U_SMALL = 128

SUB = 16
PAD = 16
VMEM_LIMIT = 56 * 1024 * 1024


def _cparams(sem):
    return pltpu.CompilerParams(dimension_semantics=sem, vmem_limit_bytes=VMEM_LIMIT)


def _dot(a, b):
    return jnp.dot(a.astype(BF16), b.astype(BF16), preferred_element_type=F32)


def _dot_nt(a, b):
    return lax.dot_general(a.astype(BF16), b.astype(BF16), (((1,), (1,)), ((), ())),
                           preferred_element_type=F32)


def _dot_tn(a, b):
    return lax.dot_general(a.astype(BF16), b.astype(BF16), (((0,), (0,)), ((), ())),
                           preferred_element_type=F32)


def _dot_hi(a, b):
    return jnp.dot(a, b, precision=HIGHEST, preferred_element_type=F32)


def _dot_nt_hi(a, b):
    return lax.dot_general(a, b, (((1,), (1,)), ((), ())), precision=HIGHEST,
                           preferred_element_type=F32)


def _log_sigmoid(z):
    return jnp.minimum(z, 0.0) - jnp.log1p(jnp.exp(-jnp.abs(z)))


def _softplus(z):
    return jnp.maximum(z, 0.0) + jnp.log1p(jnp.exp(-jnp.abs(z)))


def _silu(z):
    return z * jax.nn.sigmoid(z)


def _tri(rev):
    ri = lax.broadcasted_iota(jnp.int32, (CHUNK, CHUNK), 0)
    ci = lax.broadcasted_iota(jnp.int32, (CHUNK, CHUNK), 1)
    return (ci >= ri) if rev else (ci <= ri)


def _chunk_block(c, n_ctx_chunks, n_chunks, rev):
    if not rev:
        return c
    return jnp.where(c < n_ctx_chunks, n_ctx_chunks - 1 - c, 2 * n_ctx_chunks + n_chunks - 1 - c)


def _ada_kernel(c_ref, w_ref, b_ref, o_ref):
    s = _silu(c_ref[...])
    o_ref[...] = _dot_hi(s, w_ref[...]) + b_ref[...]


def ada_modulation(cond, ada_w, ada_b):
    depth, d, n = ada_w.shape
    tn = 512
    return pl.pallas_call(
        _ada_kernel,
        grid=(depth, n // tn),
        in_specs=[
            pl.BlockSpec((8, d), lambda l, j: (0, 0)),
            pl.BlockSpec((None, d, tn), lambda l, j: (l, 0, j)),
            pl.BlockSpec((None, 1, tn), lambda l, j: (l, 0, j)),
        ],
        out_specs=pl.BlockSpec((None, 8, tn), lambda l, j: (l, 0, j)),
        out_shape=jax.ShapeDtypeStruct((depth, 8, n), F32),
        compiler_params=_cparams(("arbitrary", "arbitrary")),
        name="ada_modulation",
    )(cond, ada_w, ada_b.reshape(depth, 1, n))


def _pack_bf16_pair(a, b):
    hi = lax.bitcast_convert_type(a.astype(BF16).astype(F32), jnp.uint32)
    lo = lax.bitcast_convert_type(b.astype(BF16).astype(F32), jnp.uint32)
    return hi | (lo >> 16)


def _unpack_bf16_pair(w):
    hi = lax.bitcast_convert_type(w & jnp.uint32(0xFFFF0000), F32)
    lo = lax.bitcast_convert_type(w << 16, F32)
    return hi, lo


def _modulate_kernel(x_ref, g_ref, sh_ref, sc_ref, *rest, with_router):
    x = x_ref[...]
    y = x * lax.rsqrt(jnp.mean(x * x, axis=-1, keepdims=True) + EPS) * g_ref[...]
    h = y * (1.0 + sc_ref[...]) + sh_ref[...]
    if with_router:
        rw_ref, h_ref, lg_ref = rest
        lg_ref[...] = _dot_hi(h, rw_ref[...])
        half = h.shape[1] // 2
        h_ref[...] = _pack_bf16_pair(h[:, :half], h[:, half:])
    else:
        (h_ref,) = rest
        h_ref[...] = h.astype(h_ref.dtype)


def modulate(x, g, shift, scale, seg_of_tile, tm, router_w=None):
    t, d = x.shape
    with_router = router_w is not None
    vec = pl.BlockSpec((None, 1, d), lambda i: (seg_of_tile(i), 0, 0))
    in_specs = [pl.BlockSpec((tm, d), lambda i: (i, 0)),
                pl.BlockSpec((1, d), lambda i: (0, 0)), vec, vec]
    args = [x, g.reshape(1, d), shift, scale]
    if with_router:
        in_specs.append(pl.BlockSpec((d, 128), lambda i: (0, 0)))
        args.append(router_w)
        out_specs = [pl.BlockSpec((tm, d // 2), lambda i: (i, 0)),
                     pl.BlockSpec((tm, 128), lambda i: (i, 0))]
        out_shape = [jax.ShapeDtypeStruct((t, d // 2), jnp.uint32),
                     jax.ShapeDtypeStruct((t, 128), F32)]
    else:
        out_specs = pl.BlockSpec((tm, d), lambda i: (i, 0))
        out_shape = jax.ShapeDtypeStruct((t, d), BF16)
    return pl.pallas_call(
        functools.partial(_modulate_kernel, with_router=with_router),
        grid=(t // tm,),
        in_specs=in_specs, out_specs=out_specs, out_shape=out_shape,
        compiler_params=_cparams(("arbitrary",)),
        name="modulate_router" if with_router else "modulate",
    )(*args)


def _mm_kernel(a_ref, b_ref, o_ref):
    o_ref[...] = jnp.dot(a_ref[...], b_ref[...].astype(BF16), preferred_element_type=F32)


def matmul(a, b, *, tm, tn, n_tiles=None, name="matmul"):
    m, k = a.shape
    n_tiles = b.shape[1] // tn if n_tiles is None else n_tiles
    return pl.pallas_call(
        _mm_kernel,
        grid=(m // tm, n_tiles),
        in_specs=[pl.BlockSpec((tm, k), lambda i, j: (i, 0)),
                  pl.BlockSpec((k, tn), lambda i, j: (0, j))],
        out_specs=pl.BlockSpec((tm, tn), lambda i, j: (i, j)),
        out_shape=jax.ShapeDtypeStruct((m, n_tiles * tn), F32),
        compiler_params=_cparams(("arbitrary", "arbitrary")),
        name=name,
    )(a, b)


def _mm_res_kernel(a_ref, b_ref, r_ref, g_ref, o_ref, b_s):
    @pl.when(pl.program_id(1) == 0)
    def _():
        b_s[...] = b_ref[...].astype(BF16)

    acc = jnp.dot(a_ref[...], b_s[...], preferred_element_type=F32)
    o_ref[...] = r_ref[...] + g_ref[...] * acc


def matmul_residual(a, b, res, gate, seg_of_tile, *, tm, tn):
    m, k = a.shape
    n = b.shape[1]
    return pl.pallas_call(
        _mm_res_kernel,
        grid=(n // tn, m // tm),
        in_specs=[pl.BlockSpec((tm, k), lambda j, i: (i, 0)),
                  pl.BlockSpec((k, tn), lambda j, i: (0, j)),
                  pl.BlockSpec((tm, tn), lambda j, i: (i, j)),
                  pl.BlockSpec((None, 1, tn), lambda j, i: (seg_of_tile(i), 0, j))],
        out_specs=pl.BlockSpec((tm, tn), lambda j, i: (i, j)),
        out_shape=jax.ShapeDtypeStruct((m, n), F32),
        scratch_shapes=[pltpu.VMEM((k, tn), BF16)],
        compiler_params=_cparams(("arbitrary", "arbitrary")),
        name="matmul_residual",
    )(a, b, res, gate)


def _hgrn2_kernel(q_ref, f_ref, v_ref, lb_ref, y_ref, qs_s, k_s, v_s, gc_s, gx_s, st_s, y_s, *, rev):
    c = pl.program_id(1)

    @pl.when(c == 0)
    def _():
        st_s[...] = jnp.zeros_like(st_s)
        k_s[...] = jnp.zeros_like(k_s)
        v_s[...] = jnp.zeros_like(v_s)
        gc_s[...] = jnp.zeros_like(gc_s)

    q = q_ref[0]
    z = f_ref[0]
    v = v_ref[0]
    log_lb = lb_ref[0:1, :]
    log1m_lb = lb_ref[1:2, :]
    one_m_lb = lb_ref[2:3, :]
    qs = q * jax.nn.sigmoid(q) * (A_DK ** -0.5)
    b2 = log1m_lb + _log_sigmoid(z)
    logf = jnp.maximum(log_lb, b2) + jnp.log1p(jnp.exp(-jnp.abs(log_lb - b2)))
    k = one_m_lb * jax.nn.sigmoid(-z)
    gc = _dot_hi(_tri(rev).astype(F32), logf)
    gx = gc - logf
    for h in range(A_HEADS):
        sl = slice(h * A_DK, (h + 1) * A_DK)
        qs_s[h] = qs[:, sl]
        k_s[h, PAD:PAD + CHUNK, :] = k[:, sl]
        v_s[h, PAD:PAD + CHUNK, :] = v[:, sl]
        gc_s[h, PAD:PAD + CHUNK, :] = gc[:, sl]
        gx_s[h] = gx[:, sl]

    rowmod = lax.broadcasted_iota(jnp.int32, (CHUNK, 1), 0) % SUB
    rows = lax.broadcasted_iota(jnp.int32, (CHUNK, 1), 0)
    n_sub = CHUNK // SUB

    def head(h, carry):
        qh = qs_s[h]
        kh = k_s[h, PAD:PAD + CHUNK, :]
        vh = v_s[h, PAD:PAD + CHUNK, :]
        gcm = gc_s[h, PAD:PAD + CHUNK, :]
        gxm = gx_s[h]
        st = st_s[h]
        y = _dot_nt(qh * jnp.exp(gcm), st)
        for dlt in range(SUB):
            off = PAD + dlt if rev else PAD - dlt
            ksh = k_s[h, off:off + CHUNK, :]
            vsh = v_s[h, off:off + CHUNK, :]
            gsh = gc_s[h, off:off + CHUNK, :]
            valid = (rowmod + dlt < SUB) if rev else (rowmod >= dlt)
            e = jnp.exp(jnp.where(valid, gcm - gsh, NEG_INF))
            r = jnp.sum(qh * ksh * e, axis=1, keepdims=True)
            y = y + r * vsh
        pieces = []
        for blk in range(n_sub):
            r0, r1 = blk * SUB, (blk + 1) * SUB
            first = (blk == n_sub - 1) if rev else (blk == 0)
            if first:
                pieces.append(jnp.zeros((SUB, A_DK), F32))
                continue
            ref = gxm[r1 - 1:r1, :] if rev else gxm[r0:r0 + 1, :]
            earlier = (rows >= r1) if rev else (rows < r0)
            qhat = qh[r0:r1, :] * jnp.exp(gcm[r0:r1, :] - ref)
            khat = kh * jnp.exp(jnp.where(earlier, ref - gcm, NEG_INF))
            sc = _dot_nt(qhat, khat)
            pieces.append(_dot(sc, vh))
        y = y + jnp.concatenate(pieces, axis=0)
        gtot = gcm[0:1, :] if rev else gcm[CHUNK - 1:CHUNK, :]
        st_s[h] = st * jnp.exp(gtot) + _dot_tn(vh, kh * jnp.exp(gtot - gcm))
        y_s[h] = y
        return carry

    lax.fori_loop(0, A_HEADS, head, 0)
    for h in range(A_HEADS):
        y_ref[0, :, h * A_DK:(h + 1) * A_DK] = y_s[h]


def hgrn2_scan(u_ab, lb_tab, n_ctx, rev):
    nb, length, _ = u_ab.shape
    n_chunks = length // CHUNK
    n_ctx_chunks = n_ctx // CHUNK
    n_lat_chunks = n_chunks - n_ctx_chunks
    blk = functools.partial(_chunk_block, n_ctx_chunks=n_ctx_chunks, n_chunks=n_lat_chunks, rev=rev)
    fcol = 2 if rev else 1

    def col(j):
        return pl.BlockSpec((1, CHUNK, A_WIDTH), lambda b, c: (b, blk(c), j))

    return pl.pallas_call(
        functools.partial(_hgrn2_kernel, rev=rev),
        grid=(nb, n_chunks),
        in_specs=[col(0), col(fcol), col(3), pl.BlockSpec((3, A_WIDTH), lambda b, c: (0, 0))],
        out_specs=pl.BlockSpec((1, CHUNK, A_WIDTH), lambda b, c: (b, blk(c), 0)),
        out_shape=jax.ShapeDtypeStruct((nb, length, A_WIDTH), F32),
        scratch_shapes=[
            pltpu.VMEM((A_HEADS, CHUNK, A_DK), F32),
            pltpu.VMEM((A_HEADS, CHUNK + 2 * PAD, A_DK), F32),
            pltpu.VMEM((A_HEADS, CHUNK + 2 * PAD, A_DK), F32),
            pltpu.VMEM((A_HEADS, CHUNK + 2 * PAD, A_DK), F32),
            pltpu.VMEM((A_HEADS, CHUNK, A_DK), F32),
            pltpu.VMEM((A_HEADS, A_DK, A_DK), F32),
            pltpu.VMEM((A_HEADS, CHUNK, A_DK), F32),
        ],
        compiler_params=_cparams(("arbitrary", "arbitrary")),
        name="hgrn2_bwd" if rev else "hgrn2_fwd",
    )(u_ab, u_ab, u_ab, lb_tab)


def _mlstm_kernel(q_ref, k_ref, v_ref, g_ref, bias_ref, h_ref, c_s, n_s, m_s, *, rev):
    c = pl.program_id(1)

    @pl.when(c == 0)
    def _():
        c_s[...] = jnp.zeros_like(c_s)
        n_s[...] = jnp.zeros_like(n_s)
        m_s[...] = jnp.zeros_like(m_s)

    d = 1 if rev else 0
    gates = g_ref[0] + bias_ref[...]
    ig_all = gates[:, 64 + 8 * d:72 + 8 * d]
    lf_all = _log_sigmoid(gates[:, 80 + 8 * d:88 + 8 * d])
    mask = _tri(rev)
    b_all = _dot_hi(mask.astype(F32), lf_all)
    lane = lax.broadcasted_iota(jnp.int32, (CHUNK, 128), 1)
    last = 0 if rev else CHUNK - 1
    for h in range(C_HEADS):
        sl = slice(h * C_DK, (h + 1) * C_DK)
        qh = q_ref[0, :, sl] * (C_DK ** -0.5)
        kh = k_ref[0, :, sl]
        vh = v_ref[0, :, sl]
        bcol = b_all[:, h:h + 1]
        igcol = ig_all[:, h:h + 1]
        acol = igcol - bcol
        xm = jnp.where(lane == 0, bcol, jnp.where(lane == 1, 1.0, 0.0))
        ym = jnp.where(lane == 0, 1.0, jnp.where(lane == 1, acol, 0.0))
        dmat = jnp.where(mask, _dot_nt_hi(xm, ym), NEG_INF)
        m0 = m_s[h:h + 1, 0:1]
        m_inter = bcol + m0
        m_i = jnp.maximum(m_inter, jnp.max(dmat, axis=1, keepdims=True))
        w_intra = jnp.exp(dmat - m_i)
        w_inter = jnp.exp(m_inter - m_i)
        s = _dot_nt(qh, kh) * w_intra
        cmat = c_s[h]
        nvec = n_s[h:h + 1, :]
        num = _dot(s, vh) + w_inter * _dot(qh, cmat)
        den = jnp.sum(s, axis=1, keepdims=True) + w_inter * jnp.sum(qh * nvec, axis=1, keepdims=True)
        h_ref[0, :, sl] = num / jnp.maximum(jnp.abs(den), jnp.exp(-m_i))
        b_last = bcol[last:last + 1, :]
        log_wj = b_last - bcol + igcol
        m_new = jnp.maximum(b_last + m0, jnp.max(log_wj, axis=0, keepdims=True))
        w0 = jnp.exp(b_last + m0 - m_new)
        wk = jnp.exp(log_wj - m_new) * kh
        c_s[h] = w0 * cmat + _dot_tn(wk, vh)
        n_s[h:h + 1, :] = w0 * nvec + jnp.sum(wk, axis=0, keepdims=True)
        m_s[h:h + 1, :] = jnp.broadcast_to(m_new, (1, 128))


def mlstm_scan(u_c, u_s, gate_bias, n_ctx, rev):
    nb, length, _ = u_c.shape
    n_chunks = length // CHUNK
    n_ctx_chunks = n_ctx // CHUNK
    blk = functools.partial(_chunk_block, n_ctx_chunks=n_ctx_chunks,
                            n_chunks=n_chunks - n_ctx_chunks, rev=rev)

    def col(j):
        return pl.BlockSpec((1, CHUNK, C_WIDTH), lambda b, c: (b, blk(c), j))

    return pl.pallas_call(
        functools.partial(_mlstm_kernel, rev=rev),
        grid=(nb, n_chunks),
        in_specs=[col(0), col(1), col(2),
                  pl.BlockSpec((1, CHUNK, U_SMALL), lambda b, c: (b, blk(c), 0)),
                  pl.BlockSpec((1, U_SMALL), lambda b, c: (0, 0))],
        out_specs=pl.BlockSpec((1, CHUNK, C_WIDTH), lambda b, c: (b, blk(c), 0)),
        out_shape=jax.ShapeDtypeStruct((nb, length, C_WIDTH), F32),
        scratch_shapes=[
            pltpu.VMEM((C_HEADS, C_DK, C_DK), F32),
            pltpu.VMEM((C_HEADS, C_DK), F32),
            pltpu.VMEM((C_HEADS, 128), F32),
        ],
        compiler_params=_cparams(("arbitrary", "arbitrary")),
        name="mlstm_bwd" if rev else "mlstm_fwd",
    )(u_c, u_c, u_c, u_s, gate_bias)


CONV_TM = 256
CONV_TN = 1024
HALO = 8


def _conv_kernel(prev_ref, cur_ref, next_ref, w_ref, b_ref, o_ref, win_s, *, tiles_ctx, tiles_all):
    i = pl.program_id(1)
    seg_start = (i == 0) | (i == tiles_ctx)
    seg_end = (i == tiles_ctx - 1) | (i == tiles_all - 1)
    win_s[0:HALO, :] = jnp.where(seg_start, 0.0, prev_ref[0])
    win_s[HALO:HALO + CONV_TM, :] = cur_ref[0]
    win_s[HALO + CONV_TM:2 * HALO + CONV_TM, :] = jnp.where(seg_end, 0.0, next_ref[0])
    acc = b_ref[...] + jnp.zeros((CONV_TM, CONV_TN), F32)
    for t in range(B_CONV):
        o = HALO + t - B_CONV // 2
        acc = acc + w_ref[t:t + 1, :] * win_s[o:o + CONV_TM, :]
    o_ref[0] = _silu(acc)


def ssd_conv(u_ab, conv_w, conv_b, n_ctx):
    nb, length, _ = u_ab.shape
    tiles_all = length // CONV_TM
    tiles_ctx = n_ctx // CONV_TM
    col0 = (U_AB - B_CONV_CH) // CONV_TN
    per = CONV_TM // HALO
    n_halo = length // HALO
    return pl.pallas_call(
        functools.partial(_conv_kernel, tiles_ctx=tiles_ctx, tiles_all=tiles_all),
        grid=(nb, tiles_all, B_CONV_CH // CONV_TN),
        in_specs=[
            pl.BlockSpec((1, HALO, CONV_TN), lambda b, i, j: (b, jnp.maximum(i * per - 1, 0), col0 + j)),
            pl.BlockSpec((1, CONV_TM, CONV_TN), lambda b, i, j: (b, i, col0 + j)),
            pl.BlockSpec((1, HALO, CONV_TN),
                         lambda b, i, j: (b, jnp.minimum((i + 1) * per, n_halo - 1), col0 + j)),
            pl.BlockSpec((B_CONV, CONV_TN), lambda b, i, j: (0, j)),
            pl.BlockSpec((1, CONV_TN), lambda b, i, j: (0, j)),
        ],
        out_specs=pl.BlockSpec((1, CONV_TM, CONV_TN), lambda b, i, j: (b, i, j)),
        out_shape=jax.ShapeDtypeStruct((nb, length, B_CONV_CH), F32),
        scratch_shapes=[pltpu.VMEM((CONV_TM + 2 * HALO, CONV_TN), F32)],
        compiler_params=_cparams(("arbitrary", "arbitrary", "arbitrary")),
        name="ssd_conv",
    )(u_ab, u_ab, u_ab, conv_w, conv_b.reshape(1, B_CONV_CH))


def _ssd_kernel(xc_ref, us_ref, bias_ref, a_ref, an_ref, e_ref, y_ref, s_s, *, rev):
    c = pl.program_id(1)

    @pl.when(c == 0)
    def _():
        s_s[...] = jnp.zeros_like(s_s)

    d = 1 if rev else 0
    mask = _tri(rev)
    tri = mask.astype(F32)
    dt_all = _softplus(us_ref[0] + bias_ref[...])
    dt_x = _dot_hi(dt_all, e_ref[...])
    cum_x = _dot_hi(tri, dt_x * a_ref[...])
    dt_n = dt_all[:, 32 * d:32 * d + B_HEADS]
    cum_n = _dot_hi(tri, dt_n * an_ref[...])
    eye = (lax.broadcasted_iota(jnp.int32, (B_HEADS, B_HEADS), 0)
           == lax.broadcasted_iota(jnp.int32, (B_HEADS, B_HEADS), 1)).astype(F32)
    cum_t = _dot_nt_hi(eye, cum_n)
    last = 0 if rev else CHUNK - 1
    gw = B_RPG * B_HEADDIM
    head_of_lane = lax.broadcasted_iota(jnp.int32, (1, gw), 1) // B_HEADDIM
    for g in range(B_GROUPS):
        xs = slice(g * gw, (g + 1) * gw)
        xdt = xc_ref[0, :, xs] * dt_x[:, xs]
        bm = xc_ref[0, :, B_WIDTH + g * B_STATE:B_WIDTH + (g + 1) * B_STATE]
        cm = xc_ref[0, :, B_WIDTH + B_GROUPS * B_STATE + g * B_STATE:
                    B_WIDTH + B_GROUPS * B_STATE + (g + 1) * B_STATE]
        cum_g = cum_x[:, xs]
        state = s_s[g]
        cb = _dot_nt(cm, bm)
        y = jnp.exp(cum_g) * _dot(cm, state)
        for r in range(B_RPG):
            hh = g * B_RPG + r
            diff = cum_n[:, hh:hh + 1] - cum_t[hh:hh + 1, :]
            lmat = cb * jnp.exp(jnp.where(mask, diff, NEG_INF))
            y = y + _dot(lmat, jnp.where(head_of_lane == r, xdt, 0.0))
        c_last = cum_g[last:last + 1, :]
        s_s[g] = jnp.exp(c_last) * state + _dot_tn(bm, xdt * jnp.exp(c_last - cum_g))
        y_ref[0, :, xs] = y


def ssd_scan(xc, u_s, dt_bias, a_x, a_n, expand, n_ctx, rev):
    nb, length, _ = xc.shape
    n_chunks = length // CHUNK
    n_ctx_chunks = n_ctx // CHUNK
    blk = functools.partial(_chunk_block, n_ctx_chunks=n_ctx_chunks,
                            n_chunks=n_chunks - n_ctx_chunks, rev=rev)
    const = lambda shape: pl.BlockSpec(shape, lambda b, c: (0, 0))
    return pl.pallas_call(
        functools.partial(_ssd_kernel, rev=rev),
        grid=(nb, n_chunks),
        in_specs=[pl.BlockSpec((1, CHUNK, B_CONV_CH), lambda b, c: (b, blk(c), 0)),
                  pl.BlockSpec((1, CHUNK, U_SMALL), lambda b, c: (b, blk(c), 0)),
                  const((1, U_SMALL)), const((1, B_WIDTH)), const((1, B_HEADS)),
                  const((U_SMALL, B_WIDTH))],
        out_specs=pl.BlockSpec((1, CHUNK, B_WIDTH), lambda b, c: (b, blk(c), 0)),
        out_shape=jax.ShapeDtypeStruct((nb, length, B_WIDTH), F32),
        scratch_shapes=[pltpu.VMEM((B_GROUPS, B_STATE, B_RPG * B_HEADDIM), F32)],
        compiler_params=_cparams(("arbitrary", "arbitrary")),
        name="ssd_bwd" if rev else "ssd_fwd",
    )(xc, u_s, dt_bias, a_x, a_n, expand)


def _group_rmsnorm(y, width):
    out = []
    for g in range(y.shape[1] // width):
        yg = y[:, g * width:(g + 1) * width]
        out.append(yg * lax.rsqrt(jnp.mean(yg * yg, axis=1, keepdims=True) + EPS))
    return jnp.concatenate(out, axis=1)


def _mix_out_kernel(af_ref, ab_ref, ag_ref, bf_ref, bb_ref, bx_ref, bz0_ref, bz1_ref,
                    cf_ref, cb_ref, co_ref, wa_ref, wb_ref, wc_ref, dsk_ref, o_ref):
    ya = _group_rmsnorm(af_ref[...] + ab_ref[...], A_DK) * wa_ref[...] * _silu(ag_ref[...])
    o_ref[:, 0:A_WIDTH] = ya.astype(o_ref.dtype)
    z = jnp.concatenate([bz0_ref[...], bz1_ref[...]], axis=1)
    yb = (bf_ref[...] + bb_ref[...] + dsk_ref[...] * bx_ref[...]) * _silu(z)
    yb = _group_rmsnorm(yb, B_WIDTH // B_GROUPS) * wb_ref[...]
    o_ref[:, A_WIDTH:A_WIDTH + B_WIDTH] = yb.astype(o_ref.dtype)
    yc = (cf_ref[...] + cb_ref[...]) * jax.nn.sigmoid(co_ref[...])
    yc = _group_rmsnorm(yc, C_DK) * wc_ref[...]
    o_ref[:, A_WIDTH + B_WIDTH:] = yc.astype(o_ref.dtype)


def mix_out(ya_f, ya_b, yb_f, yb_b, yc_f, yc_b, u_ab, xc, u_c, wa, wb, wc, dskip_x, tm=128):
    t = ya_f.shape[0]
    row = lambda w, j=0: pl.BlockSpec((tm, w), lambda i: (i, j))
    vec = lambda w: pl.BlockSpec((1, w), lambda i: (0, 0))
    return pl.pallas_call(
        _mix_out_kernel,
        grid=(t // tm,),
        in_specs=[row(A_WIDTH), row(A_WIDTH), row(A_WIDTH, 4),
                  row(B_WIDTH), row(B_WIDTH), row(B_WIDTH), row(1024, 5), row(1024, 6),
                  row(C_WIDTH), row(C_WIDTH), row(C_WIDTH, 3),
                  vec(A_WIDTH), vec(B_WIDTH), vec(C_WIDTH), vec(B_WIDTH)],
        out_specs=pl.BlockSpec((tm, D_MODEL), lambda i: (i, 0)),
        out_shape=jax.ShapeDtypeStruct((t, D_MODEL), BF16),
        compiler_params=_cparams(("arbitrary",)),
        name="mix_out",
    )(ya_f, ya_b, u_ab, yb_f, yb_b, xc, u_ab, u_ab, yc_f, yc_b, u_c, wa, wb, wc, dskip_x)


def _first_argmax(vals, lane):
    m = jnp.max(vals, axis=1, keepdims=True)
    idx = jnp.min(jnp.where(vals == m, lane, 1 << 20), axis=1, keepdims=True)
    return m, idx


def _route_kernel(lg_ref, bias_ref, idx_ref, w_ref):
    logits = lg_ref[...]
    tm = logits.shape[0]
    lane = lax.broadcasted_iota(jnp.int32, (tm, 128), 1)
    is_expert = lane < N_EXPERTS
    scores = jax.nn.sigmoid(logits)
    biased = jnp.where(is_expert, scores + bias_ref[...], NEG_INF)
    per_group = N_EXPERTS // N_EXPERT_GROUPS
    grp = lane // per_group
    gscore = jnp.full((tm, 128), NEG_INF, F32)
    for g in range(N_EXPERT_GROUPS):
        vals = jnp.where(grp == g, biased, NEG_INF)
        m1, i1 = _first_argmax(vals, lane)
        m2 = jnp.max(jnp.where(lane == i1, NEG_INF, vals), axis=1, keepdims=True)
        gscore = jnp.where(lane == g, m1 + m2, gscore)
    allowed = jnp.zeros((tm, 128), jnp.bool_)
    for _ in range(TOPK_GROUPS):
        _, gi = _first_argmax(gscore, lane)
        allowed = allowed | (grp == gi)
        gscore = jnp.where(lane == gi, NEG_INF, gscore)
    masked = jnp.where(allowed & is_expert, biased, NEG_INF)
    idx_out = jnp.zeros((tm, 128), jnp.int32)
    w_out = jnp.zeros((tm, 128), F32)
    for kk in range(TOP_K):
        _, ei = _first_argmax(masked, lane)
        sel = lane == ei
        wk = jnp.sum(jnp.where(sel, scores, 0.0), axis=1, keepdims=True)
        idx_out = jnp.where(lane == kk, ei, idx_out)
        w_out = jnp.where(lane == kk, wk, w_out)
        masked = jnp.where(sel, NEG_INF, masked)
    w_out = w_out / jnp.sum(w_out, axis=1, keepdims=True) * ROUTED_SCALE
    idx_ref[...] = idx_out
    w_ref[...] = w_out


def route(logits, router_bias, tm=256):
    t = logits.shape[0]
    bias = jnp.concatenate([router_bias.astype(F32), jnp.zeros((128 - N_EXPERTS,), F32)])[None]
    row = pl.BlockSpec((tm, 128), lambda i: (i, 0))
    return pl.pallas_call(
        _route_kernel,
        grid=(t // tm,),
        in_specs=[row, pl.BlockSpec((1, 128), lambda i: (0, 0))],
        out_specs=[row, row],
        out_shape=[jax.ShapeDtypeStruct((t, 128), jnp.int32), jax.ShapeDtypeStruct((t, 128), F32)],
        compiler_params=_cparams(("arbitrary",)),
        name="route",
    )(logits, bias)


EXP_TM = 256


def _start_row_gather(tok_ref, src_hbm, dst, sem, n):
    def body(r, carry):
        pltpu.make_async_copy(src_hbm.at[pl.ds(tok_ref[0, r], 1), :], dst.at[pl.ds(r, 1), :], sem).start()
        return carry
    lax.fori_loop(0, n, body, 0, unroll=8)


def _expert_kernel(te_ref, nu_ref, tok_ref, tokn_ref, x_hbm, wg_ref, wu_ref, wd_ref, y_ref,
                   xbuf, sem, wgu_s, wd_s):
    i = pl.program_id(0)
    n_used = nu_ref[0]
    slot = i % 2
    half = xbuf.shape[2]

    @pl.when(i == 0)
    def _():
        _start_row_gather(tok_ref, x_hbm, xbuf.at[0], sem.at[0], EXP_TM)

    @pl.when(i + 1 < n_used)
    def _():
        _start_row_gather(tokn_ref, x_hbm, xbuf.at[1 - slot], sem.at[1 - slot], EXP_TM)

    new_expert = (i == 0) | (te_ref[i] != te_ref[jnp.maximum(i - 1, 0)])

    @pl.when(new_expert & (i < n_used))
    def _():
        wgu_s[:, 0:D_EXPERT] = wg_ref[...].astype(BF16)
        wgu_s[:, D_EXPERT:2 * D_EXPERT] = wu_ref[...].astype(BF16)
        wd_s[...] = wd_ref[...].astype(BF16)

    @pl.when(i < n_used)
    def _():
        pltpu.make_async_copy(x_hbm.at[pl.ds(0, EXP_TM), :], xbuf.at[slot], sem.at[slot]).wait()
        x_hi, x_lo = _unpack_bf16_pair(xbuf[slot])
        h = (jnp.dot(x_hi.astype(BF16), wgu_s[0:half, :], preferred_element_type=F32)
             + jnp.dot(x_lo.astype(BF16), wgu_s[half:2 * half, :], preferred_element_type=F32))
        act = _silu(h[:, 0:D_EXPERT]) * h[:, D_EXPERT:2 * D_EXPERT]
        y = jnp.dot(act.astype(BF16), wd_s[...], preferred_element_type=F32)
        y_ref[...] = _pack_bf16_pair(y[:, 0:half], y[:, half:2 * half])

    @pl.when(i >= n_used)
    def _():
        y_ref[...] = jnp.zeros_like(y_ref)


def routed_experts(hp, tile_expert, n_used, row_token, w_gate, w_up, w_down):
    n_tiles = row_token.shape[0]
    half = hp.shape[1]
    d = 2 * half
    last = n_tiles - 1

    def used(i, nu):
        return jnp.minimum(i, nu[0] - 1)

    grid_spec = pltpu.PrefetchScalarGridSpec(
        num_scalar_prefetch=2,
        grid=(n_tiles,),
        in_specs=[
            pl.BlockSpec((None, 1, EXP_TM), lambda i, te, nu: (i, 0, 0), memory_space=pltpu.SMEM),
            pl.BlockSpec((None, 1, EXP_TM), lambda i, te, nu: (jnp.minimum(i + 1, last), 0, 0),
                         memory_space=pltpu.SMEM),
            pl.BlockSpec(memory_space=pl.ANY),
            pl.BlockSpec((None, d, D_EXPERT), lambda i, te, nu: (te[used(i, nu)], 0, 0)),
            pl.BlockSpec((None, d, D_EXPERT), lambda i, te, nu: (te[used(i, nu)], 0, 0)),
            pl.BlockSpec((None, D_EXPERT, d), lambda i, te, nu: (te[used(i, nu)], 0, 0)),
        ],
        out_specs=pl.BlockSpec((EXP_TM, half), lambda i, te, nu: (i, 0)),
        scratch_shapes=[
            pltpu.VMEM((2, EXP_TM, half), jnp.uint32),
            pltpu.SemaphoreType.DMA((2,)),
            pltpu.VMEM((d, 2 * D_EXPERT), BF16),
            pltpu.VMEM((D_EXPERT, d), BF16),
        ],
    )
    return pl.pallas_call(
        _expert_kernel,
        grid_spec=grid_spec,
        out_shape=jax.ShapeDtypeStruct((n_tiles * EXP_TM, half), jnp.uint32),
        compiler_params=_cparams(("arbitrary",)),
        name="routed_experts",
    )(tile_expert, n_used, row_token, row_token, hp, w_gate, w_up, w_down)


CMB_TM = 128


def _start_combine_gather(pos_ref, y_hbm, dst, sem):
    def body(r, carry):
        for kk in range(TOP_K):
            p = pos_ref[0, r * TOP_K + kk]
            pltpu.make_async_copy(y_hbm.at[pl.ds(p, 1), :], dst.at[kk, pl.ds(r, 1), :], sem).start()
        return carry
    lax.fori_loop(0, CMB_TM, body, 0)


def _combine_kernel(pos_ref, posn_ref, y_hbm, w_ref, hp_ref, x_ref, gate_ref, sgu_ref, sd_ref,
                    o_ref, ybuf, sem):
    i = pl.program_id(0)
    n = pl.num_programs(0)
    slot = i % 2
    half = hp_ref.shape[1]

    @pl.when(i == 0)
    def _():
        _start_combine_gather(pos_ref, y_hbm, ybuf.at[0], sem.at[0])

    @pl.when(i + 1 < n)
    def _():
        _start_combine_gather(posn_ref, y_hbm, ybuf.at[1 - slot], sem.at[1 - slot])

    x_hi, x_lo = _unpack_bf16_pair(hp_ref[...])
    h = (jnp.dot(x_hi.astype(BF16), sgu_ref[0:half, :], preferred_element_type=F32)
         + jnp.dot(x_lo.astype(BF16), sgu_ref[half:2 * half, :], preferred_element_type=F32))
    act = _silu(h[:, 0:D_SHARED]) * h[:, D_SHARED:2 * D_SHARED]
    shared = jnp.dot(act.astype(BF16), sd_ref[...], preferred_element_type=F32)

    for kk in range(TOP_K):
        pltpu.make_async_copy(y_hbm.at[pl.ds(0, CMB_TM), :], ybuf.at[slot, kk], sem.at[slot]).wait()
    acc_hi = shared[:, 0:half]
    acc_lo = shared[:, half:2 * half]
    w = w_ref[...]
    for kk in range(TOP_K):
        y_hi, y_lo = _unpack_bf16_pair(ybuf[slot, kk])
        wk = w[:, kk:kk + 1]
        acc_hi = acc_hi + wk * y_hi
        acc_lo = acc_lo + wk * y_lo
    g = gate_ref[...]
    o_ref[:, 0:half] = x_ref[:, 0:half] + g[:, 0:half] * acc_hi
    o_ref[:, half:2 * half] = x_ref[:, half:2 * half] + g[:, half:2 * half] * acc_lo


def moe_combine(pos, y_sorted, w, hp, x, gate, seg_of_tile, sh_gate_up, sh_down):
    t, d = x.shape
    half = d // 2
    n = t // CMB_TM
    return pl.pallas_call(
        _combine_kernel,
        grid=(n,),
        in_specs=[
            pl.BlockSpec((None, 1, CMB_TM * TOP_K), lambda i: (i, 0, 0), memory_space=pltpu.SMEM),
            pl.BlockSpec((None, 1, CMB_TM * TOP_K), lambda i: (jnp.minimum(i + 1, n - 1), 0, 0),
                         memory_space=pltpu.SMEM),
            pl.BlockSpec(memory_space=pl.ANY),
            pl.BlockSpec((CMB_TM, 128), lambda i: (i, 0)),
            pl.BlockSpec((CMB_TM, half), lambda i: (i, 0)),
            pl.BlockSpec((CMB_TM, d), lambda i: (i, 0)),
            pl.BlockSpec((None, 1, d), lambda i: (seg_of_tile(i), 0, 0)),
            pl.BlockSpec((d, 2 * D_SHARED), lambda i: (0, 0)),
            pl.BlockSpec((D_SHARED, d), lambda i: (0, 0)),
        ],
        out_specs=pl.BlockSpec((CMB_TM, d), lambda i: (i, 0)),
        out_shape=jax.ShapeDtypeStruct((t, d), F32),
        scratch_shapes=[
            pltpu.VMEM((2, TOP_K, CMB_TM, half), jnp.uint32),
            pltpu.SemaphoreType.DMA((2,)),
        ],
        compiler_params=_cparams(("arbitrary",)),
        name="moe_combine",
    )(pos, pos, y_sorted, w, hp, x, gate, sh_gate_up, sh_down)


def moe_plan(eidx):
    t = eidx.shape[0]
    n_assign = t * TOP_K
    n_tiles = n_assign // EXP_TM + N_EXPERTS
    flat_e = eidx.reshape(-1).astype(jnp.int32)
    order = jnp.argsort(flat_e, stable=True).astype(jnp.int32)
    sorted_e = flat_e[order]
    counts = jnp.sum((flat_e[:, None] == jnp.arange(N_EXPERTS)[None, :]).astype(jnp.int32), axis=0)
    tiles_per = (counts + EXP_TM - 1) // EXP_TM
    tile_end = jnp.cumsum(tiles_per)
    tile_start = tile_end - tiles_per
    start = jnp.cumsum(counts) - counts
    rank = jnp.arange(n_assign, dtype=jnp.int32) - start[sorted_e]
    dest = tile_start[sorted_e] * EXP_TM + rank
    row_token = jnp.zeros((n_tiles * EXP_TM,), jnp.int32).at[dest].set(order // TOP_K)
    pos = jnp.zeros((n_assign,), jnp.int32).at[order].set(dest)
    n_used = tile_end[-1:].astype(jnp.int32)
    tile_expert = jnp.searchsorted(tile_end, jnp.arange(n_tiles, dtype=jnp.int32), side="right")
    tile_expert = jnp.minimum(tile_expert, N_EXPERTS - 1).astype(jnp.int32)
    return tile_expert, n_used, row_token.reshape(n_tiles, 1, EXP_TM), pos.reshape(t, TOP_K)


def _rmsnorm_kernel(x_ref, g_ref, o_ref):
    x = x_ref[...]
    o_ref[...] = x * lax.rsqrt(jnp.mean(x * x, axis=-1, keepdims=True) + EPS) * g_ref[...]


def final_rmsnorm(x, g, tm=256):
    t, d = x.shape
    return pl.pallas_call(
        _rmsnorm_kernel,
        grid=(t // tm,),
        in_specs=[pl.BlockSpec((tm, d), lambda i: (i, 0)), pl.BlockSpec((1, d), lambda i: (0, 0))],
        out_specs=pl.BlockSpec((tm, d), lambda i: (i, 0)),
        out_shape=jax.ShapeDtypeStruct((t, d), F32),
        compiler_params=_cparams(("arbitrary",)),
        name="final_rmsnorm",
    )(x, g.reshape(1, d))


def split_u(u):
    u_ab = u[..., :U_AB]
    u_c = u[..., U_C0:U_G0]
    pad = jnp.zeros(u.shape[:-1] + (U_SMALL - 96,), u.dtype)
    u_s = jnp.concatenate([u[..., U_DT0:U_C0], u[..., U_G0:], pad], axis=-1)
    return u_ab, u_c, u_s


def lb_table(lb):
    lb = lb.astype(F32)
    return jnp.stack([jnp.log(lb), jnp.log1p(-lb), 1.0 - lb])


def mlstm_gate_bias(i_bias, f_bias):
    z = jnp.zeros((64,), F32)
    return jnp.concatenate([z, i_bias.astype(F32).reshape(-1), f_bias.astype(F32).reshape(-1),
                            jnp.zeros((32,), F32)])[None]


def ssd_consts(dt_bias, a_log):
    bias = jnp.concatenate([dt_bias.astype(F32).reshape(-1), jnp.zeros((64,), F32)])[None]
    a = -jnp.exp(a_log.astype(F32))
    lane = jnp.arange(U_SMALL)[:, None]
    head_of_col = jnp.arange(B_WIDTH)[None, :] // B_HEADDIM
    out = []
    for d in range(2):
        expand = (lane == 32 * d + head_of_col).astype(F32)
        out.append((bias, jnp.repeat(a[d], B_HEADDIM)[None], a[d][None], expand))
    return out


def _mod_tables(mod, nb):
    d = mod.shape[1] // 6
    parts = mod.reshape(mod.shape[0], 6, d)
    lat = parts[:nb]
    ctx = jnp.broadcast_to(parts[nb][None], (nb, 6, d))
    tab = jnp.stack([ctx, lat], axis=1).reshape(2 * nb, 6, d)
    return jnp.transpose(tab, (1, 0, 2))[:, :, None, :]


def _seg_fn(rows_per_batch, ctx_rows, tm):
    tiles_per_batch = rows_per_batch // tm
    ctx_tiles = ctx_rows // tm

    def seg(i):
        return 2 * (i // tiles_per_batch) + jnp.where(i % tiles_per_batch >= ctx_tiles, 1, 0)
    return seg


def _to_scan_order(a, rows):
    nb, _, d = a.shape
    return a.reshape(nb, rows, GRID_W, d).transpose(0, 2, 1, 3).reshape(nb, rows * GRID_W, d)


def _from_scan_order(a, rows):
    nb, _, d = a.shape
    return a.reshape(nb, GRID_W, rows, d).transpose(0, 2, 1, 3).reshape(nb, rows * GRID_W, d)


def _moe_block(x2, tab, seg_mod, seg_cmb, norm_w, router_w, router_bias, w_gate, w_up, w_down,
               sh_gate, sh_up, sh_down):
    t, d = x2.shape
    rw = jnp.concatenate([router_w.astype(F32), jnp.zeros((d, 128 - N_EXPERTS), F32)], axis=1)
    hp, logits = modulate(x2, norm_w, tab[3], tab[4], seg_mod, 256, router_w=rw)
    eidx, ew = route(logits, router_bias)
    tile_expert, n_used, row_token, pos = moe_plan(eidx[:, :TOP_K])
    y_sorted = routed_experts(hp, tile_expert, n_used, row_token, w_gate, w_up, w_down)
    sgu = jnp.concatenate([sh_gate, sh_up], axis=1).astype(BF16)
    return moe_combine(pos.reshape(t // CMB_TM, 1, CMB_TM * TOP_K), y_sorted, ew, hp, x2, tab[5],
                       seg_cmb, sgu, sh_down.astype(BF16))


def kernel(x, c, ctx, c_ctx, ada_w, ada_b, norm_mix, norm_ffn, norm_final, w_in, w_out, hgrn_lb,
           hgrn_norm, ssm_conv_w, ssm_conv_b, ssm_dt_bias, ssm_a_log, ssm_d, ssm_norm, mlstm_i_bias,
           mlstm_f_bias, mlstm_norm, router_w, router_bias, moe_w_gate, moe_w_up, moe_w_down,
           shared_w_gate, shared_w_up, shared_w_down):
    nb, seq, d = x.shape
    n_ctx = ctx.shape[1]
    length = n_ctx + seq
    rows = seq // GRID_W
    depth = ada_w.shape[0]
    t_all = nb * length
    t_lat = nb * seq

    lb_all = jnp.cumsum(jax.nn.softmax(hgrn_lb.astype(F32), axis=0), axis=0)
    lb_all = lb_all - lb_all[0]
    cond = jnp.concatenate([c, c_ctx[None], jnp.zeros((8 - nb - 1, d), F32)], axis=0)
    mod = ada_modulation(cond, ada_w, ada_b)

    xs = jnp.concatenate([ctx, x], axis=1)
    for l in range(depth):
        last = l == depth - 1
        tab = _mod_tables(mod[l], nb)
        h = modulate(xs.reshape(t_all, d), norm_mix[l], tab[0], tab[1], _seg_fn(length, n_ctx, 256), 256)
        if l % 2 == 1:
            h3 = h.reshape(nb, length, d)
            h = jnp.concatenate([h3[:, :n_ctx], _to_scan_order(h3[:, n_ctx:], rows)], axis=1)
            h = h.reshape(t_all, d)
        w = w_in[l]
        w_small = jnp.concatenate([w[:, U_DT0:U_C0], w[:, U_G0:], jnp.zeros((d, U_SMALL - 96), F32)], axis=1)
        u_ab = matmul(h, w, tm=1024, tn=512, n_tiles=U_AB // 512, name="in_proj_ab")
        u_c = matmul(h, w[:, U_C0:U_G0], tm=1024, tn=512, name="in_proj_c")
        u_s = matmul(h, w_small, tm=1024, tn=U_SMALL, name="in_proj_small")
        u_ab3 = u_ab.reshape(nb, length, U_AB)
        u_c3 = u_c.reshape(nb, length, 4 * C_WIDTH)
        u_s3 = u_s.reshape(nb, length, U_SMALL)

        xc = ssd_conv(u_ab3, ssm_conv_w[l], ssm_conv_b[l], n_ctx)
        consts = ssd_consts(ssm_dt_bias[l], ssm_a_log[l])
        gate_bias = mlstm_gate_bias(mlstm_i_bias[l], mlstm_f_bias[l])
        ya = [hgrn2_scan(u_ab3, lb_table(lb_all[l][dd]), n_ctx, rev=bool(dd)) for dd in range(2)]
        yb = [ssd_scan(xc, u_s3, *consts[dd], n_ctx, rev=bool(dd)) for dd in range(2)]
        yc = [mlstm_scan(u_c3, u_s3, gate_bias, n_ctx, rev=bool(dd)) for dd in range(2)]
        flat = lambda a: a.reshape(t_all, a.shape[-1])
        ymix = mix_out(flat(ya[0]), flat(ya[1]), flat(yb[0]), flat(yb[1]), flat(yc[0]), flat(yc[1]),
                       u_ab, flat(xc), u_c, hgrn_norm[l][None], ssm_norm[l][None], mlstm_norm[l][None],
                       jnp.repeat(ssm_d[l].astype(F32), B_HEADDIM)[None])
        if l % 2 == 1:
            y3 = ymix.reshape(nb, length, d)
            ymix = jnp.concatenate([y3[:, :n_ctx], _from_scan_order(y3[:, n_ctx:], rows)], axis=1)
            ymix = ymix.reshape(t_all, d)

        moe_w = (norm_ffn[l], router_w[l], router_bias[l], moe_w_gate[l], moe_w_up[l], moe_w_down[l],
                 shared_w_gate[l], shared_w_up[l], shared_w_down[l])
        if not last:
            x2 = matmul_residual(ymix, w_out[l], xs.reshape(t_all, d), tab[2],
                                 _seg_fn(length, n_ctx, 256), tm=256, tn=512)
            x2 = _moe_block(x2, tab, _seg_fn(length, n_ctx, 256), _seg_fn(length, n_ctx, CMB_TM), *moe_w)
            xs = x2.reshape(nb, length, d)
        else:
            y_lat = ymix.reshape(nb, length, d)[:, n_ctx:].reshape(t_lat, d)
            x2 = matmul_residual(y_lat, w_out[l], xs[:, n_ctx:].reshape(t_lat, d), tab[2],
                                 _seg_fn(seq, 0, 256), tm=256, tn=512)
            x2 = _moe_block(x2, tab, _seg_fn(seq, 0, 256), _seg_fn(seq, 0, CMB_TM), *moe_w)
            return final_rmsnorm(x2, norm_final).reshape(nb, seq, d)
```

```python
import functools

import jax
import jax.numpy as jnp
from jax import lax
from jax.experimental import pallas as pl
from jax.experimental.pallas import tpu as pltpu

F32 = jnp.float32
BF16 = jnp.bfloat16
HIGHEST = lax.Precision.HIGHEST
NEG_INF = float("-inf")

D_MODEL = 4096
GRID_W = 64
CHUNK = 64
EPS = 1e-6
A_WIDTH = 1024
A_DK = 128
A_HEADS = 8
B_WIDTH = 2048
B_HEADDIM = 64
B_HEADS = 32
B_GROUPS = 8
B_RPG = 4
B_STATE = 128
B_CONV = 5
B_CONV_CH = 4096
C_WIDTH = 1024
C_DK = 128
C_HEADS = 8
N_EXPERTS = 64
N_EXPERT_GROUPS = 8
TOPK_GROUPS = 4
TOP_K = 8
D_EXPERT = 256
D_SHARED = 256
ROUTED_SCALE = 2.5

U_AB = 11264
U_DT0 = 11264
U_C0 = 11328
U_G0 = 15424
U_SMALL = 128

SUB = 16
PAD = 16
VMEM_LIMIT = 56 * 1024 * 1024


def _cparams(sem):
    return pltpu.CompilerParams(dimension_semantics=sem, vmem_limit_bytes=VMEM_LIMIT)


def _dot(a, b):
    return jnp.dot(a.astype(BF16), b.astype(BF16), preferred_element_type=F32)


def _bdot(a, b, ca, cb):
    return lax.dot_general(a.astype(BF16), b.astype(BF16), (((ca,), (cb,)), ((0,), (0,))),
                           preferred_element_type=F32)


def _dot_hi(a, b):
    return jnp.dot(a, b, precision=HIGHEST, preferred_element_type=F32)


def _dot_nt_hi(a, b):
    return lax.dot_general(a, b, (((1,), (1,)), ((), ())), precision=HIGHEST,
                           preferred_element_type=F32)


def _log_sigmoid(z):
    return jnp.minimum(z, 0.0) - jnp.log1p(jnp.exp(-jnp.abs(z)))


def _softplus(z):
    return jnp.maximum(z, 0.0) + jnp.log1p(jnp.exp(-jnp.abs(z)))


def _silu(z):
    return z * jax.nn.sigmoid(z)


def _tri(rev):
    ri = lax.broadcasted_iota(jnp.int32, (CHUNK, CHUNK), 0)
    ci = lax.broadcasted_iota(jnp.int32, (CHUNK, CHUNK), 1)
    return (ci >= ri) if rev else (ci <= ri)


def _chunk_block(c, n_ctx_chunks, n_lat_chunks, rev):
    if not rev:
        return c
    return jnp.where(c < n_ctx_chunks, n_ctx_chunks - 1 - c, 2 * n_ctx_chunks + n_lat_chunks - 1 - c)


def _ada_kernel(c_ref, w_ref, b_ref, o_ref):
    s = _silu(c_ref[...])
    o_ref[...] = _dot(s, w_ref[...]) + b_ref[...]


def ada_modulation(cond, ada_w, ada_b):
    depth, d, n = ada_w.shape
    tn = 512
    return pl.pallas_call(
        _ada_kernel,
        grid=(depth, n // tn),
        in_specs=[
            pl.BlockSpec((8, d), lambda l, j: (0, 0)),
            pl.BlockSpec((None, d, tn), lambda l, j: (l, 0, j)),
            pl.BlockSpec((None, 1, tn), lambda l, j: (l, 0, j)),
        ],
        out_specs=pl.BlockSpec((None, 8, tn), lambda l, j: (l, 0, j)),
        out_shape=jax.ShapeDtypeStruct((depth, 8, n), F32),
        compiler_params=_cparams(("arbitrary", "arbitrary")),
        name="ada_modulation",
    )(cond, ada_w, ada_b.reshape(depth, 1, n))


def _pack_bf16_pair(a, b):
    hi = lax.bitcast_convert_type(a.astype(BF16).astype(F32), jnp.uint32)
    lo = lax.bitcast_convert_type(b.astype(BF16).astype(F32), jnp.uint32)
    return hi | (lo >> 16)


def _unpack_bf16_pair(w):
    hi = lax.bitcast_convert_type(w & jnp.uint32(0xFFFF0000), F32)
    lo = lax.bitcast_convert_type(w << 16, F32)
    return hi, lo


def _modulate_kernel(x_ref, g_ref, sh_ref, sc_ref, *rest, with_router):
    x = x_ref[...]
    y = x * lax.rsqrt(jnp.mean(x * x, axis=-1, keepdims=True) + EPS) * g_ref[...]
    h = y * (1.0 + sc_ref[...]) + sh_ref[...]
    if with_router:
        rw_ref, h_ref, lg_ref = rest
        lg_ref[...] = _dot_hi(h, rw_ref[...])
        half = h.shape[1] // 2
        h_ref[...] = _pack_bf16_pair(h[:, :half], h[:, half:])
    else:
        (h_ref,) = rest
        h_ref[...] = h.astype(h_ref.dtype)


def modulate(x, g, shift, scale, seg_of_tile, tm, router_w=None):
    t, d = x.shape
    with_router = router_w is not None
    vec = pl.BlockSpec((None, 1, d), lambda i: (seg_of_tile(i), 0, 0))
    in_specs = [pl.BlockSpec((tm, d), lambda i: (i, 0)),
                pl.BlockSpec((1, d), lambda i: (0, 0)), vec, vec]
    args = [x, g.reshape(1, d), shift, scale]
    if with_router:
        in_specs.append(pl.BlockSpec((d, 128), lambda i: (0, 0)))
        args.append(router_w)
        out_specs = [pl.BlockSpec((tm, d // 2), lambda i: (i, 0)),
                     pl.BlockSpec((tm, 128), lambda i: (i, 0))]
        out_shape = [jax.ShapeDtypeStruct((t, d // 2), jnp.uint32),
                     jax.ShapeDtypeStruct((t, 128), F32)]
    else:
        out_specs = pl.BlockSpec((tm, d), lambda i: (i, 0))
        out_shape = jax.ShapeDtypeStruct((t, d), BF16)
    return pl.pallas_call(
        functools.partial(_modulate_kernel, with_router=with_router),
        grid=(t // tm,),
        in_specs=in_specs, out_specs=out_specs, out_shape=out_shape,
        compiler_params=_cparams(("arbitrary",)),
        name="modulate_router" if with_router else "modulate",
    )(*args)


def _mm_kernel(a_ref, b_ref, o_ref):
    o_ref[...] = jnp.dot(a_ref[...], b_ref[...].astype(BF16), preferred_element_type=F32)


def matmul(a, b, layer, *, tm, tn, n_tiles=None, name="matmul"):
    m, k = a.shape
    n_tiles = b.shape[2] // tn if n_tiles is None else n_tiles
    return pl.pallas_call(
        _mm_kernel,
        grid=(m // tm, n_tiles),
        in_specs=[pl.BlockSpec((tm, k), lambda i, j: (i, 0)),
                  pl.BlockSpec((None, k, tn), lambda i, j: (layer, 0, j))],
        out_specs=pl.BlockSpec((tm, tn), lambda i, j: (i, j)),
        out_shape=jax.ShapeDtypeStruct((m, n_tiles * tn), F32),
        compiler_params=_cparams(("arbitrary", "arbitrary")),
        name=name,
    )(a, b)


def _mm_res_kernel(a_ref, b_ref, r_ref, g_ref, o_ref, b_s):
    @pl.when(pl.program_id(1) == 0)
    def _():
        b_s[...] = b_ref[...].astype(BF16)

    acc = jnp.dot(a_ref[...], b_s[...], preferred_element_type=F32)
    o_ref[...] = r_ref[...] + g_ref[...] * acc


def matmul_residual(a, b, layer, res, gate, seg_of_tile, *, tm, tn):
    m, k = a.shape
    n = b.shape[2]
    return pl.pallas_call(
        _mm_res_kernel,
        grid=(n // tn, m // tm),
        in_specs=[pl.BlockSpec((tm, k), lambda j, i: (i, 0)),
                  pl.BlockSpec((None, k, tn), lambda j, i: (layer, 0, j)),
                  pl.BlockSpec((tm, tn), lambda j, i: (i, j)),
                  pl.BlockSpec((None, 1, tn), lambda j, i: (seg_of_tile(i), 0, j))],
        out_specs=pl.BlockSpec((tm, tn), lambda j, i: (i, j)),
        out_shape=jax.ShapeDtypeStruct((m, n), F32),
        scratch_shapes=[pltpu.VMEM((k, tn), BF16)],
        compiler_params=_cparams(("arbitrary", "arbitrary")),
        name="matmul_residual",
    )(a, b, res, gate)


def _hgrn2_kernel(q_ref, f_ref, v_ref, lb_ref, y_ref, k_s, v_s, gc_s, st_s, *, rev):
    c = pl.program_id(1)

    @pl.when(c == 0)
    def _():
        st_s[...] = jnp.zeros_like(st_s)
        k_s[...] = jnp.zeros_like(k_s)
        v_s[...] = jnp.zeros_like(v_s)
        gc_s[...] = jnp.zeros_like(gc_s)

    nh = A_HEADS
    q = q_ref[0]
    z = f_ref[0]
    log_lb = lb_ref[0:1, :]
    log1m_lb = lb_ref[1:2, :]
    one_m_lb = lb_ref[2:3, :]
    b2 = log1m_lb + _log_sigmoid(z)
    logf = jnp.maximum(log_lb, b2) + jnp.log1p(jnp.exp(-jnp.abs(log_lb - b2)))
    gc2 = _dot_hi(_tri(rev).astype(F32), logf)
    heads = lambda a: jnp.stack([a[:, h * A_DK:(h + 1) * A_DK] for h in range(nh)])
    qh = heads(q * jax.nn.sigmoid(q) * (A_DK ** -0.5))
    kh = heads(one_m_lb * jax.nn.sigmoid(-z))
    vh = heads(v_ref[0])
    gcm = heads(gc2)
    gxm = gcm - heads(logf)
    k_s[:, PAD:PAD + CHUNK, :] = kh
    v_s[:, PAD:PAD + CHUNK, :] = vh
    gc_s[:, PAD:PAD + CHUNK, :] = gcm

    rows = lax.broadcasted_iota(jnp.int32, (1, CHUNK, 1), 1)
    rowmod = rows % SUB
    n_sub = CHUNK // SUB
    st = st_s[...]
    y = _bdot(qh * jnp.exp(gcm), st, 2, 2)
    for dlt in range(SUB):
        off = PAD + dlt if rev else PAD - dlt
        ksh = k_s[:, off:off + CHUNK, :]
        vsh = v_s[:, off:off + CHUNK, :]
        gsh = gc_s[:, off:off + CHUNK, :]
        valid = (rowmod + dlt < SUB) if rev else (rowmod >= dlt)
        e = jnp.exp(jnp.where(valid, gcm - gsh, NEG_INF))
        r = jnp.sum(qh * ksh * e, axis=2, keepdims=True)
        y = y + r * vsh
    pieces = []
    for blk in range(n_sub):
        r0, r1 = blk * SUB, (blk + 1) * SUB
        first = (blk == n_sub - 1) if rev else (blk == 0)
        if first:
            pieces.append(jnp.zeros((nh, SUB, A_DK), F32))
            continue
        ref = gxm[:, r1 - 1:r1, :] if rev else gxm[:, r0:r0 + 1, :]
        earlier = (rows >= r1) if rev else (rows < r0)
        qhat = qh[:, r0:r1, :] * jnp.exp(gcm[:, r0:r1, :] - ref)
        khat = kh * jnp.exp(jnp.where(earlier, ref - gcm, NEG_INF))
        sc = _bdot(qhat, khat, 2, 2)
        pieces.append(_bdot(sc, vh, 2, 1))
    y = y + jnp.concatenate(pieces, axis=1)
    gtot = gcm[:, 0:1, :] if rev else gcm[:, CHUNK - 1:CHUNK, :]
    st_s[...] = st * jnp.exp(gtot) + _bdot(vh, kh * jnp.exp(gtot - gcm), 1, 1)
    for h in range(nh):
        y_ref[0, :, h * A_DK:(h + 1) * A_DK] = y[h]


def hgrn2_scan(u_ab, lb_tab, n_ctx, rev):
    nb, length, _ = u_ab.shape
    n_chunks = length // CHUNK
    n_ctx_chunks = n_ctx // CHUNK
    blk = functools.partial(_chunk_block, n_ctx_chunks=n_ctx_chunks,
                            n_lat_chunks=n_chunks - n_ctx_chunks, rev=rev)
    fcol = 2 if rev else 1

    def col(j):
        return pl.BlockSpec((1, CHUNK, A_WIDTH), lambda b, c: (b, blk(c), j))

    return pl.pallas_call(
        functools.partial(_hgrn2_kernel, rev=rev),
        grid=(nb, n_chunks),
        in_specs=[col(0), col(fcol), col(3), pl.BlockSpec((3, A_WIDTH), lambda b, c: (0, 0))],
        out_specs=pl.BlockSpec((1, CHUNK, A_WIDTH), lambda b, c: (b, blk(c), 0)),
        out_shape=jax.ShapeDtypeStruct((nb, length, A_WIDTH), F32),
        scratch_shapes=[
            pltpu.VMEM((A_HEADS, CHUNK + 2 * PAD, A_DK), F32),
            pltpu.VMEM((A_HEADS, CHUNK + 2 * PAD, A_DK), F32),
            pltpu.VMEM((A_HEADS, CHUNK + 2 * PAD, A_DK), F32),
            pltpu.VMEM((A_HEADS, A_DK, A_DK), F32),
        ],
        compiler_params=_cparams(("arbitrary", "arbitrary")),
        name="hgrn2_bwd" if rev else "hgrn2_fwd",
    )(u_ab, u_ab, u_ab, lb_tab)


def _mlstm_kernel(q_ref, k_ref, v_ref, g_ref, bias_ref, h_ref, c_s, n_s, m_s, *, rev):
    c = pl.program_id(1)

    @pl.when(c == 0)
    def _():
        c_s[...] = jnp.zeros_like(c_s)
        n_s[...] = jnp.zeros_like(n_s)
        m_s[...] = jnp.zeros_like(m_s)

    d = 1 if rev else 0
    nh = C_HEADS
    gates = g_ref[0] + bias_ref[...]
    ig_all = gates[:, 64 + 8 * d:72 + 8 * d]
    lf_all = _log_sigmoid(gates[:, 80 + 8 * d:88 + 8 * d])
    mask = _tri(rev)
    b_all = _dot_hi(mask.astype(F32), lf_all)
    eye = (lax.broadcasted_iota(jnp.int32, (nh, nh), 0)
           == lax.broadcasted_iota(jnp.int32, (nh, nh), 1)).astype(F32)
    a_rows = _dot_nt_hi(eye, ig_all - b_all)
    heads = lambda ref: jnp.stack([ref[0, :, h * C_DK:(h + 1) * C_DK] for h in range(nh)])
    q = heads(q_ref) * (C_DK ** -0.5)
    k = heads(k_ref)
    v = heads(v_ref)
    bcol = jnp.stack([b_all[:, h:h + 1] for h in range(nh)])
    igcol = jnp.stack([ig_all[:, h:h + 1] for h in range(nh)])
    arow = a_rows[:, None, :]
    last = 0 if rev else CHUNK - 1
    dmat = jnp.where(mask[None], bcol + arow, NEG_INF)
    m0 = m_s[...]
    m_inter = bcol + m0
    m_i = jnp.maximum(m_inter, jnp.max(dmat, axis=2, keepdims=True))
    w_intra = jnp.exp(dmat - m_i)
    w_inter = jnp.exp(m_inter - m_i)
    s = _bdot(q, k, 2, 2) * w_intra
    cmat = c_s[...]
    nvec = n_s[...]
    num = _bdot(s, v, 2, 1) + w_inter * _bdot(q, cmat, 2, 1)
    den = jnp.sum(s, axis=2, keepdims=True) + w_inter * jnp.sum(q * nvec, axis=2, keepdims=True)
    out = num / jnp.maximum(jnp.abs(den), jnp.exp(-m_i))
    for h in range(nh):
        h_ref[0, :, h * C_DK:(h + 1) * C_DK] = out[h]
    b_last = bcol[:, last:last + 1, :]
    log_wj = b_last - bcol + igcol
    m_new = jnp.maximum(b_last + m0, jnp.max(log_wj, axis=1, keepdims=True))
    w0 = jnp.exp(b_last + m0 - m_new)
    wk = jnp.exp(log_wj - m_new) * k
    c_s[...] = w0 * cmat + _bdot(wk, v, 1, 1)
    n_s[...] = w0 * nvec + jnp.sum(wk, axis=1, keepdims=True)
    m_s[...] = m_new


def mlstm_scan(u_c, u_s, gate_bias, n_ctx, rev):
    nb, length, _ = u_c.shape
    n_chunks = length // CHUNK
    n_ctx_chunks = n_ctx // CHUNK
    blk = functools.partial(_chunk_block, n_ctx_chunks=n_ctx_chunks,
                            n_lat_chunks=n_chunks - n_ctx_chunks, rev=rev)

    def col(j):
        return pl.BlockSpec((1, CHUNK, C_WIDTH), lambda b, c: (b, blk(c), j))

    return pl.pallas_call(
        functools.partial(_mlstm_kernel, rev=rev),
        grid=(nb, n_chunks),
        in_specs=[col(0), col(1), col(2),
                  pl.BlockSpec((1, CHUNK, U_SMALL), lambda b, c: (b, blk(c), 0)),
                  pl.BlockSpec((1, U_SMALL), lambda b, c: (0, 0))],
        out_specs=pl.BlockSpec((1, CHUNK, C_WIDTH), lambda b, c: (b, blk(c), 0)),
        out_shape=jax.ShapeDtypeStruct((nb, length, C_WIDTH), F32),
        scratch_shapes=[
            pltpu.VMEM((C_HEADS, C_DK, C_DK), F32),
            pltpu.VMEM((C_HEADS, 1, C_DK), F32),
            pltpu.VMEM((C_HEADS, 1, 1), F32),
        ],
        compiler_params=_cparams(("arbitrary", "arbitrary")),
        name="mlstm_bwd" if rev else "mlstm_fwd",
    )(u_c, u_c, u_c, u_s, gate_bias)


CONV_TM = 256
CONV_TN = 1024
HALO = 8


def _conv_kernel(prev_ref, cur_ref, next_ref, w_ref, b_ref, o_ref, win_s, *, tiles_ctx, tiles_all):
    i = pl.program_id(1)
    seg_start = (i == 0) | (i == tiles_ctx)
    seg_end = (i == tiles_ctx - 1) | (i == tiles_all - 1)
    win_s[0:HALO, :] = jnp.where(seg_start, 0.0, prev_ref[0])
    win_s[HALO:HALO + CONV_TM, :] = cur_ref[0]
    win_s[HALO + CONV_TM:2 * HALO + CONV_TM, :] = jnp.where(seg_end, 0.0, next_ref[0])
    acc = b_ref[...] + jnp.zeros((CONV_TM, CONV_TN), F32)
    for t in range(B_CONV):
        o = HALO + t - B_CONV // 2
        acc = acc + w_ref[t:t + 1, :] * win_s[o:o + CONV_TM, :]
    o_ref[0] = _silu(acc)


def ssd_conv(u_ab, conv_w, conv_b, n_ctx):
    nb, length, _ = u_ab.shape
    tiles_all = length // CONV_TM
    tiles_ctx = n_ctx // CONV_TM
    col0 = (U_AB - B_CONV_CH) // CONV_TN
    per = CONV_TM // HALO
    n_halo = length // HALO
    return pl.pallas_call(
        functools.partial(_conv_kernel, tiles_ctx=tiles_ctx, tiles_all=tiles_all),
        grid=(nb, tiles_all, B_CONV_CH // CONV_TN),
        in_specs=[
            pl.BlockSpec((1, HALO, CONV_TN), lambda b, i, j: (b, jnp.maximum(i * per - 1, 0), col0 + j)),
            pl.BlockSpec((1, CONV_TM, CONV_TN), lambda b, i, j: (b, i, col0 + j)),
            pl.BlockSpec((1, HALO, CONV_TN),
                         lambda b, i, j: (b, jnp.minimum((i + 1) * per, n_halo - 1), col0 + j)),
            pl.BlockSpec((B_CONV, CONV_TN), lambda b, i, j: (0, j)),
            pl.BlockSpec((1, CONV_TN), lambda b, i, j: (0, j)),
        ],
        out_specs=pl.BlockSpec((1, CONV_TM, CONV_TN), lambda b, i, j: (b, i, j)),
        out_shape=jax.ShapeDtypeStruct((nb, length, B_CONV_CH), F32),
        scratch_shapes=[pltpu.VMEM((CONV_TM + 2 * HALO, CONV_TN), F32)],
        compiler_params=_cparams(("arbitrary", "arbitrary", "arbitrary")),
        name="ssd_conv",
    )(u_ab, u_ab, u_ab, conv_w, conv_b.reshape(1, B_CONV_CH))


def _ssd_kernel(xc_ref, us_ref, bias_ref, an_ref, e_ref, y_ref, s_s, *, rev):
    c = pl.program_id(1)

    @pl.when(c == 0)
    def _():
        s_s[...] = jnp.zeros_like(s_s)

    d = 1 if rev else 0
    ng = B_GROUPS
    gw = B_RPG * B_HEADDIM
    mask = _tri(rev)
    last = 0 if rev else CHUNK - 1
    dt_n = _softplus(us_ref[0] + bias_ref[...])[:, 32 * d:32 * d + B_HEADS]
    cum_n = _dot_hi(mask.astype(F32), dt_n * an_ref[...])
    eye = (lax.broadcasted_iota(jnp.int32, (B_HEADS, B_HEADS), 0)
           == lax.broadcasted_iota(jnp.int32, (B_HEADS, B_HEADS), 1)).astype(F32)
    cum_t = _dot_nt_hi(eye, cum_n)
    decay_in = jnp.exp(cum_n)
    decay_out = jnp.exp(cum_n[last:last + 1, :] - cum_n)
    spread = _dot(jnp.concatenate([dt_n, decay_in, decay_out], axis=0), e_ref[...])
    groups = lambda a: jnp.stack([a[:, g * gw:(g + 1) * gw] for g in range(ng)])
    dt_x = groups(spread[0:CHUNK])
    in_x = groups(spread[CHUNK:2 * CHUNK])
    out_x = groups(spread[2 * CHUNK:3 * CHUNK])
    xdt = jnp.stack([xc_ref[0, :, g * gw:(g + 1) * gw] for g in range(ng)]) * dt_x
    bm = jnp.stack([xc_ref[0, :, B_WIDTH + g * B_STATE:B_WIDTH + (g + 1) * B_STATE]
                    for g in range(ng)])
    c0 = B_WIDTH + B_GROUPS * B_STATE
    cm = jnp.stack([xc_ref[0, :, c0 + g * B_STATE:c0 + (g + 1) * B_STATE] for g in range(ng)])
    state = s_s[...]
    cb = _bdot(cm, bm, 2, 2)
    y = in_x * _bdot(cm, state, 2, 1)
    head_of_lane = lax.broadcasted_iota(jnp.int32, (1, 1, gw), 2) // B_HEADDIM
    for r in range(B_RPG):
        ccol = jnp.stack([cum_n[:, g * B_RPG + r:g * B_RPG + r + 1] for g in range(ng)])
        crow = jnp.stack([cum_t[g * B_RPG + r:g * B_RPG + r + 1, :] for g in range(ng)])
        lmat = cb * jnp.exp(jnp.where(mask[None], ccol - crow, NEG_INF))
        y = y + _bdot(lmat, jnp.where(head_of_lane == r, xdt, 0.0), 2, 1)
    s_s[...] = in_x[:, last:last + 1, :] * state + _bdot(bm, xdt * out_x, 1, 1)
    for g in range(ng):
        y_ref[0, :, g * gw:(g + 1) * gw] = y[g]


def ssd_scan(xc, u_s, dt_bias, a_n, expand, n_ctx, rev):
    nb, length, _ = xc.shape
    n_chunks = length // CHUNK
    n_ctx_chunks = n_ctx // CHUNK
    blk = functools.partial(_chunk_block, n_ctx_chunks=n_ctx_chunks,
                            n_lat_chunks=n_chunks - n_ctx_chunks, rev=rev)
    const = lambda shape: pl.BlockSpec(shape, lambda b, c: (0, 0))
    return pl.pallas_call(
        functools.partial(_ssd_kernel, rev=rev),
        grid=(nb, n_chunks),
        in_specs=[pl.BlockSpec((1, CHUNK, B_CONV_CH), lambda b, c: (b, blk(c), 0)),
                  pl.BlockSpec((1, CHUNK, U_SMALL), lambda b, c: (b, blk(c), 0)),
                  const((1, U_SMALL)), const((1, B_HEADS)), const((B_HEADS, B_WIDTH))],
        out_specs=pl.BlockSpec((1, CHUNK, B_WIDTH), lambda b, c: (b, blk(c), 0)),
        out_shape=jax.ShapeDtypeStruct((nb, length, B_WIDTH), F32),
        scratch_shapes=[pltpu.VMEM((B_GROUPS, B_STATE, B_RPG * B_HEADDIM), F32)],
        compiler_params=_cparams(("arbitrary", "arbitrary")),
        name="ssd_bwd" if rev else "ssd_fwd",
    )(xc, u_s, dt_bias, a_n, expand)


def _group_rmsnorm(y, width):
    out = []
    for g in range(y.shape[1] // width):
        yg = y[:, g * width:(g + 1) * width]
        out.append(yg * lax.rsqrt(jnp.mean(yg * yg, axis=1, keepdims=True) + EPS))
    return jnp.concatenate(out, axis=1)


def _mix_out_kernel(af_ref, ab_ref, ag_ref, bf_ref, bb_ref, bx_ref, bz0_ref, bz1_ref,
                    cf_ref, cb_ref, co_ref, wa_ref, wb_ref, wc_ref, dsk_ref, o_ref):
    ya = _group_rmsnorm(af_ref[...] + ab_ref[...], A_DK) * wa_ref[...] * _silu(ag_ref[...])
    o_ref[:, 0:A_WIDTH] = ya.astype(o_ref.dtype)
    z = jnp.concatenate([bz0_ref[...], bz1_ref[...]], axis=1)
    yb = (bf_ref[...] + bb_ref[...] + dsk_ref[...] * bx_ref[...]) * _silu(z)
    yb = _group_rmsnorm(yb, B_WIDTH // B_GROUPS) * wb_ref[...]
    o_ref[:, A_WIDTH:A_WIDTH + B_WIDTH] = yb.astype(o_ref.dtype)
    yc = (cf_ref[...] + cb_ref[...]) * jax.nn.sigmoid(co_ref[...])
    yc = _group_rmsnorm(yc, C_DK) * wc_ref[...]
    o_ref[:, A_WIDTH + B_WIDTH:] = yc.astype(o_ref.dtype)


def mix_out(ya_f, ya_b, yb_f, yb_b, yc_f, yc_b, u_ab, xc, u_c, wa, wb, wc, dskip_x, tm=128):
    t = ya_f.shape[0]
    row = lambda w, j=0: pl.BlockSpec((tm, w), lambda i: (i, j))
    vec = lambda w: pl.BlockSpec((1, w), lambda i: (0, 0))
    return pl.pallas_call(
        _mix_out_kernel,
        grid=(t // tm,),
        in_specs=[row(A_WIDTH), row(A_WIDTH), row(A_WIDTH, 4),
                  row(B_WIDTH), row(B_WIDTH), row(B_WIDTH), row(1024, 5), row(1024, 6),
                  row(C_WIDTH), row(C_WIDTH), row(C_WIDTH, 3),
                  vec(A_WIDTH), vec(B_WIDTH), vec(C_WIDTH), vec(B_WIDTH)],
        out_specs=pl.BlockSpec((tm, D_MODEL), lambda i: (i, 0)),
        out_shape=jax.ShapeDtypeStruct((t, D_MODEL), BF16),
        compiler_params=_cparams(("arbitrary",)),
        name="mix_out",
    )(ya_f, ya_b, u_ab, yb_f, yb_b, xc, u_ab, u_ab, yc_f, yc_b, u_c, wa, wb, wc, dskip_x)


def _first_argmax(vals, lane):
    m = jnp.max(vals, axis=1, keepdims=True)
    idx = jnp.min(jnp.where(vals == m, lane, 1 << 20), axis=1, keepdims=True)
    return m, idx


def _route_kernel(lg_ref, bias_ref, idx_ref, w_ref, rank_ref, cnt_ref, run_s):
    @pl.when(pl.program_id(0) == 0)
    def _():
        run_s[...] = jnp.zeros_like(run_s)

    logits = lg_ref[...]
    tm = logits.shape[0]
    lane = lax.broadcasted_iota(jnp.int32, (tm, 128), 1)
    is_expert = lane < N_EXPERTS
    scores = jax.nn.sigmoid(logits)
    biased = jnp.where(is_expert, scores + bias_ref[...], NEG_INF)
    per_group = N_EXPERTS // N_EXPERT_GROUPS
    grp = lane // per_group
    gscore = jnp.full((tm, 128), NEG_INF, F32)
    for g in range(N_EXPERT_GROUPS):
        vals = jnp.where(grp == g, biased, NEG_INF)
        m1, i1 = _first_argmax(vals, lane)
        m2 = jnp.max(jnp.where(lane == i1, NEG_INF, vals), axis=1, keepdims=True)
        gscore = jnp.where(lane == g, m1 + m2, gscore)
    allowed = jnp.zeros((tm, 128), jnp.bool_)
    for _ in range(TOPK_GROUPS):
        _, gi = _first_argmax(gscore, lane)
        allowed = allowed | (grp == gi)
        gscore = jnp.where(lane == gi, NEG_INF, gscore)
    masked = jnp.where(allowed & is_expert, biased, NEG_INF)
    idx_out = jnp.zeros((tm, 128), jnp.int32)
    w_out = jnp.zeros((tm, 128), F32)
    picks = []
    chosen = jnp.zeros((tm, 128), F32)
    for kk in range(TOP_K):
        _, ei = _first_argmax(masked, lane)
        sel = lane == ei
        picks.append(sel)
        chosen = chosen + sel.astype(F32)
        wk = jnp.sum(jnp.where(sel, scores, 0.0), axis=1, keepdims=True)
        idx_out = jnp.where(lane == kk, ei, idx_out)
        w_out = jnp.where(lane == kk, wk, w_out)
        masked = jnp.where(sel, NEG_INF, masked)
    w_out = w_out / jnp.sum(w_out, axis=1, keepdims=True) * ROUTED_SCALE
    idx_ref[...] = idx_out
    w_ref[...] = w_out
    strict = (lax.broadcasted_iota(jnp.int32, (tm, tm), 1)
              < lax.broadcasted_iota(jnp.int32, (tm, tm), 0)).astype(F32)
    before = _dot(strict, chosen) + run_s[...]
    rank_out = jnp.zeros((tm, 128), jnp.int32)
    for kk in range(TOP_K):
        rk = jnp.sum(jnp.where(picks[kk], before, 0.0), axis=1, keepdims=True)
        rank_out = jnp.where(lane == kk, rk.astype(jnp.int32), rank_out)
    rank_ref[...] = rank_out
    total = run_s[...] + jnp.sum(chosen, axis=0, keepdims=True)
    run_s[...] = total
    cnt_ref[...] = total.astype(jnp.int32)


def route(logits, router_bias, tm=256):
    t = logits.shape[0]
    bias = jnp.concatenate([router_bias.astype(F32), jnp.zeros((128 - N_EXPERTS,), F32)])[None]
    row = pl.BlockSpec((tm, 128), lambda i: (i, 0))
    one = pl.BlockSpec((1, 128), lambda i: (0, 0))
    return pl.pallas_call(
        _route_kernel,
        grid=(t // tm,),
        in_specs=[row, one],
        out_specs=[row, row, row, one],
        out_shape=[jax.ShapeDtypeStruct((t, 128), jnp.int32), jax.ShapeDtypeStruct((t, 128), F32),
                   jax.ShapeDtypeStruct((t, 128), jnp.int32), jax.ShapeDtypeStruct((1, 128), jnp.int32)],
        scratch_shapes=[pltpu.VMEM((1, 128), F32)],
        compiler_params=_cparams(("arbitrary",)),
        name="route",
    )(logits, bias)


EXP_TM = 256
SCT_TM = 256


def _scatter_kernel(lt_ref, pos_ref, hp_ref, xs_hbm, zero_s, sem, zsem):
    i = pl.program_id(0)

    @pl.when(i == 0)
    def _():
        zero_s[...] = jnp.zeros_like(zero_s)
        for e in range(2 * N_EXPERTS):
            @pl.when(lt_ref[e] >= 0)
            def _():
                pltpu.make_async_copy(zero_s, xs_hbm.at[pl.ds(lt_ref[e] * EXP_TM, EXP_TM), :], zsem).start()
        for e in range(2 * N_EXPERTS):
            @pl.when(lt_ref[e] >= 0)
            def _():
                pltpu.make_async_copy(zero_s, xs_hbm.at[pl.ds(lt_ref[e] * EXP_TM, EXP_TM), :], zsem).wait()

    def body(r, carry):
        for kk in range(TOP_K):
            p = pos_ref[0, r * TOP_K + kk]
            pltpu.make_async_copy(hp_ref.at[pl.ds(r, 1), :], xs_hbm.at[pl.ds(p, 1), :], sem).start()
        return carry
    lax.fori_loop(0, SCT_TM, body, 0)
    for kk in range(TOP_K):
        pltpu.make_async_copy(hp_ref, xs_hbm.at[pl.ds(0, SCT_TM), :], sem).wait()


def scatter_rows(hp, pos, clear_tiles, n_rows):
    t, half = hp.shape
    n = t // SCT_TM
    grid_spec = pltpu.PrefetchScalarGridSpec(
        num_scalar_prefetch=1,
        grid=(n,),
        in_specs=[
            pl.BlockSpec((None, 1, SCT_TM * TOP_K), lambda i, lt: (i, 0, 0), memory_space=pltpu.SMEM),
            pl.BlockSpec((SCT_TM, half), lambda i, lt: (i, 0)),
        ],
        out_specs=pl.BlockSpec(memory_space=pl.ANY),
        scratch_shapes=[
            pltpu.VMEM((EXP_TM, half), jnp.uint32),
            pltpu.SemaphoreType.DMA(()),
            pltpu.SemaphoreType.DMA(()),
        ],
    )
    return pl.pallas_call(
        _scatter_kernel,
        grid_spec=grid_spec,
        out_shape=jax.ShapeDtypeStruct((n_rows, half), jnp.uint32),
        compiler_params=_cparams(("arbitrary",)),
        name="scatter_rows",
    )(clear_tiles, pos.reshape(n, 1, SCT_TM * TOP_K), hp)


def _expert_kernel(te_ref, nu_ref, x_ref, wg_ref, wu_ref, wd_ref, y_ref, wgu_s, wd_s):
    i = pl.program_id(0)
    n_used = nu_ref[0]
    half = x_ref.shape[1]
    new_expert = (i == 0) | (te_ref[i] != te_ref[jnp.maximum(i - 1, 0)])

    @pl.when(new_expert & (i < n_used))
    def _():
        wgu_s[:, 0:D_EXPERT] = wg_ref[...].astype(BF16)
        wgu_s[:, D_EXPERT:2 * D_EXPERT] = wu_ref[...].astype(BF16)
        wd_s[...] = wd_ref[...].astype(BF16)

    @pl.when(i < n_used)
    def _():
        x_hi, x_lo = _unpack_bf16_pair(x_ref[...])
        h = (jnp.dot(x_hi.astype(BF16), wgu_s[0:half, :], preferred_element_type=F32)
             + jnp.dot(x_lo.astype(BF16), wgu_s[half:2 * half, :], preferred_element_type=F32))
        act = _silu(h[:, 0:D_EXPERT]) * h[:, D_EXPERT:2 * D_EXPERT]
        y = jnp.dot(act.astype(BF16), wd_s[...], preferred_element_type=F32)
        y_ref[...] = _pack_bf16_pair(y[:, 0:half], y[:, half:2 * half])

    @pl.when(i >= n_used)
    def _():
        y_ref[...] = jnp.zeros_like(y_ref)


def routed_experts(x_sorted, tile_expert, n_used, w_gate, w_up, w_down, layer):
    n_rows, half = x_sorted.shape
    n_tiles = n_rows // EXP_TM
    d = 2 * half

    def used(i, nu):
        return jnp.minimum(i, nu[0] - 1)

    grid_spec = pltpu.PrefetchScalarGridSpec(
        num_scalar_prefetch=2,
        grid=(n_tiles,),
        in_specs=[
            pl.BlockSpec((EXP_TM, half), lambda i, te, nu: (used(i, nu), 0)),
            pl.BlockSpec((None, None, d, D_EXPERT), lambda i, te, nu: (layer, te[used(i, nu)], 0, 0)),
            pl.BlockSpec((None, None, d, D_EXPERT), lambda i, te, nu: (layer, te[used(i, nu)], 0, 0)),
            pl.BlockSpec((None, None, D_EXPERT, d), lambda i, te, nu: (layer, te[used(i, nu)], 0, 0)),
        ],
        out_specs=pl.BlockSpec((EXP_TM, half), lambda i, te, nu: (i, 0)),
        scratch_shapes=[
            pltpu.VMEM((d, 2 * D_EXPERT), BF16),
            pltpu.VMEM((D_EXPERT, d), BF16),
        ],
    )
    return pl.pallas_call(
        _expert_kernel,
        grid_spec=grid_spec,
        out_shape=jax.ShapeDtypeStruct((n_rows, half), jnp.uint32),
        compiler_params=_cparams(("arbitrary",)),
        name="routed_experts",
    )(tile_expert, n_used, x_sorted, w_gate, w_up, w_down)


CMB_TM = 128


def _start_combine_gather(pos_ref, y_hbm, dst, sem):
    def body(r, carry):
        for kk in range(TOP_K):
            p = pos_ref[0, r * TOP_K + kk]
            pltpu.make_async_copy(y_hbm.at[pl.ds(p, 1), :], dst.at[kk, pl.ds(r, 1), :], sem).start()
        return carry
    lax.fori_loop(0, CMB_TM, body, 0)


def _combine_kernel(pos_ref, posn_ref, y_hbm, w_ref, hp_ref, x_ref, gate_ref, sgu_ref, sd_ref,
                    o_ref, ybuf, sem):
    i = pl.program_id(0)
    n = pl.num_programs(0)
    slot = i % 2
    half = hp_ref.shape[1]

    @pl.when(i == 0)
    def _():
        _start_combine_gather(pos_ref, y_hbm, ybuf.at[0], sem.at[0])

    @pl.when(i + 1 < n)
    def _():
        _start_combine_gather(posn_ref, y_hbm, ybuf.at[1 - slot], sem.at[1 - slot])

    x_hi, x_lo = _unpack_bf16_pair(hp_ref[...])
    h = (jnp.dot(x_hi.astype(BF16), sgu_ref[0:half, :], preferred_element_type=F32)
         + jnp.dot(x_lo.astype(BF16), sgu_ref[half:2 * half, :], preferred_element_type=F32))
    act = _silu(h[:, 0:D_SHARED]) * h[:, D_SHARED:2 * D_SHARED]
    shared = jnp.dot(act.astype(BF16), sd_ref[...], preferred_element_type=F32)

    for kk in range(TOP_K):
        pltpu.make_async_copy(y_hbm.at[pl.ds(0, CMB_TM), :], ybuf.at[slot, kk], sem.at[slot]).wait()
    acc_hi = shared[:, 0:half]
    acc_lo = shared[:, half:2 * half]
    w = w_ref[...]
    for kk in range(TOP_K):
        y_hi, y_lo = _unpack_bf16_pair(ybuf[slot, kk])
        wk = w[:, kk:kk + 1]
        acc_hi = acc_hi + wk * y_hi
        acc_lo = acc_lo + wk * y_lo
    g = gate_ref[...]
    o_ref[:, 0:half] = x_ref[:, 0:half] + g[:, 0:half] * acc_hi
    o_ref[:, half:2 * half] = x_ref[:, half:2 * half] + g[:, half:2 * half] * acc_lo


def moe_combine(pos, y_sorted, w, hp, x, gate, seg_of_tile, sh_gate_up, sh_down):
    t, d = x.shape
    half = d // 2
    n = t // CMB_TM
    return pl.pallas_call(
        _combine_kernel,
        grid=(n,),
        in_specs=[
            pl.BlockSpec((None, 1, CMB_TM * TOP_K), lambda i: (i, 0, 0), memory_space=pltpu.SMEM),
            pl.BlockSpec((None, 1, CMB_TM * TOP_K), lambda i: (jnp.minimum(i + 1, n - 1), 0, 0),
                         memory_space=pltpu.SMEM),
            pl.BlockSpec(memory_space=pl.ANY),
            pl.BlockSpec((CMB_TM, 128), lambda i: (i, 0)),
            pl.BlockSpec((CMB_TM, half), lambda i: (i, 0)),
            pl.BlockSpec((CMB_TM, d), lambda i: (i, 0)),
            pl.BlockSpec((None, 1, d), lambda i: (seg_of_tile(i), 0, 0)),
            pl.BlockSpec((d, 2 * D_SHARED), lambda i: (0, 0)),
            pl.BlockSpec((D_SHARED, d), lambda i: (0, 0)),
        ],
        out_specs=pl.BlockSpec((CMB_TM, d), lambda i: (i, 0)),
        out_shape=jax.ShapeDtypeStruct((t, d), F32),
        scratch_shapes=[
            pltpu.VMEM((2, TOP_K, CMB_TM, half), jnp.uint32),
            pltpu.SemaphoreType.DMA((2,)),
        ],
        compiler_params=_cparams(("arbitrary",)),
        name="moe_combine",
    )(pos, pos, y_sorted, w, hp, x, gate, sh_gate_up, sh_down)


def moe_plan(eidx, rank, counts, n_tiles):
    tiles_per = (counts + EXP_TM - 1) // EXP_TM
    tile_end = jnp.cumsum(tiles_per)
    base = (tile_end - tiles_per) * EXP_TM
    experts = jnp.arange(N_EXPERTS, dtype=jnp.int32)
    pos = rank + jnp.sum(jnp.where(eidx[..., None] == experts, base, 0), axis=-1)
    tile_expert = jnp.sum((tile_end[None, :] <= jnp.arange(n_tiles, dtype=jnp.int32)[:, None])
                          .astype(jnp.int32), axis=1)
    tile_expert = jnp.minimum(tile_expert, N_EXPERTS - 1)
    last_tile = jnp.where(tiles_per > 0, tile_end - 1, -1)
    unused = tile_end[-1] + experts
    clear_tiles = jnp.concatenate([last_tile, jnp.where(unused < n_tiles, unused, -1)])
    return tile_expert, tile_end[-1:], clear_tiles, pos


def _rmsnorm_kernel(x_ref, g_ref, o_ref):
    x = x_ref[...]
    o_ref[...] = x * lax.rsqrt(jnp.mean(x * x, axis=-1, keepdims=True) + EPS) * g_ref[...]


def final_rmsnorm(x, g, tm=256):
    t, d = x.shape
    return pl.pallas_call(
        _rmsnorm_kernel,
        grid=(t // tm,),
        in_specs=[pl.BlockSpec((tm, d), lambda i: (i, 0)), pl.BlockSpec((1, d), lambda i: (0, 0))],
        out_specs=pl.BlockSpec((tm, d), lambda i: (i, 0)),
        out_shape=jax.ShapeDtypeStruct((t, d), F32),
        compiler_params=_cparams(("arbitrary",)),
        name="final_rmsnorm",
    )(x, g.reshape(1, d))


def lb_table(lb):
    lb = lb.astype(F32)
    return jnp.stack([jnp.log(lb), jnp.log1p(-lb), 1.0 - lb])


def mlstm_gate_bias(i_bias, f_bias):
    z = jnp.zeros((64,), F32)
    return jnp.concatenate([z, i_bias.astype(F32).reshape(-1), f_bias.astype(F32).reshape(-1),
                            jnp.zeros((32,), F32)])[None]


def ssd_consts(dt_bias, a_log):
    bias = jnp.concatenate([dt_bias.astype(F32).reshape(-1), jnp.zeros((64,), F32)])[None]
    a = -jnp.exp(a_log.astype(F32))
    expand = (jnp.arange(B_HEADS)[:, None] == jnp.arange(B_WIDTH)[None, :] // B_HEADDIM).astype(BF16)
    return [(bias, a[d][None], expand) for d in range(2)]


def _mod_tables(mod, nb):
    d = mod.shape[1] // 6
    parts = mod.reshape(mod.shape[0], 6, d)
    lat = parts[:nb]
    ctx = jnp.broadcast_to(parts[nb][None], (nb, 6, d))
    tab = jnp.stack([ctx, lat], axis=1).reshape(2 * nb, 6, d)
    return jnp.transpose(tab, (1, 0, 2))[:, :, None, :]


def _seg_fn(rows_per_batch, ctx_rows, tm):
    tiles_per_batch = rows_per_batch // tm
    ctx_tiles = ctx_rows // tm

    def seg(i):
        return 2 * (i // tiles_per_batch) + jnp.where(i % tiles_per_batch >= ctx_tiles, 1, 0)
    return seg


def _to_scan_order(a, rows):
    nb, _, d = a.shape
    return a.reshape(nb, rows, GRID_W, d).transpose(0, 2, 1, 3).reshape(nb, rows * GRID_W, d)


def _from_scan_order(a, rows):
    nb, _, d = a.shape
    return a.reshape(nb, GRID_W, rows, d).transpose(0, 2, 1, 3).reshape(nb, rows * GRID_W, d)


def _moe_block(x2, tab, seg_mod, seg_cmb, layer, norm_w, router_w, router_bias, w_gate, w_up, w_down,
               sh_gate, sh_up, sh_down):
    t, d = x2.shape
    rw = jnp.concatenate([router_w.astype(F32), jnp.zeros((d, 128 - N_EXPERTS), F32)], axis=1)
    hp, logits = modulate(x2, norm_w, tab[3], tab[4], seg_mod, 256, router_w=rw)
    eidx, ew, rank, counts = route(logits, router_bias)
    n_tiles = t * TOP_K // EXP_TM + N_EXPERTS
    tile_expert, n_used, clear_tiles, pos = moe_plan(eidx[:, :TOP_K], rank[:, :TOP_K],
                                                     counts[0, :N_EXPERTS], n_tiles)
    x_sorted = scatter_rows(hp, pos, clear_tiles, n_tiles * EXP_TM)
    y_sorted = routed_experts(x_sorted, tile_expert, n_used, w_gate, w_up, w_down, layer)
    sgu = jnp.concatenate([sh_gate, sh_up], axis=1).astype(BF16)
    return moe_combine(pos.reshape(t // CMB_TM, 1, CMB_TM * TOP_K), y_sorted, ew, hp, x2, tab[5],
                       seg_cmb, sgu, sh_down.astype(BF16))


def kernel(x, c, ctx, c_ctx, ada_w, ada_b, norm_mix, norm_ffn, norm_final, w_in, w_out, hgrn_lb,
           hgrn_norm, ssm_conv_w, ssm_conv_b, ssm_dt_bias, ssm_a_log, ssm_d, ssm_norm, mlstm_i_bias,
           mlstm_f_bias, mlstm_norm, router_w, router_bias, moe_w_gate, moe_w_up, moe_w_down,
           shared_w_gate, shared_w_up, shared_w_down):
    nb, seq, d = x.shape
    n_ctx = ctx.shape[1]
    length = n_ctx + seq
    rows = seq // GRID_W
    depth = ada_w.shape[0]
    t_all = nb * length
    t_lat = nb * seq

    lb_all = jnp.cumsum(jax.nn.softmax(hgrn_lb.astype(F32), axis=0), axis=0)
    lb_all = lb_all - lb_all[0]
    cond = jnp.concatenate([c, c_ctx[None], jnp.zeros((8 - nb - 1, d), F32)], axis=0)
    mod = ada_modulation(cond, ada_w, ada_b)

    xs = jnp.concatenate([ctx, x], axis=1)
    for l in range(depth):
        last = l == depth - 1
        tab = _mod_tables(mod[l], nb)
        h = modulate(xs.reshape(t_all, d), norm_mix[l], tab[0], tab[1], _seg_fn(length, n_ctx, 256), 256)
        if l % 2 == 1:
            h3 = h.reshape(nb, length, d)
            h = jnp.concatenate([h3[:, :n_ctx], _to_scan_order(h3[:, n_ctx:], rows)], axis=1)
            h = h.reshape(t_all, d)
        w_c = w_in[l:l + 1, :, U_C0:U_G0]
        w_small = jnp.concatenate([w_in[l:l + 1, :, U_DT0:U_C0], w_in[l:l + 1, :, U_G0:],
                                   jnp.zeros((1, d, U_SMALL - 96), F32)], axis=2)
        u_ab = matmul(h, w_in, l, tm=1024, tn=512, n_tiles=U_AB // 512, name="in_proj_ab")
        u_c = matmul(h, w_c, 0, tm=1024, tn=512, name="in_proj_c")
        u_s = matmul(h, w_small, 0, tm=1024, tn=U_SMALL, name="in_proj_small")
        u_ab3 = u_ab.reshape(nb, length, U_AB)
        u_c3 = u_c.reshape(nb, length, 4 * C_WIDTH)
        u_s3 = u_s.reshape(nb, length, U_SMALL)

        xc = ssd_conv(u_ab3, ssm_conv_w[l], ssm_conv_b[l], n_ctx)
        consts = ssd_consts(ssm_dt_bias[l], ssm_a_log[l])
        gate_bias = mlstm_gate_bias(mlstm_i_bias[l], mlstm_f_bias[l])
        ya = [hgrn2_scan(u_ab3, lb_table(lb_all[l][dd]), n_ctx, rev=bool(dd)) for dd in range(2)]
        yb = [ssd_scan(xc, u_s3, *consts[dd], n_ctx, rev=bool(dd)) for dd in range(2)]
        yc = [mlstm_scan(u_c3, u_s3, gate_bias, n_ctx, rev=bool(dd)) for dd in range(2)]
        flat = lambda a: a.reshape(t_all, a.shape[-1])
        ymix = mix_out(flat(ya[0]), flat(ya[1]), flat(yb[0]), flat(yb[1]), flat(yc[0]), flat(yc[1]),
                       u_ab, flat(xc), u_c, hgrn_norm[l][None], ssm_norm[l][None], mlstm_norm[l][None],
                       jnp.repeat(ssm_d[l].astype(F32), B_HEADDIM)[None])
        if l % 2 == 1:
            y3 = ymix.reshape(nb, length, d)
            ymix = jnp.concatenate([y3[:, :n_ctx], _from_scan_order(y3[:, n_ctx:], rows)], axis=1)
            ymix = ymix.reshape(t_all, d)

        moe_w = (l, norm_ffn[l], router_w[l], router_bias[l], moe_w_gate, moe_w_up, moe_w_down,
                 shared_w_gate[l], shared_w_up[l], shared_w_down[l])
        if not last:
            x2 = matmul_residual(ymix, w_out, l, xs.reshape(t_all, d), tab[2],
                                 _seg_fn(length, n_ctx, 256), tm=256, tn=512)
            x2 = _moe_block(x2, tab, _seg_fn(length, n_ctx, 256), _seg_fn(length, n_ctx, CMB_TM), *moe_w)
            xs = x2.reshape(nb, length, d)
        else:
            y_lat = ymix.reshape(nb, length, d)[:, n_ctx:].reshape(t_lat, d)
            x2 = matmul_residual(y_lat, w_out, l, xs[:, n_ctx:].reshape(t_lat, d), tab[2],
                                 _seg_fn(seq, 0, 256), tm=256, tn=512)
            x2 = _moe_block(x2, tab, _seg_fn(seq, 0, 256), _seg_fn(seq, 0, CMB_TM), *moe_w)
            return final_rmsnorm(x2, norm_final).reshape(nb, seq, d)
```

```python
import functools

import jax
import jax.numpy as jnp
from jax import lax
from jax.experimental import pallas as pl
from jax.experimental.pallas import tpu as pltpu

F32 = jnp.float32
BF16 = jnp.bfloat16
HIGHEST = lax.Precision.HIGHEST
NEG_INF = float("-inf")

D_MODEL = 4096
GRID_W = 64
CHUNK = 64
EPS = 1e-6
A_WIDTH = 1024
A_DK = 128
A_HEADS = 8
B_WIDTH = 2048
B_HEADDIM = 64
B_HEADS = 32
B_GROUPS = 8
B_RPG = 4
B_STATE = 128
B_CONV = 5
B_CONV_CH = 4096
C_WIDTH = 1024
C_DK = 128
C_HEADS = 8
N_EXPERTS = 64
N_EXPERT_GROUPS = 8
TOPK_GROUPS = 4
TOP_K = 8
D_EXPERT = 256
D_SHARED = 256
ROUTED_SCALE = 2.5

U_AB = 11264
U_DT0 = 11264
U_C0 = 11328
U_G0 = 15424
U_SMALL = 128

SUB = 16
PAD = 16
VMEM_LIMIT = 56 * 1024 * 1024


def _cparams(sem):
    return pltpu.CompilerParams(dimension_semantics=sem, vmem_limit_bytes=VMEM_LIMIT)


def _dot(a, b):
    return jnp.dot(a.astype(BF16), b.astype(BF16), preferred_element_type=F32)


def _bdot(a, b, ca, cb):
    return lax.dot_general(a.astype(BF16), b.astype(BF16), (((ca,), (cb,)), ((0,), (0,))),
                           preferred_element_type=F32)


def _dot_hi(a, b):
    return jnp.dot(a, b, precision=HIGHEST, preferred_element_type=F32)


def _dot_nt_hi(a, b):
    return lax.dot_general(a, b, (((1,), (1,)), ((), ())), precision=HIGHEST,
                           preferred_element_type=F32)


def _log_sigmoid(z):
    return jnp.minimum(z, 0.0) - jnp.log1p(jnp.exp(-jnp.abs(z)))


def _softplus(z):
    return jnp.maximum(z, 0.0) + jnp.log1p(jnp.exp(-jnp.abs(z)))


def _silu(z):
    return z * jax.nn.sigmoid(z)


def _tri(rev):
    ri = lax.broadcasted_iota(jnp.int32, (CHUNK, CHUNK), 0)
    ci = lax.broadcasted_iota(jnp.int32, (CHUNK, CHUNK), 1)
    return (ci >= ri) if rev else (ci <= ri)


def _chunk_block(c, n_ctx_chunks, n_lat_chunks, rev):
    if not rev:
        return c
    return jnp.where(c < n_ctx_chunks, n_ctx_chunks - 1 - c, 2 * n_ctx_chunks + n_lat_chunks - 1 - c)


def _ada_kernel(c_ref, w_ref, b_ref, o_ref):
    s = _silu(c_ref[...])
    o_ref[...] = _dot(s, w_ref[...]) + b_ref[...]


def ada_modulation(cond, ada_w, ada_b):
    depth, d, n = ada_w.shape
    tn = 512
    return pl.pallas_call(
        _ada_kernel,
        grid=(depth, n // tn),
        in_specs=[
            pl.BlockSpec((8, d), lambda l, j: (0, 0)),
            pl.BlockSpec((None, d, tn), lambda l, j: (l, 0, j)),
            pl.BlockSpec((None, 1, tn), lambda l, j: (l, 0, j)),
        ],
        out_specs=pl.BlockSpec((None, 8, tn), lambda l, j: (l, 0, j)),
        out_shape=jax.ShapeDtypeStruct((depth, 8, n), F32),
        compiler_params=_cparams(("arbitrary", "arbitrary")),
        name="ada_modulation",
    )(cond, ada_w, ada_b.reshape(depth, 1, n))


def _pack_bf16_pair(a, b):
    hi = lax.bitcast_convert_type(a.astype(BF16).astype(F32), jnp.uint32)
    lo = lax.bitcast_convert_type(b.astype(BF16).astype(F32), jnp.uint32)
    return hi | (lo >> 16)


def _unpack_bf16_pair(w):
    hi = lax.bitcast_convert_type(w & jnp.uint32(0xFFFF0000), F32)
    lo = lax.bitcast_convert_type(w << 16, F32)
    return hi, lo


def _modulate_kernel(x_ref, g_ref, sh_ref, sc_ref, *rest, with_router):
    x = x_ref[...]
    y = x * lax.rsqrt(jnp.mean(x * x, axis=-1, keepdims=True) + EPS) * g_ref[...]
    h = y * (1.0 + sc_ref[...]) + sh_ref[...]
    if with_router:
        rw_ref, h_ref, lg_ref = rest
        h_hi = h.astype(BF16)
        h_lo = (h - h_hi.astype(F32)).astype(BF16)
        both = jnp.dot(h_hi, rw_ref[...], preferred_element_type=F32)
        lg_ref[...] = (both[:, 0:128] + both[:, 128:256]
                       + jnp.dot(h_lo, rw_ref[:, 0:128], preferred_element_type=F32))
        half = h.shape[1] // 2
        h_ref[...] = _pack_bf16_pair(h[:, :half], h[:, half:])
    else:
        (h_ref,) = rest
        h_ref[...] = h.astype(h_ref.dtype)


def modulate(x, g, shift, scale, seg_of_tile, tm, router_w=None):
    t, d = x.shape
    with_router = router_w is not None
    vec = pl.BlockSpec((None, 1, d), lambda i: (seg_of_tile(i), 0, 0))
    in_specs = [pl.BlockSpec((tm, d), lambda i: (i, 0)),
                pl.BlockSpec((1, d), lambda i: (0, 0)), vec, vec]
    args = [x, g.reshape(1, d), shift, scale]
    if with_router:
        in_specs.append(pl.BlockSpec((d, 256), lambda i: (0, 0)))
        args.append(router_w)
        out_specs = [pl.BlockSpec((tm, d // 2), lambda i: (i, 0)),
                     pl.BlockSpec((tm, 128), lambda i: (i, 0))]
        out_shape = [jax.ShapeDtypeStruct((t, d // 2), jnp.uint32),
                     jax.ShapeDtypeStruct((t, 128), F32)]
    else:
        out_specs = pl.BlockSpec((tm, d), lambda i: (i, 0))
        out_shape = jax.ShapeDtypeStruct((t, d), BF16)
    return pl.pallas_call(
        functools.partial(_modulate_kernel, with_router=with_router),
        grid=(t // tm,),
        in_specs=in_specs, out_specs=out_specs, out_shape=out_shape,
        compiler_params=_cparams(("arbitrary",)),
        name="modulate_router" if with_router else "modulate",
    )(*args)


def _mm_kernel(a_ref, b_ref, o_ref):
    o_ref[...] = jnp.dot(a_ref[...], b_ref[...], preferred_element_type=F32)


def matmul(a, b, *, tm, tn, name="matmul"):
    m, k = a.shape
    n = b.shape[1]
    return pl.pallas_call(
        _mm_kernel,
        grid=(m // tm, n // tn),
        in_specs=[pl.BlockSpec((tm, k), lambda i, j: (i, 0)),
                  pl.BlockSpec((k, tn), lambda i, j: (0, j))],
        out_specs=pl.BlockSpec((tm, tn), lambda i, j: (i, j)),
        out_shape=jax.ShapeDtypeStruct((m, n), F32),
        compiler_params=_cparams(("arbitrary", "arbitrary")),
        name=name,
    )(a, b)


def _mm_res_kernel(a_ref, b_ref, r_ref, g_ref, o_ref, b_s):
    @pl.when((pl.program_id(1) == 0) & (pl.program_id(2) == 0))
    def _():
        b_s[...] = b_ref[...].astype(BF16)

    acc = jnp.dot(a_ref[...], b_s[...], preferred_element_type=F32)
    o_ref[...] = r_ref[...] + g_ref[...] * acc


def matmul_residual(a, b, layer, res, gate, *, first_row, n_rows, ctx_rows, tm, tn):
    nb, _, k = a.shape
    n = b.shape[2]
    t0 = first_row // tm
    ctx_tiles = ctx_rows // tm
    seg = lambda bb, it: 2 * bb + jnp.where(t0 + it >= ctx_tiles, 1, 0)
    return pl.pallas_call(
        _mm_res_kernel,
        grid=(n // tn, nb, n_rows // tm),
        in_specs=[pl.BlockSpec((None, tm, k), lambda j, bb, it: (bb, t0 + it, 0)),
                  pl.BlockSpec((None, k, tn), lambda j, bb, it: (layer, 0, j)),
                  pl.BlockSpec((None, tm, tn), lambda j, bb, it: (bb, t0 + it, j)),
                  pl.BlockSpec((None, 1, tn), lambda j, bb, it: (seg(bb, it), 0, j))],
        out_specs=pl.BlockSpec((None, tm, tn), lambda j, bb, it: (bb, it, j)),
        out_shape=jax.ShapeDtypeStruct((nb, n_rows, n), F32),
        scratch_shapes=[pltpu.VMEM((k, tn), BF16)],
        compiler_params=_cparams(("arbitrary", "arbitrary", "arbitrary")),
        name="matmul_residual",
    )(a, b, res, gate)


def _hgrn2_kernel(q_ref, f_ref, v_ref, lb_ref, y_ref, k_s, v_s, gc_s, st_s, *, rev):
    c = pl.program_id(1)

    @pl.when(c == 0)
    def _():
        st_s[...] = jnp.zeros_like(st_s)
        k_s[...] = jnp.zeros_like(k_s)
        v_s[...] = jnp.zeros_like(v_s)
        gc_s[...] = jnp.zeros_like(gc_s)

    nh = A_HEADS
    q = q_ref[0]
    z = f_ref[0]
    log_lb = lb_ref[0:1, :]
    log1m_lb = lb_ref[1:2, :]
    one_m_lb = lb_ref[2:3, :]
    b2 = log1m_lb + _log_sigmoid(z)
    logf = jnp.maximum(log_lb, b2) + jnp.log1p(jnp.exp(-jnp.abs(log_lb - b2)))
    gc2 = _dot_hi(_tri(rev).astype(F32), logf)
    heads = lambda a: jnp.stack([a[:, h * A_DK:(h + 1) * A_DK] for h in range(nh)])
    qh = heads(q * jax.nn.sigmoid(q) * (A_DK ** -0.5))
    kh = heads(one_m_lb * jax.nn.sigmoid(-z))
    vh = heads(v_ref[0])
    gcm = heads(gc2)
    gxm = gcm - heads(logf)
    k_s[:, PAD:PAD + CHUNK, :] = kh
    v_s[:, PAD:PAD + CHUNK, :] = vh
    gc_s[:, PAD:PAD + CHUNK, :] = gcm

    rows = lax.broadcasted_iota(jnp.int32, (1, CHUNK, 1), 1)
    rowmod = rows % SUB
    n_sub = CHUNK // SUB
    st = st_s[...]
    y = _bdot(qh * jnp.exp(gcm), st, 2, 2)
    for dlt in range(SUB):
        off = PAD + dlt if rev else PAD - dlt
        ksh = k_s[:, off:off + CHUNK, :]
        vsh = v_s[:, off:off + CHUNK, :]
        gsh = gc_s[:, off:off + CHUNK, :]
        valid = (rowmod + dlt < SUB) if rev else (rowmod >= dlt)
        e = jnp.exp(jnp.where(valid, gcm - gsh, NEG_INF))
        r = jnp.sum(qh * ksh * e, axis=2, keepdims=True)
        y = y + r * vsh
    pieces = []
    for blk in range(n_sub):
        r0, r1 = blk * SUB, (blk + 1) * SUB
        first = (blk == n_sub - 1) if rev else (blk == 0)
        if first:
            pieces.append(jnp.zeros((nh, SUB, A_DK), F32))
            continue
        ref = gxm[:, r1 - 1:r1, :] if rev else gxm[:, r0:r0 + 1, :]
        earlier = (rows >= r1) if rev else (rows < r0)
        qhat = qh[:, r0:r1, :] * jnp.exp(gcm[:, r0:r1, :] - ref)
        khat = kh * jnp.exp(jnp.where(earlier, ref - gcm, NEG_INF))
        sc = _bdot(qhat, khat, 2, 2)
        pieces.append(_bdot(sc, vh, 2, 1))
    y = y + jnp.concatenate(pieces, axis=1)
    gtot = gcm[:, 0:1, :] if rev else gcm[:, CHUNK - 1:CHUNK, :]
    st_s[...] = st * jnp.exp(gtot) + _bdot(vh, kh * jnp.exp(gtot - gcm), 1, 1)
    for h in range(nh):
        y_ref[0, :, h * A_DK:(h + 1) * A_DK] = y[h]


def hgrn2_scan(u_ab, lb_tab, n_ctx, rev):
    nb, length, _ = u_ab.shape
    n_chunks = length // CHUNK
    n_ctx_chunks = n_ctx // CHUNK
    blk = functools.partial(_chunk_block, n_ctx_chunks=n_ctx_chunks,
                            n_lat_chunks=n_chunks - n_ctx_chunks, rev=rev)
    fcol = 2 if rev else 1

    def col(j):
        return pl.BlockSpec((1, CHUNK, A_WIDTH), lambda b, c: (b, blk(c), j))

    return pl.pallas_call(
        functools.partial(_hgrn2_kernel, rev=rev),
        grid=(nb, n_chunks),
        in_specs=[col(0), col(fcol), col(3), pl.BlockSpec((3, A_WIDTH), lambda b, c: (0, 0))],
        out_specs=pl.BlockSpec((1, CHUNK, A_WIDTH), lambda b, c: (b, blk(c), 0)),
        out_shape=jax.ShapeDtypeStruct((nb, length, A_WIDTH), F32),
        scratch_shapes=[
            pltpu.VMEM((A_HEADS, CHUNK + 2 * PAD, A_DK), F32),
            pltpu.VMEM((A_HEADS, CHUNK + 2 * PAD, A_DK), F32),
            pltpu.VMEM((A_HEADS, CHUNK + 2 * PAD, A_DK), F32),
            pltpu.VMEM((A_HEADS, A_DK, A_DK), F32),
        ],
        compiler_params=_cparams(("arbitrary", "arbitrary")),
        name="hgrn2_bwd" if rev else "hgrn2_fwd",
    )(u_ab, u_ab, u_ab, lb_tab)


def _mlstm_kernel(qf_ref, kf_ref, vf_ref, gf_ref, qb_ref, kb_ref, vb_ref, gb_ref, bias_ref,
                  hf_ref, hb_ref, c_s, n_s, m_s):
    @pl.when(pl.program_id(1) == 0)
    def _():
        c_s[...] = jnp.zeros_like(c_s)
        n_s[...] = jnp.zeros_like(n_s)
        m_s[...] = jnp.zeros_like(m_s)

    nh = C_HEADS
    eye = (lax.broadcasted_iota(jnp.int32, (nh, nh), 0)
           == lax.broadcasted_iota(jnp.int32, (nh, nh), 1)).astype(F32)
    heads = lambda ref: [ref[0, :, h * C_DK:(h + 1) * C_DK] for h in range(nh)]
    bcols, igcols, arows, masks = [], [], [], []
    for d, g_ref in enumerate((gf_ref, gb_ref)):
        gates = g_ref[0] + bias_ref[...]
        ig_all = gates[:, 64 + 8 * d:72 + 8 * d]
        lf_all = _log_sigmoid(gates[:, 80 + 8 * d:88 + 8 * d])
        mask = _tri(bool(d))
        b_all = _dot_hi(mask.astype(F32), lf_all)
        arows.append(_dot_nt_hi(eye, ig_all - b_all))
        bcols += [b_all[:, h:h + 1] for h in range(nh)]
        igcols += [ig_all[:, h:h + 1] for h in range(nh)]
        masks.append(jnp.broadcast_to(mask[None], (nh, CHUNK, CHUNK)))
    q = jnp.stack(heads(qf_ref) + heads(qb_ref)) * (C_DK ** -0.5)
    k = jnp.stack(heads(kf_ref) + heads(kb_ref))
    v = jnp.stack(heads(vf_ref) + heads(vb_ref))
    bcol = jnp.stack(bcols)
    igcol = jnp.stack(igcols)
    arow = jnp.concatenate(arows, axis=0)[:, None, :]
    mask = jnp.concatenate(masks, axis=0)
    dmat = jnp.where(mask, bcol + arow, NEG_INF)
    m0 = m_s[...]
    m_inter = bcol + m0
    m_i = jnp.maximum(m_inter, jnp.max(dmat, axis=2, keepdims=True))
    w_intra = jnp.exp(dmat - m_i)
    w_inter = jnp.exp(m_inter - m_i)
    s = _bdot(q, k, 2, 2) * w_intra
    cmat = c_s[...]
    nvec = n_s[...]
    num = _bdot(s, v, 2, 1) + w_inter * _bdot(q, cmat, 2, 1)
    den = jnp.sum(s, axis=2, keepdims=True) + w_inter * jnp.sum(q * nvec, axis=2, keepdims=True)
    out = num / jnp.maximum(jnp.abs(den), jnp.exp(-m_i))
    for h in range(nh):
        hf_ref[0, :, h * C_DK:(h + 1) * C_DK] = out[h]
        hb_ref[0, :, h * C_DK:(h + 1) * C_DK] = out[nh + h]
    b_last = jnp.concatenate([bcol[:nh, CHUNK - 1:CHUNK, :], bcol[nh:, 0:1, :]], axis=0)
    log_wj = b_last - bcol + igcol
    m_new = jnp.maximum(b_last + m0, jnp.max(log_wj, axis=1, keepdims=True))
    w0 = jnp.exp(b_last + m0 - m_new)
    wk = jnp.exp(log_wj - m_new) * k
    c_s[...] = w0 * cmat + _bdot(wk, v, 1, 1)
    n_s[...] = w0 * nvec + jnp.sum(wk, axis=1, keepdims=True)
    m_s[...] = m_new


def mlstm_scan(u_c, u_s, gate_bias, n_ctx):
    nb, length, _ = u_c.shape
    n_chunks = length // CHUNK
    n_ctx_chunks = n_ctx // CHUNK
    blks = [functools.partial(_chunk_block, n_ctx_chunks=n_ctx_chunks,
                              n_lat_chunks=n_chunks - n_ctx_chunks, rev=rev) for rev in (False, True)]
    in_specs = []
    for blk in blks:
        in_specs += [pl.BlockSpec((1, CHUNK, C_WIDTH), lambda b, c, j=j, blk=blk: (b, blk(c), j))
                     for j in range(3)]
        in_specs.append(pl.BlockSpec((1, CHUNK, U_SMALL), lambda b, c, blk=blk: (b, blk(c), 0)))
    in_specs.append(pl.BlockSpec((1, U_SMALL), lambda b, c: (0, 0)))
    return pl.pallas_call(
        _mlstm_kernel,
        grid=(nb, n_chunks),
        in_specs=in_specs,
        out_specs=[pl.BlockSpec((1, CHUNK, C_WIDTH), lambda b, c, blk=blk: (b, blk(c), 0)) for blk in blks],
        out_shape=[jax.ShapeDtypeStruct((nb, length, C_WIDTH), F32)] * 2,
        scratch_shapes=[
            pltpu.VMEM((2 * C_HEADS, C_DK, C_DK), F32),
            pltpu.VMEM((2 * C_HEADS, 1, C_DK), F32),
            pltpu.VMEM((2 * C_HEADS, 1, 1), F32),
        ],
        compiler_params=_cparams(("arbitrary", "arbitrary")),
        name="mlstm_scan",
    )(u_c, u_c, u_c, u_s, u_c, u_c, u_c, u_s, gate_bias)


CONV_TM = 256
CONV_TN = 1024
HALO = 8


def _conv_kernel(prev_ref, cur_ref, next_ref, w_ref, b_ref, o_ref, win_s, *, tiles_ctx, tiles_all):
    i = pl.program_id(1)
    seg_start = (i == 0) | (i == tiles_ctx)
    seg_end = (i == tiles_ctx - 1) | (i == tiles_all - 1)
    win_s[0:HALO, :] = jnp.where(seg_start, 0.0, prev_ref[0])
    win_s[HALO:HALO + CONV_TM, :] = cur_ref[0]
    win_s[HALO + CONV_TM:2 * HALO + CONV_TM, :] = jnp.where(seg_end, 0.0, next_ref[0])
    acc = b_ref[...] + jnp.zeros((CONV_TM, CONV_TN), F32)
    for t in range(B_CONV):
        o = HALO + t - B_CONV // 2
        acc = acc + w_ref[t:t + 1, :] * win_s[o:o + CONV_TM, :]
    o_ref[0] = _silu(acc)


def ssd_conv(u_ab, conv_w, conv_b, n_ctx):
    nb, length, _ = u_ab.shape
    tiles_all = length // CONV_TM
    tiles_ctx = n_ctx // CONV_TM
    col0 = (U_AB - B_CONV_CH) // CONV_TN
    per = CONV_TM // HALO
    n_halo = length // HALO
    return pl.pallas_call(
        functools.partial(_conv_kernel, tiles_ctx=tiles_ctx, tiles_all=tiles_all),
        grid=(nb, tiles_all, B_CONV_CH // CONV_TN),
        in_specs=[
            pl.BlockSpec((1, HALO, CONV_TN), lambda b, i, j: (b, jnp.maximum(i * per - 1, 0), col0 + j)),
            pl.BlockSpec((1, CONV_TM, CONV_TN), lambda b, i, j: (b, i, col0 + j)),
            pl.BlockSpec((1, HALO, CONV_TN),
                         lambda b, i, j: (b, jnp.minimum((i + 1) * per, n_halo - 1), col0 + j)),
            pl.BlockSpec((B_CONV, CONV_TN), lambda b, i, j: (0, j)),
            pl.BlockSpec((1, CONV_TN), lambda b, i, j: (0, j)),
        ],
        out_specs=pl.BlockSpec((1, CONV_TM, CONV_TN), lambda b, i, j: (b, i, j)),
        out_shape=jax.ShapeDtypeStruct((nb, length, B_CONV_CH), F32),
        scratch_shapes=[pltpu.VMEM((CONV_TM + 2 * HALO, CONV_TN), F32)],
        compiler_params=_cparams(("arbitrary", "arbitrary", "arbitrary")),
        name="ssd_conv",
    )(u_ab, u_ab, u_ab, conv_w, conv_b.reshape(1, B_CONV_CH))


def _ssd_kernel(xc_ref, us_ref, bias_ref, an_ref, e_ref, y_ref, s_s, *, rev):
    c = pl.program_id(1)

    @pl.when(c == 0)
    def _():
        s_s[...] = jnp.zeros_like(s_s)

    d = 1 if rev else 0
    ng = B_GROUPS
    gw = B_RPG * B_HEADDIM
    mask = _tri(rev)
    last = 0 if rev else CHUNK - 1
    dt_n = _softplus(us_ref[0] + bias_ref[...])[:, 32 * d:32 * d + B_HEADS]
    cum_n = _dot_hi(mask.astype(F32), dt_n * an_ref[...])
    eye = (lax.broadcasted_iota(jnp.int32, (B_HEADS, B_HEADS), 0)
           == lax.broadcasted_iota(jnp.int32, (B_HEADS, B_HEADS), 1)).astype(F32)
    cum_t = _dot_nt_hi(eye, cum_n)
    decay_in = jnp.exp(cum_n)
    decay_out = jnp.exp(cum_n[last:last + 1, :] - cum_n)
    spread = _dot(jnp.concatenate([dt_n, decay_in, decay_out], axis=0), e_ref[...])
    groups = lambda a: jnp.stack([a[:, g * gw:(g + 1) * gw] for g in range(ng)])
    dt_x = groups(spread[0:CHUNK])
    in_x = groups(spread[CHUNK:2 * CHUNK])
    out_x = groups(spread[2 * CHUNK:3 * CHUNK])
    xdt = jnp.stack([xc_ref[0, :, g * gw:(g + 1) * gw] for g in range(ng)]) * dt_x
    bm = jnp.stack([xc_ref[0, :, B_WIDTH + g * B_STATE:B_WIDTH + (g + 1) * B_STATE]
                    for g in range(ng)])
    c0 = B_WIDTH + B_GROUPS * B_STATE
    cm = jnp.stack([xc_ref[0, :, c0 + g * B_STATE:c0 + (g + 1) * B_STATE] for g in range(ng)])
    state = s_s[...]
    cb = _bdot(cm, bm, 2, 2)
    y = in_x * _bdot(cm, state, 2, 1)
    head_of_lane = lax.broadcasted_iota(jnp.int32, (1, 1, gw), 2) // B_HEADDIM
    for r in range(B_RPG):
        ccol = jnp.stack([cum_n[:, g * B_RPG + r:g * B_RPG + r + 1] for g in range(ng)])
        crow = jnp.stack([cum_t[g * B_RPG + r:g * B_RPG + r + 1, :] for g in range(ng)])
        lmat = cb * jnp.exp(jnp.where(mask[None], ccol - crow, NEG_INF))
        y = y + _bdot(lmat, jnp.where(head_of_lane == r, xdt, 0.0), 2, 1)
    s_s[...] = in_x[:, last:last + 1, :] * state + _bdot(bm, xdt * out_x, 1, 1)
    for g in range(ng):
        y_ref[0, :, g * gw:(g + 1) * gw] = y[g]


def ssd_scan(xc, u_s, dt_bias, a_n, expand, n_ctx, rev):
    nb, length, _ = xc.shape
    n_chunks = length // CHUNK
    n_ctx_chunks = n_ctx // CHUNK
    blk = functools.partial(_chunk_block, n_ctx_chunks=n_ctx_chunks,
                            n_lat_chunks=n_chunks - n_ctx_chunks, rev=rev)
    const = lambda shape: pl.BlockSpec(shape, lambda b, c: (0, 0))
    return pl.pallas_call(
        functools.partial(_ssd_kernel, rev=rev),
        grid=(nb, n_chunks),
        in_specs=[pl.BlockSpec((1, CHUNK, B_CONV_CH), lambda b, c: (b, blk(c), 0)),
                  pl.BlockSpec((1, CHUNK, U_SMALL), lambda b, c: (b, blk(c), 0)),
                  const((1, U_SMALL)), const((1, B_HEADS)), const((B_HEADS, B_WIDTH))],
        out_specs=pl.BlockSpec((1, CHUNK, B_WIDTH), lambda b, c: (b, blk(c), 0)),
        out_shape=jax.ShapeDtypeStruct((nb, length, B_WIDTH), F32),
        scratch_shapes=[pltpu.VMEM((B_GROUPS, B_STATE, B_RPG * B_HEADDIM), F32)],
        compiler_params=_cparams(("arbitrary", "arbitrary")),
        name="ssd_bwd" if rev else "ssd_fwd",
    )(xc, u_s, dt_bias, a_n, expand)


def _group_rmsnorm(y, width):
    out = []
    for g in range(y.shape[1] // width):
        yg = y[:, g * width:(g + 1) * width]
        out.append(yg * lax.rsqrt(jnp.mean(yg * yg, axis=1, keepdims=True) + EPS))
    return jnp.concatenate(out, axis=1)


def _mix_out_kernel(af_ref, ab_ref, ag_ref, bf_ref, bb_ref, bx_ref, bz0_ref, bz1_ref,
                    cf_ref, cb_ref, co_ref, wa_ref, wb_ref, wc_ref, dsk_ref, o_ref):
    ya = _group_rmsnorm(af_ref[...] + ab_ref[...], A_DK) * wa_ref[...] * _silu(ag_ref[...])
    o_ref[:, 0:A_WIDTH] = ya.astype(o_ref.dtype)
    z = jnp.concatenate([bz0_ref[...], bz1_ref[...]], axis=1)
    yb = (bf_ref[...] + bb_ref[...] + dsk_ref[...] * bx_ref[...]) * _silu(z)
    yb = _group_rmsnorm(yb, B_WIDTH // B_GROUPS) * wb_ref[...]
    o_ref[:, A_WIDTH:A_WIDTH + B_WIDTH] = yb.astype(o_ref.dtype)
    yc = (cf_ref[...] + cb_ref[...]) * jax.nn.sigmoid(co_ref[...])
    yc = _group_rmsnorm(yc, C_DK) * wc_ref[...]
    o_ref[:, A_WIDTH + B_WIDTH:] = yc.astype(o_ref.dtype)


def mix_out(ya_f, ya_b, yb_f, yb_b, yc_f, yc_b, u_ab, xc, u_c, wa, wb, wc, dskip_x, tm=128):
    t = ya_f.shape[0]
    row = lambda w, j=0: pl.BlockSpec((tm, w), lambda i: (i, j))
    vec = lambda w: pl.BlockSpec((1, w), lambda i: (0, 0))
    return pl.pallas_call(
        _mix_out_kernel,
        grid=(t // tm,),
        in_specs=[row(A_WIDTH), row(A_WIDTH), row(A_WIDTH, 4),
                  row(B_WIDTH), row(B_WIDTH), row(B_WIDTH), row(1024, 5), row(1024, 6),
                  row(C_WIDTH), row(C_WIDTH), row(C_WIDTH, 3),
                  vec(A_WIDTH), vec(B_WIDTH), vec(C_WIDTH), vec(B_WIDTH)],
        out_specs=pl.BlockSpec((tm, D_MODEL), lambda i: (i, 0)),
        out_shape=jax.ShapeDtypeStruct((t, D_MODEL), BF16),
        compiler_params=_cparams(("arbitrary",)),
        name="mix_out",
    )(ya_f, ya_b, u_ab, yb_f, yb_b, xc, u_ab, u_ab, yc_f, yc_b, u_c, wa, wb, wc, dskip_x)


def _first_argmax(vals, lane):
    m = jnp.max(vals, axis=1, keepdims=True)
    idx = jnp.min(jnp.where(vals == m, lane, 1 << 20), axis=1, keepdims=True)
    return m, idx


def _route_kernel(lg_ref, bias_ref, idx_ref, w_ref, rank_ref, cnt_ref, run_s):
    @pl.when(pl.program_id(0) == 0)
    def _():
        run_s[...] = jnp.zeros_like(run_s)

    logits = lg_ref[...]
    tm = logits.shape[0]
    lane = lax.broadcasted_iota(jnp.int32, (tm, 128), 1)
    is_expert = lane < N_EXPERTS
    scores = jax.nn.sigmoid(logits)
    biased = jnp.where(is_expert, scores + bias_ref[...], NEG_INF)
    per_group = N_EXPERTS // N_EXPERT_GROUPS
    grp = lane // per_group
    gscore = jnp.full((tm, 128), NEG_INF, F32)
    for g in range(N_EXPERT_GROUPS):
        vals = jnp.where(grp == g, biased, NEG_INF)
        m1, i1 = _first_argmax(vals, lane)
        m2 = jnp.max(jnp.where(lane == i1, NEG_INF, vals), axis=1, keepdims=True)
        gscore = jnp.where(lane == g, m1 + m2, gscore)
    allowed = jnp.zeros((tm, 128), jnp.bool_)
    for _ in range(TOPK_GROUPS):
        _, gi = _first_argmax(gscore, lane)
        allowed = allowed | (grp == gi)
        gscore = jnp.where(lane == gi, NEG_INF, gscore)
    masked = jnp.where(allowed & is_expert, biased, NEG_INF)
    idx_out = jnp.zeros((tm, 128), jnp.int32)
    w_out = jnp.zeros((tm, 128), F32)
    picks = []
    chosen = jnp.zeros((tm, 128), F32)
    for kk in range(TOP_K):
        _, ei = _first_argmax(masked, lane)
        sel = lane == ei
        picks.append(sel)
        chosen = chosen + sel.astype(F32)
        wk = jnp.sum(jnp.where(sel, scores, 0.0), axis=1, keepdims=True)
        idx_out = jnp.where(lane == kk, ei, idx_out)
        w_out = jnp.where(lane == kk, wk, w_out)
        masked = jnp.where(sel, NEG_INF, masked)
    w_out = w_out / jnp.sum(w_out, axis=1, keepdims=True) * ROUTED_SCALE
    idx_ref[...] = idx_out
    w_ref[...] = w_out
    strict = (lax.broadcasted_iota(jnp.int32, (tm, tm), 1)
              < lax.broadcasted_iota(jnp.int32, (tm, tm), 0)).astype(F32)
    before = _dot(strict, chosen) + run_s[...]
    rank_out = jnp.zeros((tm, 128), jnp.int32)
    for kk in range(TOP_K):
        rk = jnp.sum(jnp.where(picks[kk], before, 0.0), axis=1, keepdims=True)
        rank_out = jnp.where(lane == kk, rk.astype(jnp.int32), rank_out)
    rank_ref[...] = rank_out
    total = run_s[...] + jnp.sum(chosen, axis=0, keepdims=True)
    run_s[...] = total
    cnt_ref[...] = total.astype(jnp.int32)


def route(logits, router_bias, tm=1024):
    t = logits.shape[0]
    bias = jnp.concatenate([router_bias.astype(F32), jnp.zeros((128 - N_EXPERTS,), F32)])[None]
    row = pl.BlockSpec((tm, 128), lambda i: (i, 0))
    one = pl.BlockSpec((1, 128), lambda i: (0, 0))
    return pl.pallas_call(
        _route_kernel,
        grid=(t // tm,),
        in_specs=[row, one],
        out_specs=[row, row, row, one],
        out_shape=[jax.ShapeDtypeStruct((t, 128), jnp.int32), jax.ShapeDtypeStruct((t, 128), F32),
                   jax.ShapeDtypeStruct((t, 128), jnp.int32), jax.ShapeDtypeStruct((1, 128), jnp.int32)],
        scratch_shapes=[pltpu.VMEM((1, 128), F32)],
        compiler_params=_cparams(("arbitrary",)),
        name="route",
    )(logits, bias)


EXP_TM = 256
SCT_TM = 256


def _scatter_kernel(lt_ref, pos_ref, hp_ref, xs_hbm, zero_s, sem, zsem):
    i = pl.program_id(0)

    @pl.when(i == 0)
    def _():
        zero_s[...] = jnp.zeros_like(zero_s)
        for e in range(2 * N_EXPERTS):
            @pl.when(lt_ref[e] >= 0)
            def _():
                pltpu.make_async_copy(zero_s, xs_hbm.at[pl.ds(lt_ref[e] * EXP_TM, EXP_TM), :], zsem).start()
        for e in range(2 * N_EXPERTS):
            @pl.when(lt_ref[e] >= 0)
            def _():
                pltpu.make_async_copy(zero_s, xs_hbm.at[pl.ds(lt_ref[e] * EXP_TM, EXP_TM), :], zsem).wait()

    def body(r, carry):
        for kk in range(TOP_K):
            p = pos_ref[0, r * TOP_K + kk]
            pltpu.make_async_copy(hp_ref.at[pl.ds(r, 1), :], xs_hbm.at[pl.ds(p, 1), :], sem).start(
                priority=kk % 2)
        return carry
    lax.fori_loop(0, SCT_TM, body, 0)
    for kk in range(TOP_K):
        pltpu.make_async_copy(hp_ref, xs_hbm.at[pl.ds(0, SCT_TM), :], sem).wait()


def scatter_rows(hp, pos, clear_tiles, n_rows):
    t, half = hp.shape
    n = t // SCT_TM
    grid_spec = pltpu.PrefetchScalarGridSpec(
        num_scalar_prefetch=1,
        grid=(n,),
        in_specs=[
            pl.BlockSpec((None, 1, SCT_TM * TOP_K), lambda i, lt: (i, 0, 0), memory_space=pltpu.SMEM),
            pl.BlockSpec((SCT_TM, half), lambda i, lt: (i, 0)),
        ],
        out_specs=pl.BlockSpec(memory_space=pl.ANY),
        scratch_shapes=[
            pltpu.VMEM((EXP_TM, half), jnp.uint32),
            pltpu.SemaphoreType.DMA(()),
            pltpu.SemaphoreType.DMA(()),
        ],
    )
    return pl.pallas_call(
        _scatter_kernel,
        grid_spec=grid_spec,
        out_shape=jax.ShapeDtypeStruct((n_rows, half), jnp.uint32),
        compiler_params=_cparams(("arbitrary",)),
        name="scatter_rows",
    )(clear_tiles, pos.reshape(n, 1, SCT_TM * TOP_K), hp)


def _expert_kernel(te_ref, nu_ref, x_ref, wg_ref, wu_ref, wd_ref, y_ref, wgu_s, wd_s):
    i = pl.program_id(0)
    n_used = nu_ref[0]
    half = x_ref.shape[1]
    new_expert = (i == 0) | (te_ref[i] != te_ref[jnp.maximum(i - 1, 0)])

    @pl.when(new_expert & (i < n_used))
    def _():
        wgu_s[:, 0:D_EXPERT] = wg_ref[...].astype(BF16)
        wgu_s[:, D_EXPERT:2 * D_EXPERT] = wu_ref[...].astype(BF16)
        wd_s[...] = wd_ref[...].astype(BF16)

    @pl.when(i < n_used)
    def _():
        x_hi, x_lo = _unpack_bf16_pair(x_ref[...])
        h = (jnp.dot(x_hi.astype(BF16), wgu_s[0:half, :], preferred_element_type=F32)
             + jnp.dot(x_lo.astype(BF16), wgu_s[half:2 * half, :], preferred_element_type=F32))
        act = _silu(h[:, 0:D_EXPERT]) * h[:, D_EXPERT:2 * D_EXPERT]
        y = jnp.dot(act.astype(BF16), wd_s[...], preferred_element_type=F32)
        y_ref[...] = _pack_bf16_pair(y[:, 0:half], y[:, half:2 * half])

    @pl.when(i >= n_used)
    def _():
        y_ref[...] = jnp.zeros_like(y_ref)


def routed_experts(x_sorted, tile_expert, n_used, w_gate, w_up, w_down, layer):
    n_rows, half = x_sorted.shape
    n_tiles = n_rows // EXP_TM
    d = 2 * half

    def used(i, nu):
        return jnp.minimum(i, nu[0] - 1)

    grid_spec = pltpu.PrefetchScalarGridSpec(
        num_scalar_prefetch=2,
        grid=(n_tiles,),
        in_specs=[
            pl.BlockSpec((EXP_TM, half), lambda i, te, nu: (used(i, nu), 0)),
            pl.BlockSpec((None, None, d, D_EXPERT), lambda i, te, nu: (layer, te[used(i, nu)], 0, 0)),
            pl.BlockSpec((None, None, d, D_EXPERT), lambda i, te, nu: (layer, te[used(i, nu)], 0, 0)),
            pl.BlockSpec((None, None, D_EXPERT, d), lambda i, te, nu: (layer, te[used(i, nu)], 0, 0)),
        ],
        out_specs=pl.BlockSpec((EXP_TM, half), lambda i, te, nu: (i, 0)),
        scratch_shapes=[
            pltpu.VMEM((d, 2 * D_EXPERT), BF16),
            pltpu.VMEM((D_EXPERT, d), BF16),
        ],
    )
    return pl.pallas_call(
        _expert_kernel,
        grid_spec=grid_spec,
        out_shape=jax.ShapeDtypeStruct((n_rows, half), jnp.uint32),
        compiler_params=_cparams(("arbitrary",)),
        name="routed_experts",
    )(tile_expert, n_used, x_sorted, w_gate, w_up, w_down)


CMB_TM = 128


def _start_combine_gather(pos_ref, y_hbm, dst, sem):
    def body(r, carry):
        for kk in range(TOP_K):
            p = pos_ref[0, r * TOP_K + kk]
            pltpu.make_async_copy(y_hbm.at[pl.ds(p, 1), :], dst.at[kk, pl.ds(r, 1), :], sem).start(
                priority=kk % 2)
        return carry
    lax.fori_loop(0, CMB_TM, body, 0)


def _combine_kernel(pos_ref, posn_ref, y_hbm, w_ref, hp_ref, x_ref, gate_ref, sgu_ref, sd_ref,
                    o_ref, ybuf, sem):
    i = pl.program_id(0)
    n = pl.num_programs(0)
    slot = i % 2
    half = hp_ref.shape[1]

    @pl.when(i == 0)
    def _():
        _start_combine_gather(pos_ref, y_hbm, ybuf.at[0], sem.at[0])

    @pl.when(i + 1 < n)
    def _():
        _start_combine_gather(posn_ref, y_hbm, ybuf.at[1 - slot], sem.at[1 - slot])

    x_hi, x_lo = _unpack_bf16_pair(hp_ref[...])
    h = (jnp.dot(x_hi.astype(BF16), sgu_ref[0:half, :], preferred_element_type=F32)
         + jnp.dot(x_lo.astype(BF16), sgu_ref[half:2 * half, :], preferred_element_type=F32))
    act = _silu(h[:, 0:D_SHARED]) * h[:, D_SHARED:2 * D_SHARED]
    shared = jnp.dot(act.astype(BF16), sd_ref[...], preferred_element_type=F32)

    for kk in range(TOP_K):
        pltpu.make_async_copy(y_hbm.at[pl.ds(0, CMB_TM), :], ybuf.at[slot, kk], sem.at[slot]).wait()
    acc_hi = shared[:, 0:half]
    acc_lo = shared[:, half:2 * half]
    w = w_ref[...]
    for kk in range(TOP_K):
        y_hi, y_lo = _unpack_bf16_pair(ybuf[slot, kk])
        wk = w[:, kk:kk + 1]
        acc_hi = acc_hi + wk * y_hi
        acc_lo = acc_lo + wk * y_lo
    g = gate_ref[...]
    o_ref[:, 0:half] = x_ref[:, 0:half] + g[:, 0:half] * acc_hi
    o_ref[:, half:2 * half] = x_ref[:, half:2 * half] + g[:, half:2 * half] * acc_lo


def moe_combine(pos, y_sorted, w, hp, x, gate, seg_of_tile, sh_gate_up, sh_down):
    t, d = x.shape
    half = d // 2
    n = t // CMB_TM
    return pl.pallas_call(
        _combine_kernel,
        grid=(n,),
        in_specs=[
            pl.BlockSpec((None, 1, CMB_TM * TOP_K), lambda i: (i, 0, 0), memory_space=pltpu.SMEM),
            pl.BlockSpec((None, 1, CMB_TM * TOP_K), lambda i: (jnp.minimum(i + 1, n - 1), 0, 0),
                         memory_space=pltpu.SMEM),
            pl.BlockSpec(memory_space=pl.ANY),
            pl.BlockSpec((CMB_TM, 128), lambda i: (i, 0)),
            pl.BlockSpec((CMB_TM, half), lambda i: (i, 0)),
            pl.BlockSpec((CMB_TM, d), lambda i: (i, 0)),
            pl.BlockSpec((None, 1, d), lambda i: (seg_of_tile(i), 0, 0)),
            pl.BlockSpec((d, 2 * D_SHARED), lambda i: (0, 0)),
            pl.BlockSpec((D_SHARED, d), lambda i: (0, 0)),
        ],
        out_specs=pl.BlockSpec((CMB_TM, d), lambda i: (i, 0)),
        out_shape=jax.ShapeDtypeStruct((t, d), F32),
        scratch_shapes=[
            pltpu.VMEM((2, TOP_K, CMB_TM, half), jnp.uint32),
            pltpu.SemaphoreType.DMA((2,)),
        ],
        compiler_params=_cparams(("arbitrary",)),
        name="moe_combine",
    )(pos, pos, y_sorted, w, hp, x, gate, sh_gate_up, sh_down)


def moe_plan(eidx, rank, counts, n_tiles):
    tiles_per = (counts + EXP_TM - 1) // EXP_TM
    tile_end = jnp.cumsum(tiles_per)
    base = (tile_end - tiles_per) * EXP_TM
    experts = jnp.arange(N_EXPERTS, dtype=jnp.int32)
    pos = rank + jnp.sum(jnp.where(eidx[..., None] == experts, base, 0), axis=-1)
    tile_expert = jnp.sum((tile_end[None, :] <= jnp.arange(n_tiles, dtype=jnp.int32)[:, None])
                          .astype(jnp.int32), axis=1)
    tile_expert = jnp.minimum(tile_expert, N_EXPERTS - 1)
    last_tile = jnp.where(tiles_per > 0, tile_end - 1, -1)
    unused = tile_end[-1] + experts
    clear_tiles = jnp.concatenate([last_tile, jnp.where(unused < n_tiles, unused, -1)])
    return tile_expert, tile_end[-1:], clear_tiles, pos


def _rmsnorm_kernel(x_ref, g_ref, o_ref):
    x = x_ref[...]
    o_ref[...] = x * lax.rsqrt(jnp.mean(x * x, axis=-1, keepdims=True) + EPS) * g_ref[...]


def final_rmsnorm(x, g, tm=256):
    t, d = x.shape
    return pl.pallas_call(
        _rmsnorm_kernel,
        grid=(t // tm,),
        in_specs=[pl.BlockSpec((tm, d), lambda i: (i, 0)), pl.BlockSpec((1, d), lambda i: (0, 0))],
        out_specs=pl.BlockSpec((tm, d), lambda i: (i, 0)),
        out_shape=jax.ShapeDtypeStruct((t, d), F32),
        compiler_params=_cparams(("arbitrary",)),
        name="final_rmsnorm",
    )(x, g.reshape(1, d))


def lb_table(lb):
    lb = lb.astype(F32)
    return jnp.stack([jnp.log(lb), jnp.log1p(-lb), 1.0 - lb])


def mlstm_gate_bias(i_bias, f_bias):
    z = jnp.zeros((64,), F32)
    return jnp.concatenate([z, i_bias.astype(F32).reshape(-1), f_bias.astype(F32).reshape(-1),
                            jnp.zeros((32,), F32)])[None]


def ssd_consts(dt_bias, a_log):
    bias = jnp.concatenate([dt_bias.astype(F32).reshape(-1), jnp.zeros((64,), F32)])[None]
    a = -jnp.exp(a_log.astype(F32))
    expand = (jnp.arange(B_HEADS)[:, None] == jnp.arange(B_WIDTH)[None, :] // B_HEADDIM).astype(BF16)
    return [(bias, a[d][None], expand) for d in range(2)]


def _mod_tables(mod, nb):
    d = mod.shape[1] // 6
    parts = mod.reshape(mod.shape[0], 6, d)
    lat = parts[:nb]
    ctx = jnp.broadcast_to(parts[nb][None], (nb, 6, d))
    tab = jnp.stack([ctx, lat], axis=1).reshape(2 * nb, 6, d)
    return jnp.transpose(tab, (1, 0, 2))[:, :, None, :]


def _seg_fn(rows_per_batch, ctx_rows, tm):
    tiles_per_batch = rows_per_batch // tm
    ctx_tiles = ctx_rows // tm

    def seg(i):
        return 2 * (i // tiles_per_batch) + jnp.where(i % tiles_per_batch >= ctx_tiles, 1, 0)
    return seg


def _to_scan_order(a, rows):
    nb, _, d = a.shape
    return a.reshape(nb, rows, GRID_W, d).transpose(0, 2, 1, 3).reshape(nb, rows * GRID_W, d)


def _from_scan_order(a, rows):
    nb, _, d = a.shape
    return a.reshape(nb, GRID_W, rows, d).transpose(0, 2, 1, 3).reshape(nb, rows * GRID_W, d)


def _moe_block(x2, tab, seg_mod, seg_cmb, layer, norm_w, router_w, router_bias, w_gate, w_up, w_down,
               sh_gate, sh_up, sh_down):
    t, d = x2.shape
    rw = jnp.concatenate([router_w.astype(F32), jnp.zeros((d, 128 - N_EXPERTS), F32)], axis=1)
    rw_hi = rw.astype(BF16)
    rw = jnp.concatenate([rw_hi, (rw - rw_hi.astype(F32)).astype(BF16)], axis=1)
    hp, logits = modulate(x2, norm_w, tab[3], tab[4], seg_mod, 256, router_w=rw)
    eidx, ew, rank, counts = route(logits, router_bias)
    n_tiles = t * TOP_K // EXP_TM + N_EXPERTS
    tile_expert, n_used, clear_tiles, pos = moe_plan(eidx[:, :TOP_K], rank[:, :TOP_K],
                                                     counts[0, :N_EXPERTS], n_tiles)
    x_sorted = scatter_rows(hp, pos, clear_tiles, n_tiles * EXP_TM)
    y_sorted = routed_experts(x_sorted, tile_expert, n_used, w_gate, w_up, w_down, layer)
    sgu = jnp.concatenate([sh_gate, sh_up], axis=1).astype(BF16)
    return moe_combine(pos.reshape(t // CMB_TM, 1, CMB_TM * TOP_K), y_sorted, ew, hp, x2, tab[5],
                       seg_cmb, sgu, sh_down.astype(BF16))


def kernel(x, c, ctx, c_ctx, ada_w, ada_b, norm_mix, norm_ffn, norm_final, w_in, w_out, hgrn_lb,
           hgrn_norm, ssm_conv_w, ssm_conv_b, ssm_dt_bias, ssm_a_log, ssm_d, ssm_norm, mlstm_i_bias,
           mlstm_f_bias, mlstm_norm, router_w, router_bias, moe_w_gate, moe_w_up, moe_w_down,
           shared_w_gate, shared_w_up, shared_w_down):
    nb, seq, d = x.shape
    n_ctx = ctx.shape[1]
    length = n_ctx + seq
    rows = seq // GRID_W
    depth = ada_w.shape[0]
    t_all = nb * length
    t_lat = nb * seq

    lb_all = jnp.cumsum(jax.nn.softmax(hgrn_lb.astype(F32), axis=0), axis=0)
    lb_all = lb_all - lb_all[0]
    cond = jnp.concatenate([c, c_ctx[None], jnp.zeros((8 - nb - 1, d), F32)], axis=0)
    mod = ada_modulation(cond, ada_w, ada_b)

    xs = jnp.concatenate([ctx, x], axis=1)
    for l in range(depth):
        last = l == depth - 1
        tab = _mod_tables(mod[l], nb)
        h = modulate(xs.reshape(t_all, d), norm_mix[l], tab[0], tab[1], _seg_fn(length, n_ctx, 256), 256)
        if l % 2 == 1:
            h3 = h.reshape(nb, length, d)
            h = jnp.concatenate([h3[:, :n_ctx], _to_scan_order(h3[:, n_ctx:], rows)], axis=1)
            h = h.reshape(t_all, d)
        w_ab = w_in[l, :, :U_AB].astype(BF16)
        w_c = w_in[l, :, U_C0:U_G0].astype(BF16)
        w_small = jnp.concatenate([w_in[l, :, U_DT0:U_C0], w_in[l, :, U_G0:],
                                   jnp.zeros((d, U_SMALL - 96), F32)], axis=1).astype(BF16)
        u_ab = matmul(h, w_ab, tm=1024, tn=512, name="in_proj_ab")
        u_c = matmul(h, w_c, tm=1024, tn=512, name="in_proj_c")
        u_s = matmul(h, w_small, tm=1024, tn=U_SMALL, name="in_proj_small")
        u_ab3 = u_ab.reshape(nb, length, U_AB)
        u_c3 = u_c.reshape(nb, length, 4 * C_WIDTH)
        u_s3 = u_s.reshape(nb, length, U_SMALL)

        xc = ssd_conv(u_ab3, ssm_conv_w[l], ssm_conv_b[l], n_ctx)
        consts = ssd_consts(ssm_dt_bias[l], ssm_a_log[l])
        gate_bias = mlstm_gate_bias(mlstm_i_bias[l], mlstm_f_bias[l])
        ya = [hgrn2_scan(u_ab3, lb_table(lb_all[l][dd]), n_ctx, rev=bool(dd)) for dd in range(2)]
        yb = [ssd_scan(xc, u_s3, *consts[dd], n_ctx, rev=bool(dd)) for dd in range(2)]
        yc = mlstm_scan(u_c3, u_s3, gate_bias, n_ctx)
        flat = lambda a: a.reshape(t_all, a.shape[-1])
        ymix = mix_out(flat(ya[0]), flat(ya[1]), flat(yb[0]), flat(yb[1]), flat(yc[0]), flat(yc[1]),
                       u_ab, flat(xc), u_c, hgrn_norm[l][None], ssm_norm[l][None], mlstm_norm[l][None],
                       jnp.repeat(ssm_d[l].astype(F32), B_HEADDIM)[None])
        if l % 2 == 1:
            y3 = ymix.reshape(nb, length, d)
            ymix = jnp.concatenate([y3[:, :n_ctx], _from_scan_order(y3[:, n_ctx:], rows)], axis=1)
            ymix = ymix.reshape(t_all, d)

        moe_w = (l, norm_ffn[l], router_w[l], router_bias[l], moe_w_gate, moe_w_up, moe_w_down,
                 shared_w_gate[l], shared_w_up[l], shared_w_down[l])
        ymix = ymix.reshape(nb, length, d)
        if not last:
            x2 = matmul_residual(ymix, w_out, l, xs, tab[2], first_row=0, n_rows=length,
                                 ctx_rows=n_ctx, tm=256, tn=512).reshape(t_all, d)
            x2 = _moe_block(x2, tab, _seg_fn(length, n_ctx, 256), _seg_fn(length, n_ctx, CMB_TM), *moe_w)
            xs = x2.reshape(nb, length, d)
        else:
            x2 = matmul_residual(ymix, w_out, l, xs, tab[2], first_row=n_ctx, n_rows=seq,
                                 ctx_rows=n_ctx, tm=256, tn=512).reshape(t_lat, d)
            x2 = _moe_block(x2, tab, _seg_fn(seq, 0, 256), _seg_fn(seq, 0, CMB_TM), *moe_w)
            return final_rmsnorm(x2, norm_final).reshape(nb, seq, d)
```

```python
import functools

import jax
import jax.numpy as jnp
from jax import lax
from jax.experimental import pallas as pl
from jax.experimental.pallas import tpu as pltpu

F32 = jnp.float32
BF16 = jnp.bfloat16
HIGHEST = lax.Precision.HIGHEST
NEG_INF = float("-inf")

D_MODEL = 4096
GRID_W = 64
CHUNK = 64
EPS = 1e-6
A_WIDTH = 1024
A_DK = 128
A_HEADS = 8
B_WIDTH = 2048
B_HEADDIM = 64
B_HEADS = 32
B_GROUPS = 8
B_RPG = 4
B_STATE = 128
B_CONV = 5
B_CONV_CH = 4096
C_WIDTH = 1024
C_DK = 128
C_HEADS = 8
N_EXPERTS = 64
N_EXPERT_GROUPS = 8
TOPK_GROUPS = 4
TOP_K = 8
D_EXPERT = 256
D_SHARED = 256
ROUTED_SCALE = 2.5

U_AB = 11264
U_DT0 = 11264
U_C0 = 11328
U_G0 = 15424
U_SMALL = 128

SUB = 16
PAD = 16
VMEM_LIMIT = 56 * 1024 * 1024


def _cparams(sem):
    return pltpu.CompilerParams(dimension_semantics=sem, vmem_limit_bytes=VMEM_LIMIT)


def _dot(a, b):
    return jnp.dot(a.astype(BF16), b.astype(BF16), preferred_element_type=F32)


def _bdot(a, b, ca, cb):
    return lax.dot_general(a.astype(BF16), b.astype(BF16), (((ca,), (cb,)), ((0,), (0,))),
                           preferred_element_type=F32)


def _dot_hi(a, b):
    return jnp.dot(a, b, precision=HIGHEST, preferred_element_type=F32)


def _dot_nt_hi(a, b):
    return lax.dot_general(a, b, (((1,), (1,)), ((), ())), precision=HIGHEST,
                           preferred_element_type=F32)


def _log_sigmoid(z):
    return jnp.minimum(z, 0.0) - jnp.log1p(jnp.exp(-jnp.abs(z)))


def _softplus(z):
    return jnp.maximum(z, 0.0) + jnp.log1p(jnp.exp(-jnp.abs(z)))


def _silu(z):
    return z * jax.nn.sigmoid(z)


def _tri(rev):
    ri = lax.broadcasted_iota(jnp.int32, (CHUNK, CHUNK), 0)
    ci = lax.broadcasted_iota(jnp.int32, (CHUNK, CHUNK), 1)
    return (ci >= ri) if rev else (ci <= ri)


def _chunk_block(c, n_ctx_chunks, n_lat_chunks, rev):
    if not rev:
        return c
    return jnp.where(c < n_ctx_chunks, n_ctx_chunks - 1 - c, 2 * n_ctx_chunks + n_lat_chunks - 1 - c)


def _ada_kernel(c_ref, w_ref, b_ref, o_ref):
    s = _silu(c_ref[...])
    o_ref[...] = _dot(s, w_ref[...]) + b_ref[...]


def ada_modulation(cond, ada_w, ada_b):
    depth, d, n = ada_w.shape
    tn = 512
    return pl.pallas_call(
        _ada_kernel,
        grid=(depth, n // tn),
        in_specs=[
            pl.BlockSpec((8, d), lambda l, j: (0, 0)),
            pl.BlockSpec((None, d, tn), lambda l, j: (l, 0, j)),
            pl.BlockSpec((None, 1, tn), lambda l, j: (l, 0, j)),
        ],
        out_specs=pl.BlockSpec((None, 8, tn), lambda l, j: (l, 0, j)),
        out_shape=jax.ShapeDtypeStruct((depth, 8, n), F32),
        compiler_params=_cparams(("arbitrary", "arbitrary")),
        name="ada_modulation",
    )(cond, ada_w, ada_b.reshape(depth, 1, n))


def _pack_bf16_pair(a, b):
    hi = lax.bitcast_convert_type(a.astype(BF16).astype(F32), jnp.uint32)
    lo = lax.bitcast_convert_type(b.astype(BF16).astype(F32), jnp.uint32)
    return hi | (lo >> 16)


def _unpack_bf16_pair(w):
    hi = lax.bitcast_convert_type(w & jnp.uint32(0xFFFF0000), F32)
    lo = lax.bitcast_convert_type(w << 16, F32)
    return hi, lo


def _modulate_kernel(x_ref, g_ref, sh_ref, sc_ref, *rest, with_router):
    x = x_ref[...]
    y = x * lax.rsqrt(jnp.mean(x * x, axis=-1, keepdims=True) + EPS) * g_ref[...]
    h = y * (1.0 + sc_ref[...]) + sh_ref[...]
    if with_router:
        rw_ref, h_ref, lg_ref = rest
        h_hi = h.astype(BF16)
        h_lo = (h - h_hi.astype(F32)).astype(BF16)
        both = jnp.dot(h_hi, rw_ref[...], preferred_element_type=F32)
        lg_ref[...] = (both[:, 0:128] + both[:, 128:256]
                       + jnp.dot(h_lo, rw_ref[:, 0:128], preferred_element_type=F32))
        half = h.shape[1] // 2
        h_ref[...] = _pack_bf16_pair(h[:, :half], h[:, half:])
    else:
        (h_ref,) = rest
        h_ref[...] = h.astype(h_ref.dtype)


def modulate(x, g, shift, scale, seg_of_tile, tm, router_w=None):
    t, d = x.shape
    with_router = router_w is not None
    vec = pl.BlockSpec((None, 1, d), lambda i: (seg_of_tile(i), 0, 0))
    in_specs = [pl.BlockSpec((tm, d), lambda i: (i, 0)),
                pl.BlockSpec((1, d), lambda i: (0, 0)), vec, vec]
    args = [x, g.reshape(1, d), shift, scale]
    if with_router:
        in_specs.append(pl.BlockSpec((d, 256), lambda i: (0, 0)))
        args.append(router_w)
        out_specs = [pl.BlockSpec((tm, d // 2), lambda i: (i, 0)),
                     pl.BlockSpec((tm, 128), lambda i: (i, 0))]
        out_shape = [jax.ShapeDtypeStruct((t, d // 2), jnp.uint32),
                     jax.ShapeDtypeStruct((t, 128), F32)]
    else:
        out_specs = pl.BlockSpec((tm, d), lambda i: (i, 0))
        out_shape = jax.ShapeDtypeStruct((t, d), BF16)
    return pl.pallas_call(
        functools.partial(_modulate_kernel, with_router=with_router),
        grid=(t // tm,),
        in_specs=in_specs, out_specs=out_specs, out_shape=out_shape,
        compiler_params=_cparams(("arbitrary",)),
        name="modulate_router" if with_router else "modulate",
    )(*args)


def _mm_nt_kernel(a_ref, b_ref, o_ref):
    o_ref[...] = lax.dot_general(a_ref[...], b_ref[...].astype(BF16), (((1,), (1,)), ((), ())),
                                 preferred_element_type=F32)


def matmul_nt(a, b_t, layer, *, tm, tn, n_tiles=None, name="matmul_nt"):
    m, k = a.shape
    n_tiles = b_t.shape[1] // tn if n_tiles is None else n_tiles
    return pl.pallas_call(
        _mm_nt_kernel,
        grid=(m // tm, n_tiles),
        in_specs=[pl.BlockSpec((tm, k), lambda i, j: (i, 0)),
                  pl.BlockSpec((None, tn, k), lambda i, j: (layer, j, 0))],
        out_specs=pl.BlockSpec((tm, tn), lambda i, j: (i, j)),
        out_shape=jax.ShapeDtypeStruct((m, n_tiles * tn), F32),
        compiler_params=_cparams(("arbitrary", "arbitrary")),
        name=name,
    )(a, b_t)


def _mm_res_kernel(a_ref, b_ref, r_ref, g_ref, o_ref):
    acc = jnp.dot(a_ref[...], b_ref[...], preferred_element_type=F32)
    o_ref[...] = r_ref[...] + g_ref[...] * acc


def matmul_residual(a, b, res, gate, *, first_row, n_rows, ctx_rows, tm, tn):
    nb, _, k = a.shape
    n = b.shape[1]
    t0 = first_row // tm
    ctx_tiles = ctx_rows // tm
    seg = lambda bb, it: 2 * bb + jnp.where(t0 + it >= ctx_tiles, 1, 0)
    return pl.pallas_call(
        _mm_res_kernel,
        grid=(n // tn, nb, n_rows // tm),
        in_specs=[pl.BlockSpec((None, tm, k), lambda j, bb, it: (bb, t0 + it, 0)),
                  pl.BlockSpec((k, tn), lambda j, bb, it: (0, j)),
                  pl.BlockSpec((None, tm, tn), lambda j, bb, it: (bb, t0 + it, j)),
                  pl.BlockSpec((None, 1, tn), lambda j, bb, it: (seg(bb, it), 0, j))],
        out_specs=pl.BlockSpec((None, tm, tn), lambda j, bb, it: (bb, it, j)),
        out_shape=jax.ShapeDtypeStruct((nb, n_rows, n), F32),
        compiler_params=_cparams(("arbitrary", "arbitrary", "arbitrary")),
        name="matmul_residual",
    )(a, b, res, gate)


def _hgrn2_kernel(q_ref, f_ref, v_ref, lb_ref, y_ref, k_s, v_s, gc_s, st_s, *, rev):
    c = pl.program_id(1)

    @pl.when(c == 0)
    def _():
        st_s[...] = jnp.zeros_like(st_s)
        k_s[...] = jnp.zeros_like(k_s)
        v_s[...] = jnp.zeros_like(v_s)
        gc_s[...] = jnp.zeros_like(gc_s)

    nh = A_HEADS
    q = q_ref[0]
    z = f_ref[0]
    log_lb = lb_ref[0:1, :]
    log1m_lb = lb_ref[1:2, :]
    one_m_lb = lb_ref[2:3, :]
    b2 = log1m_lb + _log_sigmoid(z)
    logf = jnp.maximum(log_lb, b2) + jnp.log1p(jnp.exp(-jnp.abs(log_lb - b2)))
    gc2 = _dot_hi(_tri(rev).astype(F32), logf)
    heads = lambda a: jnp.stack([a[:, h * A_DK:(h + 1) * A_DK] for h in range(nh)])
    qh = heads(q * jax.nn.sigmoid(q) * (A_DK ** -0.5))
    kh = heads(one_m_lb * jax.nn.sigmoid(-z))
    vh = heads(v_ref[0])
    gcm = heads(gc2)
    gxm = gcm - heads(logf)
    k_s[:, PAD:PAD + CHUNK, :] = kh
    v_s[:, PAD:PAD + CHUNK, :] = vh
    gc_s[:, PAD:PAD + CHUNK, :] = gcm

    rows = lax.broadcasted_iota(jnp.int32, (1, CHUNK, 1), 1)
    rowmod = rows % SUB
    n_sub = CHUNK // SUB
    st = st_s[...]
    y = _bdot(qh * jnp.exp(gcm), st, 2, 2)
    for dlt in range(SUB):
        off = PAD + dlt if rev else PAD - dlt
        ksh = k_s[:, off:off + CHUNK, :]
        vsh = v_s[:, off:off + CHUNK, :]
        gsh = gc_s[:, off:off + CHUNK, :]
        valid = (rowmod + dlt < SUB) if rev else (rowmod >= dlt)
        e = jnp.exp(jnp.where(valid, gcm - gsh, NEG_INF))
        r = jnp.sum(qh * ksh * e, axis=2, keepdims=True)
        y = y + r * vsh
    pieces = []
    for blk in range(n_sub):
        r0, r1 = blk * SUB, (blk + 1) * SUB
        first = (blk == n_sub - 1) if rev else (blk == 0)
        if first:
            pieces.append(jnp.zeros((nh, SUB, A_DK), F32))
            continue
        ref = gxm[:, r1 - 1:r1, :] if rev else gxm[:, r0:r0 + 1, :]
        earlier = (rows >= r1) if rev else (rows < r0)
        qhat = qh[:, r0:r1, :] * jnp.exp(gcm[:, r0:r1, :] - ref)
        khat = kh * jnp.exp(jnp.where(earlier, ref - gcm, NEG_INF))
        sc = _bdot(qhat, khat, 2, 2)
        pieces.append(_bdot(sc, vh, 2, 1))
    y = y + jnp.concatenate(pieces, axis=1)
    gtot = gcm[:, 0:1, :] if rev else gcm[:, CHUNK - 1:CHUNK, :]
    st_s[...] = st * jnp.exp(gtot) + _bdot(vh, kh * jnp.exp(gtot - gcm), 1, 1)
    for h in range(nh):
        y_ref[0, :, h * A_DK:(h + 1) * A_DK] = y[h]


def hgrn2_scan(u_ab, lb_tab, n_ctx, rev):
    nb, length, _ = u_ab.shape
    n_chunks = length // CHUNK
    n_ctx_chunks = n_ctx // CHUNK
    blk = functools.partial(_chunk_block, n_ctx_chunks=n_ctx_chunks,
                            n_lat_chunks=n_chunks - n_ctx_chunks, rev=rev)
    fcol = 2 if rev else 1

    def col(j):
        return pl.BlockSpec((1, CHUNK, A_WIDTH), lambda b, c: (b, blk(c), j))

    return pl.pallas_call(
        functools.partial(_hgrn2_kernel, rev=rev),
        grid=(nb, n_chunks),
        in_specs=[col(0), col(fcol), col(3), pl.BlockSpec((3, A_WIDTH), lambda b, c: (0, 0))],
        out_specs=pl.BlockSpec((1, CHUNK, A_WIDTH), lambda b, c: (b, blk(c), 0)),
        out_shape=jax.ShapeDtypeStruct((nb, length, A_WIDTH), F32),
        scratch_shapes=[
            pltpu.VMEM((A_HEADS, CHUNK + 2 * PAD, A_DK), F32),
            pltpu.VMEM((A_HEADS, CHUNK + 2 * PAD, A_DK), F32),
            pltpu.VMEM((A_HEADS, CHUNK + 2 * PAD, A_DK), F32),
            pltpu.VMEM((A_HEADS, A_DK, A_DK), F32),
        ],
        compiler_params=_cparams(("arbitrary", "arbitrary")),
        name="hgrn2_bwd" if rev else "hgrn2_fwd",
    )(u_ab, u_ab, u_ab, lb_tab)


def _mlstm_kernel(qf_ref, kf_ref, vf_ref, gf_ref, qb_ref, kb_ref, vb_ref, gb_ref, bias_ref,
                  hf_ref, hb_ref, c_s, n_s, m_s):
    @pl.when(pl.program_id(1) == 0)
    def _():
        c_s[...] = jnp.zeros_like(c_s)
        n_s[...] = jnp.zeros_like(n_s)
        m_s[...] = jnp.zeros_like(m_s)

    nh = C_HEADS
    eye = (lax.broadcasted_iota(jnp.int32, (nh, nh), 0)
           == lax.broadcasted_iota(jnp.int32, (nh, nh), 1)).astype(F32)
    heads = lambda ref: [ref[0, :, h * C_DK:(h + 1) * C_DK] for h in range(nh)]
    bcols, igcols, arows, masks = [], [], [], []
    for d, g_ref in enumerate((gf_ref, gb_ref)):
        gates = g_ref[0] + bias_ref[...]
        ig_all = gates[:, 64 + 8 * d:72 + 8 * d]
        lf_all = _log_sigmoid(gates[:, 80 + 8 * d:88 + 8 * d])
        mask = _tri(bool(d))
        b_all = _dot_hi(mask.astype(F32), lf_all)
        arows.append(_dot_nt_hi(eye, ig_all - b_all))
        bcols += [b_all[:, h:h + 1] for h in range(nh)]
        igcols += [ig_all[:, h:h + 1] for h in range(nh)]
        masks.append(jnp.broadcast_to(mask[None], (nh, CHUNK, CHUNK)))
    q = jnp.stack(heads(qf_ref) + heads(qb_ref)) * (C_DK ** -0.5)
    k = jnp.stack(heads(kf_ref) + heads(kb_ref))
    v = jnp.stack(heads(vf_ref) + heads(vb_ref))
    bcol = jnp.stack(bcols)
    igcol = jnp.stack(igcols)
    arow = jnp.concatenate(arows, axis=0)[:, None, :]
    mask = jnp.concatenate(masks, axis=0)
    dmat = jnp.where(mask, bcol + arow, NEG_INF)
    m0 = m_s[...]
    m_inter = bcol + m0
    m_i = jnp.maximum(m_inter, jnp.max(dmat, axis=2, keepdims=True))
    w_intra = jnp.exp(dmat - m_i)
    w_inter = jnp.exp(m_inter - m_i)
    s = _bdot(q, k, 2, 2) * w_intra
    cmat = c_s[...]
    nvec = n_s[...]
    num = _bdot(s, v, 2, 1) + w_inter * _bdot(q, cmat, 2, 1)
    den = jnp.sum(s, axis=2, keepdims=True) + w_inter * jnp.sum(q * nvec, axis=2, keepdims=True)
    out = num / jnp.maximum(jnp.abs(den), jnp.exp(-m_i))
    for h in range(nh):
        hf_ref[0, :, h * C_DK:(h + 1) * C_DK] = out[h]
        hb_ref[0, :, h * C_DK:(h + 1) * C_DK] = out[nh + h]
    b_last = jnp.concatenate([bcol[:nh, CHUNK - 1:CHUNK, :], bcol[nh:, 0:1, :]], axis=0)
    log_wj = b_last - bcol + igcol
    m_new = jnp.maximum(b_last + m0, jnp.max(log_wj, axis=1, keepdims=True))
    w0 = jnp.exp(b_last + m0 - m_new)
    wk = jnp.exp(log_wj - m_new) * k
    c_s[...] = w0 * cmat + _bdot(wk, v, 1, 1)
    n_s[...] = w0 * nvec + jnp.sum(wk, axis=1, keepdims=True)
    m_s[...] = m_new


def mlstm_scan(u_c, u_s, gate_bias, n_ctx):
    nb, length, _ = u_c.shape
    n_chunks = length // CHUNK
    n_ctx_chunks = n_ctx // CHUNK
    blks = [functools.partial(_chunk_block, n_ctx_chunks=n_ctx_chunks,
                              n_lat_chunks=n_chunks - n_ctx_chunks, rev=rev) for rev in (False, True)]
    in_specs = []
    for blk in blks:
        in_specs += [pl.BlockSpec((1, CHUNK, C_WIDTH), lambda b, c, j=j, blk=blk: (b, blk(c), j))
                     for j in range(3)]
        in_specs.append(pl.BlockSpec((1, CHUNK, U_SMALL), lambda b, c, blk=blk: (b, blk(c), 0)))
    in_specs.append(pl.BlockSpec((1, U_SMALL), lambda b, c: (0, 0)))
    return pl.pallas_call(
        _mlstm_kernel,
        grid=(nb, n_chunks),
        in_specs=in_specs,
        out_specs=[pl.BlockSpec((1, CHUNK, C_WIDTH), lambda b, c, blk=blk: (b, blk(c), 0)) for blk in blks],
        out_shape=[jax.ShapeDtypeStruct((nb, length, C_WIDTH), F32)] * 2,
        scratch_shapes=[
            pltpu.VMEM((2 * C_HEADS, C_DK, C_DK), F32),
            pltpu.VMEM((2 * C_HEADS, 1, C_DK), F32),
            pltpu.VMEM((2 * C_HEADS, 1, 1), F32),
        ],
        compiler_params=_cparams(("arbitrary", "arbitrary")),
        name="mlstm_scan",
    )(u_c, u_c, u_c, u_s, u_c, u_c, u_c, u_s, gate_bias)


CONV_TM = 256
CONV_TN = 1024
HALO = 8


def _conv_kernel(prev_ref, cur_ref, next_ref, w_ref, b_ref, o_ref, win_s, *, tiles_ctx, tiles_all):
    i = pl.program_id(1)
    seg_start = (i == 0) | (i == tiles_ctx)
    seg_end = (i == tiles_ctx - 1) | (i == tiles_all - 1)
    win_s[0:HALO, :] = jnp.where(seg_start, 0.0, prev_ref[0])
    win_s[HALO:HALO + CONV_TM, :] = cur_ref[0]
    win_s[HALO + CONV_TM:2 * HALO + CONV_TM, :] = jnp.where(seg_end, 0.0, next_ref[0])
    acc = b_ref[...] + jnp.zeros((CONV_TM, CONV_TN), F32)
    for t in range(B_CONV):
        o = HALO + t - B_CONV // 2
        acc = acc + w_ref[t:t + 1, :] * win_s[o:o + CONV_TM, :]
    o_ref[0] = _silu(acc)


def ssd_conv(u_ab, conv_w, conv_b, n_ctx):
    nb, length, _ = u_ab.shape
    tiles_all = length // CONV_TM
    tiles_ctx = n_ctx // CONV_TM
    col0 = (U_AB - B_CONV_CH) // CONV_TN
    per = CONV_TM // HALO
    n_halo = length // HALO
    return pl.pallas_call(
        functools.partial(_conv_kernel, tiles_ctx=tiles_ctx, tiles_all=tiles_all),
        grid=(nb, tiles_all, B_CONV_CH // CONV_TN),
        in_specs=[
            pl.BlockSpec((1, HALO, CONV_TN), lambda b, i, j: (b, jnp.maximum(i * per - 1, 0), col0 + j)),
            pl.BlockSpec((1, CONV_TM, CONV_TN), lambda b, i, j: (b, i, col0 + j)),
            pl.BlockSpec((1, HALO, CONV_TN),
                         lambda b, i, j: (b, jnp.minimum((i + 1) * per, n_halo - 1), col0 + j)),
            pl.BlockSpec((B_CONV, CONV_TN), lambda b, i, j: (0, j)),
            pl.BlockSpec((1, CONV_TN), lambda b, i, j: (0, j)),
        ],
        out_specs=pl.BlockSpec((1, CONV_TM, CONV_TN), lambda b, i, j: (b, i, j)),
        out_shape=jax.ShapeDtypeStruct((nb, length, B_CONV_CH), F32),
        scratch_shapes=[pltpu.VMEM((CONV_TM + 2 * HALO, CONV_TN), F32)],
        compiler_params=_cparams(("arbitrary", "arbitrary", "arbitrary")),
        name="ssd_conv",
    )(u_ab, u_ab, u_ab, conv_w, conv_b.reshape(1, B_CONV_CH))


def _ssd_kernel(xc_ref, us_ref, bias_ref, an_ref, e_ref, y_ref, s_s, *, rev):
    c = pl.program_id(1)

    @pl.when(c == 0)
    def _():
        s_s[...] = jnp.zeros_like(s_s)

    d = 1 if rev else 0
    ng = B_GROUPS
    gw = B_RPG * B_HEADDIM
    mask = _tri(rev)
    last = 0 if rev else CHUNK - 1
    dt_n = _softplus(us_ref[0] + bias_ref[...])[:, 32 * d:32 * d + B_HEADS]
    cum_n = _dot_hi(mask.astype(F32), dt_n * an_ref[...])
    eye = (lax.broadcasted_iota(jnp.int32, (B_HEADS, B_HEADS), 0)
           == lax.broadcasted_iota(jnp.int32, (B_HEADS, B_HEADS), 1)).astype(F32)
    cum_t = _dot_nt_hi(eye, cum_n)
    decay_in = jnp.exp(cum_n)
    decay_out = jnp.exp(cum_n[last:last + 1, :] - cum_n)
    spread = _dot(jnp.concatenate([dt_n, decay_in, decay_out], axis=0), e_ref[...])
    groups = lambda a: jnp.stack([a[:, g * gw:(g + 1) * gw] for g in range(ng)])
    dt_x = groups(spread[0:CHUNK])
    in_x = groups(spread[CHUNK:2 * CHUNK])
    out_x = groups(spread[2 * CHUNK:3 * CHUNK])
    xdt = jnp.stack([xc_ref[0, :, g * gw:(g + 1) * gw] for g in range(ng)]) * dt_x
    bm = jnp.stack([xc_ref[0, :, B_WIDTH + g * B_STATE:B_WIDTH + (g + 1) * B_STATE]
                    for g in range(ng)])
    c0 = B_WIDTH + B_GROUPS * B_STATE
    cm = jnp.stack([xc_ref[0, :, c0 + g * B_STATE:c0 + (g + 1) * B_STATE] for g in range(ng)])
    state = s_s[...]
    cb = _bdot(cm, bm, 2, 2)
    y = in_x * _bdot(cm, state, 2, 1)
    head_of_lane = lax.broadcasted_iota(jnp.int32, (1, 1, gw), 2) // B_HEADDIM
    for r in range(B_RPG):
        ccol = jnp.stack([cum_n[:, g * B_RPG + r:g * B_RPG + r + 1] for g in range(ng)])
        crow = jnp.stack([cum_t[g * B_RPG + r:g * B_RPG + r + 1, :] for g in range(ng)])
        lmat = cb * jnp.exp(jnp.where(mask[None], ccol - crow, NEG_INF))
        y = y + _bdot(lmat, jnp.where(head_of_lane == r, xdt, 0.0), 2, 1)
    s_s[...] = in_x[:, last:last + 1, :] * state + _bdot(bm, xdt * out_x, 1, 1)
    for g in range(ng):
        y_ref[0, :, g * gw:(g + 1) * gw] = y[g]


def ssd_scan(xc, u_s, dt_bias, a_n, expand, n_ctx, rev):
    nb, length, _ = xc.shape
    n_chunks = length // CHUNK
    n_ctx_chunks = n_ctx // CHUNK
    blk = functools.partial(_chunk_block, n_ctx_chunks=n_ctx_chunks,
                            n_lat_chunks=n_chunks - n_ctx_chunks, rev=rev)
    const = lambda shape: pl.BlockSpec(shape, lambda b, c: (0, 0))
    return pl.pallas_call(
        functools.partial(_ssd_kernel, rev=rev),
        grid=(nb, n_chunks),
        in_specs=[pl.BlockSpec((1, CHUNK, B_CONV_CH), lambda b, c: (b, blk(c), 0)),
                  pl.BlockSpec((1, CHUNK, U_SMALL), lambda b, c: (b, blk(c), 0)),
                  const((1, U_SMALL)), const((1, B_HEADS)), const((B_HEADS, B_WIDTH))],
        out_specs=pl.BlockSpec((1, CHUNK, B_WIDTH), lambda b, c: (b, blk(c), 0)),
        out_shape=jax.ShapeDtypeStruct((nb, length, B_WIDTH), F32),
        scratch_shapes=[pltpu.VMEM((B_GROUPS, B_STATE, B_RPG * B_HEADDIM), F32)],
        compiler_params=_cparams(("arbitrary", "arbitrary")),
        name="ssd_bwd" if rev else "ssd_fwd",
    )(xc, u_s, dt_bias, a_n, expand)


def _group_rmsnorm(y, width):
    out = []
    for g in range(y.shape[1] // width):
        yg = y[:, g * width:(g + 1) * width]
        out.append(yg * lax.rsqrt(jnp.mean(yg * yg, axis=1, keepdims=True) + EPS))
    return jnp.concatenate(out, axis=1)


def _mix_out_kernel(af_ref, ab_ref, ag_ref, bf_ref, bb_ref, bx_ref, bz0_ref, bz1_ref,
                    cf_ref, cb_ref, co_ref, wa_ref, wb_ref, wc_ref, dsk_ref, o_ref):
    ya = _group_rmsnorm(af_ref[...] + ab_ref[...], A_DK) * wa_ref[...] * _silu(ag_ref[...])
    o_ref[:, 0:A_WIDTH] = ya.astype(o_ref.dtype)
    z = jnp.concatenate([bz0_ref[...], bz1_ref[...]], axis=1)
    yb = (bf_ref[...] + bb_ref[...] + dsk_ref[...] * bx_ref[...]) * _silu(z)
    yb = _group_rmsnorm(yb, B_WIDTH // B_GROUPS) * wb_ref[...]
    o_ref[:, A_WIDTH:A_WIDTH + B_WIDTH] = yb.astype(o_ref.dtype)
    yc = (cf_ref[...] + cb_ref[...]) * jax.nn.sigmoid(co_ref[...])
    yc = _group_rmsnorm(yc, C_DK) * wc_ref[...]
    o_ref[:, A_WIDTH + B_WIDTH:] = yc.astype(o_ref.dtype)


def mix_out(ya_f, ya_b, yb_f, yb_b, yc_f, yc_b, u_ab, xc, u_c, wa, wb, wc, dskip_x, tm=128):
    t = ya_f.shape[0]
    row = lambda w, j=0: pl.BlockSpec((tm, w), lambda i: (i, j))
    vec = lambda w: pl.BlockSpec((1, w), lambda i: (0, 0))
    return pl.pallas_call(
        _mix_out_kernel,
        grid=(t // tm,),
        in_specs=[row(A_WIDTH), row(A_WIDTH), row(A_WIDTH, 4),
                  row(B_WIDTH), row(B_WIDTH), row(B_WIDTH), row(1024, 5), row(1024, 6),
                  row(C_WIDTH), row(C_WIDTH), row(C_WIDTH, 3),
                  vec(A_WIDTH), vec(B_WIDTH), vec(C_WIDTH), vec(B_WIDTH)],
        out_specs=pl.BlockSpec((tm, D_MODEL), lambda i: (i, 0)),
        out_shape=jax.ShapeDtypeStruct((t, D_MODEL), BF16),
        compiler_params=_cparams(("arbitrary",)),
        name="mix_out",
    )(ya_f, ya_b, u_ab, yb_f, yb_b, xc, u_ab, u_ab, yc_f, yc_b, u_c, wa, wb, wc, dskip_x)


def _first_argmax(vals, lane):
    m = jnp.max(vals, axis=1, keepdims=True)
    idx = jnp.min(jnp.where(vals == m, lane, 1 << 20), axis=1, keepdims=True)
    return m, idx


def _route_kernel(lg_ref, bias_ref, idx_ref, w_ref, rank_ref, cnt_ref, run_s):
    @pl.when(pl.program_id(0) == 0)
    def _():
        run_s[...] = jnp.zeros_like(run_s)

    logits = lg_ref[...]
    tm = logits.shape[0]
    lane = lax.broadcasted_iota(jnp.int32, (tm, 128), 1)
    is_expert = lane < N_EXPERTS
    scores = jax.nn.sigmoid(logits)
    biased = jnp.where(is_expert, scores + bias_ref[...], NEG_INF)
    per_group = N_EXPERTS // N_EXPERT_GROUPS
    grp = lane // per_group
    gscore = jnp.full((tm, 128), NEG_INF, F32)
    for g in range(N_EXPERT_GROUPS):
        vals = jnp.where(grp == g, biased, NEG_INF)
        m1, i1 = _first_argmax(vals, lane)
        m2 = jnp.max(jnp.where(lane == i1, NEG_INF, vals), axis=1, keepdims=True)
        gscore = jnp.where(lane == g, m1 + m2, gscore)
    allowed = jnp.zeros((tm, 128), jnp.bool_)
    for _ in range(TOPK_GROUPS):
        _, gi = _first_argmax(gscore, lane)
        allowed = allowed | (grp == gi)
        gscore = jnp.where(lane == gi, NEG_INF, gscore)
    masked = jnp.where(allowed & is_expert, biased, NEG_INF)
    idx_out = jnp.zeros((tm, 128), jnp.int32)
    w_out = jnp.zeros((tm, 128), F32)
    picks = []
    chosen = jnp.zeros((tm, 128), F32)
    for kk in range(TOP_K):
        _, ei = _first_argmax(masked, lane)
        sel = lane == ei
        picks.append(sel)
        chosen = chosen + sel.astype(F32)
        wk = jnp.sum(jnp.where(sel, scores, 0.0), axis=1, keepdims=True)
        idx_out = jnp.where(lane == kk, ei, idx_out)
        w_out = jnp.where(lane == kk, wk, w_out)
        masked = jnp.where(sel, NEG_INF, masked)
    w_out = w_out / jnp.sum(w_out, axis=1, keepdims=True) * ROUTED_SCALE
    idx_ref[...] = idx_out
    w_ref[...] = w_out
    strict = (lax.broadcasted_iota(jnp.int32, (tm, tm), 1)
              < lax.broadcasted_iota(jnp.int32, (tm, tm), 0)).astype(F32)
    before = _dot(strict, chosen) + run_s[...]
    rank_out = jnp.zeros((tm, 128), jnp.int32)
    for kk in range(TOP_K):
        rk = jnp.sum(jnp.where(picks[kk], before, 0.0), axis=1, keepdims=True)
        rank_out = jnp.where(lane == kk, rk.astype(jnp.int32), rank_out)
    rank_ref[...] = rank_out
    total = run_s[...] + jnp.sum(chosen, axis=0, keepdims=True)
    run_s[...] = total
    cnt_ref[...] = total.astype(jnp.int32)


def route(logits, router_bias, tm=1024):
    t = logits.shape[0]
    bias = jnp.concatenate([router_bias.astype(F32), jnp.zeros((128 - N_EXPERTS,), F32)])[None]
    row = pl.BlockSpec((tm, 128), lambda i: (i, 0))
    one = pl.BlockSpec((1, 128), lambda i: (0, 0))
    return pl.pallas_call(
        _route_kernel,
        grid=(t // tm,),
        in_specs=[row, one],
        out_specs=[row, row, row, one],
        out_shape=[jax.ShapeDtypeStruct((t, 128), jnp.int32), jax.ShapeDtypeStruct((t, 128), F32),
                   jax.ShapeDtypeStruct((t, 128), jnp.int32), jax.ShapeDtypeStruct((1, 128), jnp.int32)],
        scratch_shapes=[pltpu.VMEM((1, 128), F32)],
        compiler_params=_cparams(("arbitrary",)),
        name="route",
    )(logits, bias)


EXP_TM = 256
SCT_TM = 256


def _scatter_kernel(lt_ref, pos_ref, hp_ref, xs_hbm, zero_s, sem, zsem):
    i = pl.program_id(0)

    @pl.when(i == 0)
    def _():
        zero_s[...] = jnp.zeros_like(zero_s)
        for e in range(2 * N_EXPERTS):
            @pl.when(lt_ref[e] >= 0)
            def _():
                pltpu.make_async_copy(zero_s, xs_hbm.at[pl.ds(lt_ref[e] * EXP_TM, EXP_TM), :], zsem).start()
        for e in range(2 * N_EXPERTS):
            @pl.when(lt_ref[e] >= 0)
            def _():
                pltpu.make_async_copy(zero_s, xs_hbm.at[pl.ds(lt_ref[e] * EXP_TM, EXP_TM), :], zsem).wait()

    def body(r, carry):
        for kk in range(TOP_K):
            p = pos_ref[0, r * TOP_K + kk]
            pltpu.make_async_copy(hp_ref.at[pl.ds(r, 1), :], xs_hbm.at[pl.ds(p, 1), :], sem).start(
                priority=kk % 2)
        return carry
    lax.fori_loop(0, SCT_TM, body, 0)
    for kk in range(TOP_K):
        pltpu.make_async_copy(hp_ref, xs_hbm.at[pl.ds(0, SCT_TM), :], sem).wait()


def scatter_rows(hp, pos, clear_tiles, n_rows):
    t, half = hp.shape
    n = t // SCT_TM
    grid_spec = pltpu.PrefetchScalarGridSpec(
        num_scalar_prefetch=1,
        grid=(n,),
        in_specs=[
            pl.BlockSpec((None, 1, SCT_TM * TOP_K), lambda i, lt: (i, 0, 0), memory_space=pltpu.SMEM),
            pl.BlockSpec((SCT_TM, half), lambda i, lt: (i, 0)),
        ],
        out_specs=pl.BlockSpec(memory_space=pl.ANY),
        scratch_shapes=[
            pltpu.VMEM((EXP_TM, half), jnp.uint32),
            pltpu.SemaphoreType.DMA(()),
            pltpu.SemaphoreType.DMA(()),
        ],
    )
    return pl.pallas_call(
        _scatter_kernel,
        grid_spec=grid_spec,
        out_shape=jax.ShapeDtypeStruct((n_rows, half), jnp.uint32),
        compiler_params=_cparams(("arbitrary",)),
        name="scatter_rows",
    )(clear_tiles, pos.reshape(n, 1, SCT_TM * TOP_K), hp)


def _expert_weight_copies(e, layer, wg_hbm, wu_hbm, wd_hbm, wg_buf, wu_buf, wd_buf, sem, slot):
    return [pltpu.make_async_copy(wg_hbm.at[layer, e], wg_buf.at[slot], sem.at[slot]),
            pltpu.make_async_copy(wu_hbm.at[layer, e], wu_buf.at[slot], sem.at[slot]),
            pltpu.make_async_copy(wd_hbm.at[layer, e], wd_buf.at[slot], sem.at[slot])]


def _expert_kernel(te_ref, nu_ref, nx_ref, x_ref, wg_hbm, wu_hbm, wd_hbm, y_ref,
                   wg_buf, wu_buf, wd_buf, sem, wgu_s, wd_s, grp_s, *, layer):
    i = pl.program_id(0)
    n_used = nu_ref[0]
    half = x_ref.shape[1]
    copies = functools.partial(_expert_weight_copies, layer=layer, wg_hbm=wg_hbm, wu_hbm=wu_hbm,
                               wd_hbm=wd_hbm, wg_buf=wg_buf, wu_buf=wu_buf, wd_buf=wd_buf, sem=sem)

    @pl.when(i == 0)
    def _():
        grp_s[0] = 0
        for cp in copies(te_ref[0], slot=0):
            cp.start()

    new_expert = (i == 0) | (te_ref[i] != te_ref[jnp.maximum(i - 1, 0)])

    @pl.when(new_expert & (i < n_used))
    def _():
        slot = grp_s[0] % 2
        for cp in copies(te_ref[i], slot=slot):
            cp.wait()
        wgu_s[:, 0:D_EXPERT] = wg_buf[slot].astype(BF16)
        wgu_s[:, D_EXPERT:2 * D_EXPERT] = wu_buf[slot].astype(BF16)
        wd_s[...] = wd_buf[slot].astype(BF16)

        @pl.when(nx_ref[i] >= 0)
        def _():
            for cp in copies(nx_ref[i], slot=1 - slot):
                cp.start()
        grp_s[0] = grp_s[0] + 1

    @pl.when(i < n_used)
    def _():
        x_hi, x_lo = _unpack_bf16_pair(x_ref[...])
        h = (jnp.dot(x_hi.astype(BF16), wgu_s[0:half, :], preferred_element_type=F32)
             + jnp.dot(x_lo.astype(BF16), wgu_s[half:2 * half, :], preferred_element_type=F32))
        act = _silu(h[:, 0:D_EXPERT]) * h[:, D_EXPERT:2 * D_EXPERT]
        y = jnp.dot(act.astype(BF16), wd_s[...], preferred_element_type=F32)
        y_ref[...] = _pack_bf16_pair(y[:, 0:half], y[:, half:2 * half])

    @pl.when(i >= n_used)
    def _():
        y_ref[...] = jnp.zeros_like(y_ref)


def routed_experts(x_sorted, tile_expert, n_used, next_expert, w_gate, w_up, w_down, layer):
    n_rows, half = x_sorted.shape
    n_tiles = n_rows // EXP_TM
    d = 2 * half
    grid_spec = pltpu.PrefetchScalarGridSpec(
        num_scalar_prefetch=3,
        grid=(n_tiles,),
        in_specs=[
            pl.BlockSpec((EXP_TM, half), lambda i, te, nu, nx: (jnp.minimum(i, nu[0] - 1), 0)),
            pl.BlockSpec(memory_space=pl.ANY),
            pl.BlockSpec(memory_space=pl.ANY),
            pl.BlockSpec(memory_space=pl.ANY),
        ],
        out_specs=pl.BlockSpec((EXP_TM, half), lambda i, te, nu, nx: (i, 0)),
        scratch_shapes=[
            pltpu.VMEM((2, d, D_EXPERT), F32),
            pltpu.VMEM((2, d, D_EXPERT), F32),
            pltpu.VMEM((2, D_EXPERT, d), F32),
            pltpu.SemaphoreType.DMA((2,)),
            pltpu.VMEM((d, 2 * D_EXPERT), BF16),
            pltpu.VMEM((D_EXPERT, d), BF16),
            pltpu.SMEM((1,), jnp.int32),
        ],
    )
    return pl.pallas_call(
        functools.partial(_expert_kernel, layer=layer),
        grid_spec=grid_spec,
        out_shape=jax.ShapeDtypeStruct((n_rows, half), jnp.uint32),
        compiler_params=_cparams(("arbitrary",)),
        name="routed_experts",
    )(tile_expert, n_used, next_expert, x_sorted, w_gate, w_up, w_down)


CMB_TM = 128


def _start_combine_gather(pos_ref, y_hbm, dst, sem):
    def body(r, carry):
        for kk in range(TOP_K):
            p = pos_ref[0, r * TOP_K + kk]
            pltpu.make_async_copy(y_hbm.at[pl.ds(p, 1), :], dst.at[kk, pl.ds(r, 1), :], sem).start(
                priority=kk % 2)
        return carry
    lax.fori_loop(0, CMB_TM, body, 0)


def _combine_kernel(pos_ref, posn_ref, y_hbm, w_ref, hp_ref, x_ref, gate_ref, sgu_ref, sd_ref,
                    o_ref, ybuf, sem):
    i = pl.program_id(0)
    n = pl.num_programs(0)
    slot = i % 2
    half = hp_ref.shape[1]

    @pl.when(i == 0)
    def _():
        _start_combine_gather(pos_ref, y_hbm, ybuf.at[0], sem.at[0])

    @pl.when(i + 1 < n)
    def _():
        _start_combine_gather(posn_ref, y_hbm, ybuf.at[1 - slot], sem.at[1 - slot])

    x_hi, x_lo = _unpack_bf16_pair(hp_ref[...])
    h = (jnp.dot(x_hi.astype(BF16), sgu_ref[0:half, :], preferred_element_type=F32)
         + jnp.dot(x_lo.astype(BF16), sgu_ref[half:2 * half, :], preferred_element_type=F32))
    act = _silu(h[:, 0:D_SHARED]) * h[:, D_SHARED:2 * D_SHARED]
    shared = jnp.dot(act.astype(BF16), sd_ref[...], preferred_element_type=F32)

    for kk in range(TOP_K):
        pltpu.make_async_copy(y_hbm.at[pl.ds(0, CMB_TM), :], ybuf.at[slot, kk], sem.at[slot]).wait()
    acc_hi = shared[:, 0:half]
    acc_lo = shared[:, half:2 * half]
    w = w_ref[...]
    for kk in range(TOP_K):
        y_hi, y_lo = _unpack_bf16_pair(ybuf[slot, kk])
        wk = w[:, kk:kk + 1]
        acc_hi = acc_hi + wk * y_hi
        acc_lo = acc_lo + wk * y_lo
    g = gate_ref[...]
    o_ref[:, 0:half] = x_ref[:, 0:half] + g[:, 0:half] * acc_hi
    o_ref[:, half:2 * half] = x_ref[:, half:2 * half] + g[:, half:2 * half] * acc_lo


def moe_combine(pos, y_sorted, w, hp, x, gate, seg_of_tile, sh_gate_up, sh_down):
    t, d = x.shape
    half = d // 2
    n = t // CMB_TM
    return pl.pallas_call(
        _combine_kernel,
        grid=(n,),
        in_specs=[
            pl.BlockSpec((None, 1, CMB_TM * TOP_K), lambda i: (i, 0, 0), memory_space=pltpu.SMEM),
            pl.BlockSpec((None, 1, CMB_TM * TOP_K), lambda i: (jnp.minimum(i + 1, n - 1), 0, 0),
                         memory_space=pltpu.SMEM),
            pl.BlockSpec(memory_space=pl.ANY),
            pl.BlockSpec((CMB_TM, 128), lambda i: (i, 0)),
            pl.BlockSpec((CMB_TM, half), lambda i: (i, 0)),
            pl.BlockSpec((CMB_TM, d), lambda i: (i, 0)),
            pl.BlockSpec((None, 1, d), lambda i: (seg_of_tile(i), 0, 0)),
            pl.BlockSpec((d, 2 * D_SHARED), lambda i: (0, 0)),
            pl.BlockSpec((D_SHARED, d), lambda i: (0, 0)),
        ],
        out_specs=pl.BlockSpec((CMB_TM, d), lambda i: (i, 0)),
        out_shape=jax.ShapeDtypeStruct((t, d), F32),
        scratch_shapes=[
            pltpu.VMEM((2, TOP_K, CMB_TM, half), jnp.uint32),
            pltpu.SemaphoreType.DMA((2,)),
        ],
        compiler_params=_cparams(("arbitrary",)),
        name="moe_combine",
    )(pos, pos, y_sorted, w, hp, x, gate, sh_gate_up, sh_down)


def moe_plan(eidx, rank, counts, n_tiles):
    tiles_per = (counts + EXP_TM - 1) // EXP_TM
    tile_end = jnp.cumsum(tiles_per)
    base = (tile_end - tiles_per) * EXP_TM
    experts = jnp.arange(N_EXPERTS, dtype=jnp.int32)
    pos = rank + jnp.sum(jnp.where(eidx[..., None] == experts, base, 0), axis=-1)
    tile_expert = jnp.sum((tile_end[None, :] <= jnp.arange(n_tiles, dtype=jnp.int32)[:, None])
                          .astype(jnp.int32), axis=1)
    tile_expert = jnp.minimum(tile_expert, N_EXPERTS - 1)
    last_tile = jnp.where(tiles_per > 0, tile_end - 1, -1)
    unused = tile_end[-1] + experts
    clear_tiles = jnp.concatenate([last_tile, jnp.where(unused < n_tiles, unused, -1)])
    later = (experts[None, :] > experts[:, None]) & (tiles_per[None, :] > 0)
    nxt = jnp.min(jnp.where(later, experts[None, :], N_EXPERTS), axis=1)
    nxt = jnp.where(nxt < N_EXPERTS, nxt, -1)
    next_expert = jnp.sum(jnp.where(tile_expert[:, None] == experts[None, :], nxt[None, :], 0), axis=1)
    return tile_expert, tile_end[-1:], clear_tiles, next_expert, pos


def _rmsnorm_kernel(x_ref, g_ref, o_ref):
    x = x_ref[...]
    o_ref[...] = x * lax.rsqrt(jnp.mean(x * x, axis=-1, keepdims=True) + EPS) * g_ref[...]


def final_rmsnorm(x, g, tm=256):
    t, d = x.shape
    return pl.pallas_call(
        _rmsnorm_kernel,
        grid=(t // tm,),
        in_specs=[pl.BlockSpec((tm, d), lambda i: (i, 0)), pl.BlockSpec((1, d), lambda i: (0, 0))],
        out_specs=pl.BlockSpec((tm, d), lambda i: (i, 0)),
        out_shape=jax.ShapeDtypeStruct((t, d), F32),
        compiler_params=_cparams(("arbitrary",)),
        name="final_rmsnorm",
    )(x, g.reshape(1, d))


def lb_table(lb):
    lb = lb.astype(F32)
    return jnp.stack([jnp.log(lb), jnp.log1p(-lb), 1.0 - lb])


def mlstm_gate_bias(i_bias, f_bias):
    z = jnp.zeros((64,), F32)
    return jnp.concatenate([z, i_bias.astype(F32).reshape(-1), f_bias.astype(F32).reshape(-1),
                            jnp.zeros((32,), F32)])[None]


def ssd_consts(dt_bias, a_log):
    bias = jnp.concatenate([dt_bias.astype(F32).reshape(-1), jnp.zeros((64,), F32)])[None]
    a = -jnp.exp(a_log.astype(F32))
    expand = (jnp.arange(B_HEADS)[:, None] == jnp.arange(B_WIDTH)[None, :] // B_HEADDIM).astype(BF16)
    return [(bias, a[d][None], expand) for d in range(2)]


def _mod_tables(mod, nb):
    d = mod.shape[1] // 6
    parts = mod.reshape(mod.shape[0], 6, d)
    lat = parts[:nb]
    ctx = jnp.broadcast_to(parts[nb][None], (nb, 6, d))
    tab = jnp.stack([ctx, lat], axis=1).reshape(2 * nb, 6, d)
    return jnp.transpose(tab, (1, 0, 2))[:, :, None, :]


def _seg_fn(rows_per_batch, ctx_rows, tm):
    tiles_per_batch = rows_per_batch // tm
    ctx_tiles = ctx_rows // tm

    def seg(i):
        return 2 * (i // tiles_per_batch) + jnp.where(i % tiles_per_batch >= ctx_tiles, 1, 0)
    return seg


def _to_scan_order(a, rows):
    nb, _, d = a.shape
    return a.reshape(nb, rows, GRID_W, d).transpose(0, 2, 1, 3).reshape(nb, rows * GRID_W, d)


def _from_scan_order(a, rows):
    nb, _, d = a.shape
    return a.reshape(nb, GRID_W, rows, d).transpose(0, 2, 1, 3).reshape(nb, rows * GRID_W, d)


def _moe_block(x2, tab, seg_mod, seg_cmb, layer, norm_w, router_w, router_bias, w_gate, w_up, w_down,
               sh_gate, sh_up, sh_down):
    t, d = x2.shape
    rw = jnp.concatenate([router_w.astype(F32), jnp.zeros((d, 128 - N_EXPERTS), F32)], axis=1)
    rw_hi = rw.astype(BF16)
    rw = jnp.concatenate([rw_hi, (rw - rw_hi.astype(F32)).astype(BF16)], axis=1)
    hp, logits = modulate(x2, norm_w, tab[3], tab[4], seg_mod, 256, router_w=rw)
    eidx, ew, rank, counts = route(logits, router_bias)
    n_tiles = t * TOP_K // EXP_TM + N_EXPERTS
    tile_expert, n_used, clear_tiles, next_expert, pos = moe_plan(
        eidx[:, :TOP_K], rank[:, :TOP_K], counts[0, :N_EXPERTS], n_tiles)
    x_sorted = scatter_rows(hp, pos, clear_tiles, n_tiles * EXP_TM)
    y_sorted = routed_experts(x_sorted, tile_expert, n_used, next_expert, w_gate, w_up, w_down, layer)
    sgu = jnp.concatenate([sh_gate, sh_up], axis=1).astype(BF16)
    return moe_combine(pos.reshape(t // CMB_TM, 1, CMB_TM * TOP_K), y_sorted, ew, hp, x2, tab[5],
                       seg_cmb, sgu, sh_down.astype(BF16))


def kernel(x, c, ctx, c_ctx, ada_w, ada_b, norm_mix, norm_ffn, norm_final, w_in, w_out, hgrn_lb,
           hgrn_norm, ssm_conv_w, ssm_conv_b, ssm_dt_bias, ssm_a_log, ssm_d, ssm_norm, mlstm_i_bias,
           mlstm_f_bias, mlstm_norm, router_w, router_bias, moe_w_gate, moe_w_up, moe_w_down,
           shared_w_gate, shared_w_up, shared_w_down):
    nb, seq, d = x.shape
    n_ctx = ctx.shape[1]
    length = n_ctx + seq
    rows = seq // GRID_W
    depth = ada_w.shape[0]
    t_all = nb * length
    t_lat = nb * seq

    lb_all = jnp.cumsum(jax.nn.softmax(hgrn_lb.astype(F32), axis=0), axis=0)
    lb_all = lb_all - lb_all[0]
    cond = jnp.concatenate([c, c_ctx[None], jnp.zeros((8 - nb - 1, d), F32)], axis=0)
    mod = ada_modulation(cond, ada_w, ada_b)

    xs = jnp.concatenate([ctx, x], axis=1)
    w_in_t = jnp.swapaxes(w_in, 1, 2)
    for l in range(depth):
        last = l == depth - 1
        tab = _mod_tables(mod[l], nb)
        h = modulate(xs.reshape(t_all, d), norm_mix[l], tab[0], tab[1], _seg_fn(length, n_ctx, 256), 256)
        if l % 2 == 1:
            h3 = h.reshape(nb, length, d)
            h = jnp.concatenate([h3[:, :n_ctx], _to_scan_order(h3[:, n_ctx:], rows)], axis=1)
            h = h.reshape(t_all, d)
        w_c = w_in_t[l:l + 1, U_C0:U_G0]
        w_small = jnp.concatenate([w_in_t[l:l + 1, U_DT0:U_C0], w_in_t[l:l + 1, U_G0:],
                                   jnp.zeros((1, U_SMALL - 96, d), F32)], axis=1)
        u_ab = matmul_nt(h, w_in_t, l, tm=1024, tn=512, n_tiles=U_AB // 512, name="in_proj_ab")
        u_c = matmul_nt(h, w_c, 0, tm=1024, tn=512, name="in_proj_c")
        u_s = matmul_nt(h, w_small, 0, tm=1024, tn=U_SMALL, name="in_proj_small")
        u_ab3 = u_ab.reshape(nb, length, U_AB)
        u_c3 = u_c.reshape(nb, length, 4 * C_WIDTH)
        u_s3 = u_s.reshape(nb, length, U_SMALL)

        xc = ssd_conv(u_ab3, ssm_conv_w[l], ssm_conv_b[l], n_ctx)
        consts = ssd_consts(ssm_dt_bias[l], ssm_a_log[l])
        gate_bias = mlstm_gate_bias(mlstm_i_bias[l], mlstm_f_bias[l])
        ya = [hgrn2_scan(u_ab3, lb_table(lb_all[l][dd]), n_ctx, rev=bool(dd)) for dd in range(2)]
        yb = [ssd_scan(xc, u_s3, *consts[dd], n_ctx, rev=bool(dd)) for dd in range(2)]
        yc = mlstm_scan(u_c3, u_s3, gate_bias, n_ctx)
        flat = lambda a: a.reshape(t_all, a.shape[-1])
        ymix = mix_out(flat(ya[0]), flat(ya[1]), flat(yb[0]), flat(yb[1]), flat(yc[0]), flat(yc[1]),
                       u_ab, flat(xc), u_c, hgrn_norm[l][None], ssm_norm[l][None], mlstm_norm[l][None],
                       jnp.repeat(ssm_d[l].astype(F32), B_HEADDIM)[None])
        if l % 2 == 1:
            y3 = ymix.reshape(nb, length, d)
            ymix = jnp.concatenate([y3[:, :n_ctx], _from_scan_order(y3[:, n_ctx:], rows)], axis=1)
            ymix = ymix.reshape(t_all, d)

        moe_w = (l, norm_ffn[l], router_w[l], router_bias[l], moe_w_gate, moe_w_up, moe_w_down,
                 shared_w_gate[l], shared_w_up[l], shared_w_down[l])
        ymix = ymix.reshape(nb, length, d)
        w_o = w_out[l].astype(BF16)
        if not last:
            x2 = matmul_residual(ymix, w_o, xs, tab[2], first_row=0, n_rows=length,
                                 ctx_rows=n_ctx, tm=256, tn=1024).reshape(t_all, d)
            x2 = _moe_block(x2, tab, _seg_fn(length, n_ctx, 256), _seg_fn(length, n_ctx, CMB_TM), *moe_w)
            xs = x2.reshape(nb, length, d)
        else:
            x2 = matmul_residual(ymix, w_o, xs, tab[2], first_row=n_ctx, n_rows=seq,
                                 ctx_rows=n_ctx, tm=256, tn=1024).reshape(t_lat, d)
            x2 = _moe_block(x2, tab, _seg_fn(seq, 0, 256), _seg_fn(seq, 0, CMB_TM), *moe_w)
            return final_rmsnorm(x2, norm_final).reshape(nb, seq, d)
```

```python
import functools

import jax
import jax.numpy as jnp
from jax import lax
from jax.experimental import pallas as pl
from jax.experimental.pallas import tpu as pltpu

F32 = jnp.float32
BF16 = jnp.bfloat16
HIGHEST = lax.Precision.HIGHEST
NEG_INF = float("-inf")

D_MODEL = 4096
GRID_W = 64
CHUNK = 64
EPS = 1e-6
A_WIDTH = 1024
A_DK = 128
A_HEADS = 8
B_WIDTH = 2048
B_HEADDIM = 64
B_HEADS = 32
B_GROUPS = 8
B_RPG = 4
B_STATE = 128
B_CONV = 5
B_CONV_CH = 4096
C_WIDTH = 1024
C_DK = 128
C_HEADS = 8
N_EXPERTS = 64
N_EXPERT_GROUPS = 8
TOPK_GROUPS = 4
TOP_K = 8
D_EXPERT = 256
D_SHARED = 256
ROUTED_SCALE = 2.5

U_AB = 11264
U_DT0 = 11264
U_C0 = 11328
U_G0 = 15424
U_SMALL = 128

SUB = 16
PAD = 16
MAX_SAFE_EXPONENT = 40.0
VMEM_LIMIT = 56 * 1024 * 1024


def _cparams(sem):
    return pltpu.CompilerParams(dimension_semantics=sem, vmem_limit_bytes=VMEM_LIMIT)


def _dot(a, b):
    return jnp.dot(a.astype(BF16), b.astype(BF16), preferred_element_type=F32)


def _bdot(a, b, ca, cb):
    return lax.dot_general(a.astype(BF16), b.astype(BF16), (((ca,), (cb,)), ((0,), (0,))),
                           preferred_element_type=F32)


def _dot_hi(a, b):
    return jnp.dot(a, b, precision=HIGHEST, preferred_element_type=F32)


def _dot_nt_hi(a, b):
    return lax.dot_general(a, b, (((1,), (1,)), ((), ())), precision=HIGHEST,
                           preferred_element_type=F32)


def _log_sigmoid(z):
    return jnp.minimum(z, 0.0) - jnp.log(1.0 + jnp.exp(-jnp.abs(z)))


def _softplus(z):
    return jnp.maximum(z, 0.0) + jnp.log(1.0 + jnp.exp(-jnp.abs(z)))


def _silu(z):
    return z * jax.nn.sigmoid(z)


def _tri(rev):
    ri = lax.broadcasted_iota(jnp.int32, (CHUNK, CHUNK), 0)
    ci = lax.broadcasted_iota(jnp.int32, (CHUNK, CHUNK), 1)
    return (ci >= ri) if rev else (ci <= ri)


def _chunk_block(c, n_ctx_chunks, n_lat_chunks, rev):
    if not rev:
        return c
    return jnp.where(c < n_ctx_chunks, n_ctx_chunks - 1 - c, 2 * n_ctx_chunks + n_lat_chunks - 1 - c)


def _ada_kernel(c_ref, w_ref, b_ref, o_ref):
    s = _silu(c_ref[...])
    o_ref[...] = _dot(s, w_ref[...]) + b_ref[...]


def ada_modulation(cond, ada_w, ada_b):
    depth, d, n = ada_w.shape
    tn = 512
    return pl.pallas_call(
        _ada_kernel,
        grid=(depth, n // tn),
        in_specs=[
            pl.BlockSpec((8, d), lambda l, j: (0, 0)),
            pl.BlockSpec((None, d, tn), lambda l, j: (l, 0, j)),
            pl.BlockSpec((None, 1, tn), lambda l, j: (l, 0, j)),
        ],
        out_specs=pl.BlockSpec((None, 8, tn), lambda l, j: (l, 0, j)),
        out_shape=jax.ShapeDtypeStruct((depth, 8, n), F32),
        compiler_params=_cparams(("arbitrary", "arbitrary")),
        name="ada_modulation",
    )(cond, ada_w, ada_b.reshape(depth, 1, n))


def _pack_bf16_pair(a, b):
    hi = lax.bitcast_convert_type(a.astype(BF16).astype(F32), jnp.uint32)
    lo = lax.bitcast_convert_type(b.astype(BF16).astype(F32), jnp.uint32)
    return hi | (lo >> 16)


def _unpack_bf16_pair(w):
    hi = lax.bitcast_convert_type(w & jnp.uint32(0xFFFF0000), F32)
    lo = lax.bitcast_convert_type(w << 16, F32)
    return hi, lo


def _modulate_kernel(x_ref, g_ref, sh_ref, sc_ref, *rest, with_router):
    x = x_ref[...]
    y = x * lax.rsqrt(jnp.mean(x * x, axis=-1, keepdims=True) + EPS) * g_ref[...]
    h = y * (1.0 + sc_ref[...]) + sh_ref[...]
    if with_router:
        rw_ref, h_ref, lg_ref = rest
        h_hi = h.astype(BF16)
        h_lo = (h - h_hi.astype(F32)).astype(BF16)
        both = jnp.dot(h_hi, rw_ref[...], preferred_element_type=F32)
        lg_ref[...] = (both[:, 0:128] + both[:, 128:256]
                       + jnp.dot(h_lo, rw_ref[:, 0:128], preferred_element_type=F32))
        half = h.shape[1] // 2
        h_ref[...] = _pack_bf16_pair(h[:, :half], h[:, half:])
    else:
        (h_ref,) = rest
        h_ref[...] = h.astype(h_ref.dtype)


def modulate(x, g, shift, scale, seg_of_tile, tm, router_w=None):
    t, d = x.shape
    with_router = router_w is not None
    vec = pl.BlockSpec((None, 1, d), lambda i: (seg_of_tile(i), 0, 0))
    in_specs = [pl.BlockSpec((tm, d), lambda i: (i, 0)),
                pl.BlockSpec((1, d), lambda i: (0, 0)), vec, vec]
    args = [x, g.reshape(1, d), shift, scale]
    if with_router:
        in_specs.append(pl.BlockSpec((d, 256), lambda i: (0, 0)))
        args.append(router_w)
        out_specs = [pl.BlockSpec((tm, d // 2), lambda i: (i, 0)),
                     pl.BlockSpec((tm, 128), lambda i: (i, 0))]
        out_shape = [jax.ShapeDtypeStruct((t, d // 2), jnp.uint32),
                     jax.ShapeDtypeStruct((t, 128), F32)]
    else:
        out_specs = pl.BlockSpec((tm, d), lambda i: (i, 0))
        out_shape = jax.ShapeDtypeStruct((t, d), BF16)
    return pl.pallas_call(
        functools.partial(_modulate_kernel, with_router=with_router),
        grid=(t // tm,),
        in_specs=in_specs, out_specs=out_specs, out_shape=out_shape,
        compiler_params=_cparams(("arbitrary",)),
        name="modulate_router" if with_router else "modulate",
    )(*args)


def _mm_nt_kernel(a_ref, b_ref, o_ref):
    o_ref[...] = lax.dot_general(a_ref[...], b_ref[...].astype(BF16), (((1,), (1,)), ((), ())),
                                 preferred_element_type=F32)


def matmul_nt(a, b_t, layer, *, tm, tn, n_tiles=None, name="matmul_nt"):
    m, k = a.shape
    n_tiles = b_t.shape[1] // tn if n_tiles is None else n_tiles
    return pl.pallas_call(
        _mm_nt_kernel,
        grid=(m // tm, n_tiles),
        in_specs=[pl.BlockSpec((tm, k), lambda i, j: (i, 0)),
                  pl.BlockSpec((None, tn, k), lambda i, j: (layer, j, 0))],
        out_specs=pl.BlockSpec((tm, tn), lambda i, j: (i, j)),
        out_shape=jax.ShapeDtypeStruct((m, n_tiles * tn), F32),
        compiler_params=_cparams(("arbitrary", "arbitrary")),
        name=name,
    )(a, b_t)


def _mm_res_kernel(a_ref, b_ref, r_ref, g_ref, o_ref):
    acc = jnp.dot(a_ref[...], b_ref[...], preferred_element_type=F32)
    o_ref[...] = r_ref[...] + g_ref[...] * acc


def matmul_residual(a, b, res, gate, *, first_row, n_rows, ctx_rows, tm, tn):
    nb, _, k = a.shape
    n = b.shape[1]
    t0 = first_row // tm
    ctx_tiles = ctx_rows // tm
    seg = lambda bb, it: 2 * bb + jnp.where(t0 + it >= ctx_tiles, 1, 0)
    return pl.pallas_call(
        _mm_res_kernel,
        grid=(n // tn, nb, n_rows // tm),
        in_specs=[pl.BlockSpec((None, tm, k), lambda j, bb, it: (bb, t0 + it, 0)),
                  pl.BlockSpec((k, tn), lambda j, bb, it: (0, j)),
                  pl.BlockSpec((None, tm, tn), lambda j, bb, it: (bb, t0 + it, j)),
                  pl.BlockSpec((None, 1, tn), lambda j, bb, it: (seg(bb, it), 0, j))],
        out_specs=pl.BlockSpec((None, tm, tn), lambda j, bb, it: (bb, it, j)),
        out_shape=jax.ShapeDtypeStruct((nb, n_rows, n), F32),
        compiler_params=_cparams(("arbitrary", "arbitrary", "arbitrary")),
        name="matmul_residual",
    )(a, b, res, gate)


def _hgrn2_kernel(q_ref, f_ref, v_ref, lb_ref, y_ref, k_s, v_s, gc_s, st_s, *, rev):
    c = pl.program_id(1)

    @pl.when(c == 0)
    def _():
        st_s[...] = jnp.zeros_like(st_s)
        k_s[...] = jnp.zeros_like(k_s)
        v_s[...] = jnp.zeros_like(v_s)
        gc_s[...] = jnp.zeros_like(gc_s)

    nh = A_HEADS
    q = q_ref[0]
    z = f_ref[0]
    log_lb = lb_ref[0:1, :]
    log1m_lb = lb_ref[1:2, :]
    one_m_lb = lb_ref[2:3, :]
    b2 = log1m_lb + _log_sigmoid(z)
    logf = jnp.maximum(log_lb, b2) + jnp.log(1.0 + jnp.exp(-jnp.abs(log_lb - b2)))
    gc2 = _dot_hi(_tri(rev).astype(F32), logf)
    heads = lambda a: jnp.stack([a[:, h * A_DK:(h + 1) * A_DK] for h in range(nh)])
    qh = heads(q * jax.nn.sigmoid(q) * (A_DK ** -0.5))
    kh = heads(one_m_lb * jax.nn.sigmoid(-z))
    vh = heads(v_ref[0])
    gcm = heads(gc2)
    gxm = gcm - heads(logf)
    st = st_s[...]
    y_inter = _bdot(qh * jnp.exp(gcm), st, 2, 2)
    mid = gcm[:, CHUNK // 2:CHUNK // 2 + 1, :]

    def factored(_):
        sc = _bdot(qh * jnp.exp(gcm - mid), kh * jnp.exp(mid - gcm), 2, 2)
        return _bdot(jnp.where(_tri(rev)[None], sc, 0.0), vh, 2, 1)

    def exact(_):
        k_s[:, PAD:PAD + CHUNK, :] = kh
        v_s[:, PAD:PAD + CHUNK, :] = vh
        gc_s[:, PAD:PAD + CHUNK, :] = gcm
        rows = lax.broadcasted_iota(jnp.int32, (1, CHUNK, 1), 1)
        rowmod = rows % SUB
        n_sub = CHUNK // SUB
        y = jnp.zeros((nh, CHUNK, A_DK), F32)
        for dlt in range(SUB):
            off = PAD + dlt if rev else PAD - dlt
            ksh = k_s[:, off:off + CHUNK, :]
            vsh = v_s[:, off:off + CHUNK, :]
            gsh = gc_s[:, off:off + CHUNK, :]
            valid = (rowmod + dlt < SUB) if rev else (rowmod >= dlt)
            e = jnp.exp(jnp.where(valid, gcm - gsh, NEG_INF))
            r = jnp.sum(qh * ksh * e, axis=2, keepdims=True)
            y = y + r * vsh
        pieces = []
        for blk in range(n_sub):
            r0, r1 = blk * SUB, (blk + 1) * SUB
            first = (blk == n_sub - 1) if rev else (blk == 0)
            if first:
                pieces.append(jnp.zeros((nh, SUB, A_DK), F32))
                continue
            ref = gxm[:, r1 - 1:r1, :] if rev else gxm[:, r0:r0 + 1, :]
            earlier = (rows >= r1) if rev else (rows < r0)
            qhat = qh[:, r0:r1, :] * jnp.exp(gcm[:, r0:r1, :] - ref)
            khat = kh * jnp.exp(jnp.where(earlier, ref - gcm, NEG_INF))
            sc = _bdot(qhat, khat, 2, 2)
            pieces.append(_bdot(sc, vh, 2, 1))
        return y + jnp.concatenate(pieces, axis=1)

    worst = jnp.max(jnp.abs(gcm - mid))
    y = y_inter + lax.cond(worst < MAX_SAFE_EXPONENT, factored, exact, 0)
    gtot = gcm[:, 0:1, :] if rev else gcm[:, CHUNK - 1:CHUNK, :]
    st_s[...] = st * jnp.exp(gtot) + _bdot(vh, kh * jnp.exp(gtot - gcm), 1, 1)
    for h in range(nh):
        y_ref[0, :, h * A_DK:(h + 1) * A_DK] = y[h]


def hgrn2_scan(u_ab, lb_tab, n_ctx, rev):
    nb, length, _ = u_ab.shape
    n_chunks = length // CHUNK
    n_ctx_chunks = n_ctx // CHUNK
    blk = functools.partial(_chunk_block, n_ctx_chunks=n_ctx_chunks,
                            n_lat_chunks=n_chunks - n_ctx_chunks, rev=rev)
    fcol = 2 if rev else 1

    def col(j):
        return pl.BlockSpec((1, CHUNK, A_WIDTH), lambda b, c: (b, blk(c), j))

    return pl.pallas_call(
        functools.partial(_hgrn2_kernel, rev=rev),
        grid=(nb, n_chunks),
        in_specs=[col(0), col(fcol), col(3), pl.BlockSpec((3, A_WIDTH), lambda b, c: (0, 0))],
        out_specs=pl.BlockSpec((1, CHUNK, A_WIDTH), lambda b, c: (b, blk(c), 0)),
        out_shape=jax.ShapeDtypeStruct((nb, length, A_WIDTH), F32),
        scratch_shapes=[
            pltpu.VMEM((A_HEADS, CHUNK + 2 * PAD, A_DK), F32),
            pltpu.VMEM((A_HEADS, CHUNK + 2 * PAD, A_DK), F32),
            pltpu.VMEM((A_HEADS, CHUNK + 2 * PAD, A_DK), F32),
            pltpu.VMEM((A_HEADS, A_DK, A_DK), F32),
        ],
        compiler_params=_cparams(("arbitrary", "arbitrary")),
        name="hgrn2_bwd" if rev else "hgrn2_fwd",
    )(u_ab, u_ab, u_ab, lb_tab)


def _mlstm_kernel(qf_ref, kf_ref, vf_ref, gf_ref, qb_ref, kb_ref, vb_ref, gb_ref, bias_ref,
                  hf_ref, hb_ref, c_s, n_s, m_s):
    @pl.when(pl.program_id(1) == 0)
    def _():
        c_s[...] = jnp.zeros_like(c_s)
        n_s[...] = jnp.zeros_like(n_s)
        m_s[...] = jnp.zeros_like(m_s)

    nh = C_HEADS
    eye = (lax.broadcasted_iota(jnp.int32, (nh, nh), 0)
           == lax.broadcasted_iota(jnp.int32, (nh, nh), 1)).astype(F32)
    heads = lambda ref: [ref[0, :, h * C_DK:(h + 1) * C_DK] for h in range(nh)]
    bcols, igcols, arows, masks = [], [], [], []
    for d, g_ref in enumerate((gf_ref, gb_ref)):
        gates = g_ref[0] + bias_ref[...]
        ig_all = gates[:, 64 + 8 * d:72 + 8 * d]
        lf_all = _log_sigmoid(gates[:, 80 + 8 * d:88 + 8 * d])
        mask = _tri(bool(d))
        b_all = _dot_hi(mask.astype(F32), lf_all)
        arows.append(_dot_nt_hi(eye, ig_all - b_all))
        bcols += [b_all[:, h:h + 1] for h in range(nh)]
        igcols += [ig_all[:, h:h + 1] for h in range(nh)]
        masks.append(jnp.broadcast_to(mask[None], (nh, CHUNK, CHUNK)))
    q = jnp.stack(heads(qf_ref) + heads(qb_ref)) * (C_DK ** -0.5)
    k = jnp.stack(heads(kf_ref) + heads(kb_ref))
    v = jnp.stack(heads(vf_ref) + heads(vb_ref))
    bcol = jnp.stack(bcols)
    igcol = jnp.stack(igcols)
    arow = jnp.concatenate(arows, axis=0)[:, None, :]
    mask = jnp.concatenate(masks, axis=0)
    dmat = jnp.where(mask, bcol + arow, NEG_INF)
    m0 = m_s[...]
    m_inter = bcol + m0
    m_i = jnp.maximum(m_inter, jnp.max(dmat, axis=2, keepdims=True))
    w_intra = jnp.exp(dmat - m_i)
    w_inter = jnp.exp(m_inter - m_i)
    s = _bdot(q, k, 2, 2) * w_intra
    cmat = c_s[...]
    nvec = n_s[...]
    num = _bdot(s, v, 2, 1) + w_inter * _bdot(q, cmat, 2, 1)
    den = jnp.sum(s, axis=2, keepdims=True) + w_inter * jnp.sum(q * nvec, axis=2, keepdims=True)
    out = num / jnp.maximum(jnp.abs(den), jnp.exp(-m_i))
    for h in range(nh):
        hf_ref[0, :, h * C_DK:(h + 1) * C_DK] = out[h]
        hb_ref[0, :, h * C_DK:(h + 1) * C_DK] = out[nh + h]
    b_last = jnp.concatenate([bcol[:nh, CHUNK - 1:CHUNK, :], bcol[nh:, 0:1, :]], axis=0)
    log_wj = b_last - bcol + igcol
    m_new = jnp.maximum(b_last + m0, jnp.max(log_wj, axis=1, keepdims=True))
    w0 = jnp.exp(b_last + m0 - m_new)
    wk = jnp.exp(log_wj - m_new) * k
    c_s[...] = w0 * cmat + _bdot(wk, v, 1, 1)
    n_s[...] = w0 * nvec + jnp.sum(wk, axis=1, keepdims=True)
    m_s[...] = m_new


def mlstm_scan(u_c, u_s, gate_bias, n_ctx):
    nb, length, _ = u_c.shape
    n_chunks = length // CHUNK
    n_ctx_chunks = n_ctx // CHUNK
    blks = [functools.partial(_chunk_block, n_ctx_chunks=n_ctx_chunks,
                              n_lat_chunks=n_chunks - n_ctx_chunks, rev=rev) for rev in (False, True)]
    in_specs = []
    for blk in blks:
        in_specs += [pl.BlockSpec((1, CHUNK, C_WIDTH), lambda b, c, j=j, blk=blk: (b, blk(c), j))
                     for j in range(3)]
        in_specs.append(pl.BlockSpec((1, CHUNK, U_SMALL), lambda b, c, blk=blk: (b, blk(c), 0)))
    in_specs.append(pl.BlockSpec((1, U_SMALL), lambda b, c: (0, 0)))
    return pl.pallas_call(
        _mlstm_kernel,
        grid=(nb, n_chunks),
        in_specs=in_specs,
        out_specs=[pl.BlockSpec((1, CHUNK, C_WIDTH), lambda b, c, blk=blk: (b, blk(c), 0)) for blk in blks],
        out_shape=[jax.ShapeDtypeStruct((nb, length, C_WIDTH), F32)] * 2,
        scratch_shapes=[
            pltpu.VMEM((2 * C_HEADS, C_DK, C_DK), F32),
            pltpu.VMEM((2 * C_HEADS, 1, C_DK), F32),
            pltpu.VMEM((2 * C_HEADS, 1, 1), F32),
        ],
        compiler_params=_cparams(("arbitrary", "arbitrary")),
        name="mlstm_scan",
    )(u_c, u_c, u_c, u_s, u_c, u_c, u_c, u_s, gate_bias)


CONV_TM = 256
CONV_TN = 1024
HALO = 8


def _conv_kernel(prev_ref, cur_ref, next_ref, w_ref, b_ref, o_ref, win_s, *, tiles_ctx, tiles_all):
    i = pl.program_id(1)
    seg_start = (i == 0) | (i == tiles_ctx)
    seg_end = (i == tiles_ctx - 1) | (i == tiles_all - 1)
    win_s[0:HALO, :] = jnp.where(seg_start, 0.0, prev_ref[0])
    win_s[HALO:HALO + CONV_TM, :] = cur_ref[0]
    win_s[HALO + CONV_TM:2 * HALO + CONV_TM, :] = jnp.where(seg_end, 0.0, next_ref[0])
    acc = b_ref[...] + jnp.zeros((CONV_TM, CONV_TN), F32)
    for t in range(B_CONV):
        o = HALO + t - B_CONV // 2
        acc = acc + w_ref[t:t + 1, :] * win_s[o:o + CONV_TM, :]
    o_ref[0] = _silu(acc)


def ssd_conv(u_ab, conv_w, conv_b, n_ctx):
    nb, length, _ = u_ab.shape
    tiles_all = length // CONV_TM
    tiles_ctx = n_ctx // CONV_TM
    col0 = (U_AB - B_CONV_CH) // CONV_TN
    per = CONV_TM // HALO
    n_halo = length // HALO
    return pl.pallas_call(
        functools.partial(_conv_kernel, tiles_ctx=tiles_ctx, tiles_all=tiles_all),
        grid=(nb, tiles_all, B_CONV_CH // CONV_TN),
        in_specs=[
            pl.BlockSpec((1, HALO, CONV_TN), lambda b, i, j: (b, jnp.maximum(i * per - 1, 0), col0 + j)),
            pl.BlockSpec((1, CONV_TM, CONV_TN), lambda b, i, j: (b, i, col0 + j)),
            pl.BlockSpec((1, HALO, CONV_TN),
                         lambda b, i, j: (b, jnp.minimum((i + 1) * per, n_halo - 1), col0 + j)),
            pl.BlockSpec((B_CONV, CONV_TN), lambda b, i, j: (0, j)),
            pl.BlockSpec((1, CONV_TN), lambda b, i, j: (0, j)),
        ],
        out_specs=pl.BlockSpec((1, CONV_TM, CONV_TN), lambda b, i, j: (b, i, j)),
        out_shape=jax.ShapeDtypeStruct((nb, length, B_CONV_CH), F32),
        scratch_shapes=[pltpu.VMEM((CONV_TM + 2 * HALO, CONV_TN), F32)],
        compiler_params=_cparams(("arbitrary", "arbitrary", "arbitrary")),
        name="ssd_conv",
    )(u_ab, u_ab, u_ab, conv_w, conv_b.reshape(1, B_CONV_CH))


def _ssd_kernel(xc_ref, us_ref, bias_ref, an_ref, e_ref, y_ref, s_s, *, rev):
    c = pl.program_id(1)

    @pl.when(c == 0)
    def _():
        s_s[...] = jnp.zeros_like(s_s)

    d = 1 if rev else 0
    ng = B_GROUPS
    gw = B_RPG * B_HEADDIM
    mask = _tri(rev)
    last = 0 if rev else CHUNK - 1
    dt_n = _softplus(us_ref[0] + bias_ref[...])[:, 32 * d:32 * d + B_HEADS]
    cum_n = _dot_hi(mask.astype(F32), dt_n * an_ref[...])
    eye = (lax.broadcasted_iota(jnp.int32, (B_HEADS, B_HEADS), 0)
           == lax.broadcasted_iota(jnp.int32, (B_HEADS, B_HEADS), 1)).astype(F32)
    cum_t = _dot_nt_hi(eye, cum_n)
    decay_in = jnp.exp(cum_n)
    decay_out = jnp.exp(cum_n[last:last + 1, :] - cum_n)
    spread = _dot(jnp.concatenate([dt_n, decay_in, decay_out], axis=0), e_ref[...])
    groups = lambda a: jnp.stack([a[:, g * gw:(g + 1) * gw] for g in range(ng)])
    dt_x = groups(spread[0:CHUNK])
    in_x = groups(spread[CHUNK:2 * CHUNK])
    out_x = groups(spread[2 * CHUNK:3 * CHUNK])
    xdt = jnp.stack([xc_ref[0, :, g * gw:(g + 1) * gw] for g in range(ng)]) * dt_x
    bm = jnp.stack([xc_ref[0, :, B_WIDTH + g * B_STATE:B_WIDTH + (g + 1) * B_STATE]
                    for g in range(ng)])
    c0 = B_WIDTH + B_GROUPS * B_STATE
    cm = jnp.stack([xc_ref[0, :, c0 + g * B_STATE:c0 + (g + 1) * B_STATE] for g in range(ng)])
    state = s_s[...]
    cb = _bdot(cm, bm, 2, 2)
    y = in_x * _bdot(cm, state, 2, 1)
    head_of_lane = lax.broadcasted_iota(jnp.int32, (1, 1, gw), 2) // B_HEADDIM
    for r in range(B_RPG):
        ccol = jnp.stack([cum_n[:, g * B_RPG + r:g * B_RPG + r + 1] for g in range(ng)])
        crow = jnp.stack([cum_t[g * B_RPG + r:g * B_RPG + r + 1, :] for g in range(ng)])
        lmat = cb * jnp.exp(jnp.where(mask[None], ccol - crow, NEG_INF))
        y = y + _bdot(lmat, jnp.where(head_of_lane == r, xdt, 0.0), 2, 1)
    s_s[...] = in_x[:, last:last + 1, :] * state + _bdot(bm, xdt * out_x, 1, 1)
    for g in range(ng):
        y_ref[0, :, g * gw:(g + 1) * gw] = y[g]


def ssd_scan(xc, u_s, dt_bias, a_n, expand, n_ctx, rev):
    nb, length, _ = xc.shape
    n_chunks = length // CHUNK
    n_ctx_chunks = n_ctx // CHUNK
    blk = functools.partial(_chunk_block, n_ctx_chunks=n_ctx_chunks,
                            n_lat_chunks=n_chunks - n_ctx_chunks, rev=rev)
    const = lambda shape: pl.BlockSpec(shape, lambda b, c: (0, 0))
    return pl.pallas_call(
        functools.partial(_ssd_kernel, rev=rev),
        grid=(nb, n_chunks),
        in_specs=[pl.BlockSpec((1, CHUNK, B_CONV_CH), lambda b, c: (b, blk(c), 0)),
                  pl.BlockSpec((1, CHUNK, U_SMALL), lambda b, c: (b, blk(c), 0)),
                  const((1, U_SMALL)), const((1, B_HEADS)), const((B_HEADS, B_WIDTH))],
        out_specs=pl.BlockSpec((1, CHUNK, B_WIDTH), lambda b, c: (b, blk(c), 0)),
        out_shape=jax.ShapeDtypeStruct((nb, length, B_WIDTH), F32),
        scratch_shapes=[pltpu.VMEM((B_GROUPS, B_STATE, B_RPG * B_HEADDIM), F32)],
        compiler_params=_cparams(("arbitrary", "arbitrary")),
        name="ssd_bwd" if rev else "ssd_fwd",
    )(xc, u_s, dt_bias, a_n, expand)


def _group_rmsnorm(y, width):
    out = []
    for g in range(y.shape[1] // width):
        yg = y[:, g * width:(g + 1) * width]
        out.append(yg * lax.rsqrt(jnp.mean(yg * yg, axis=1, keepdims=True) + EPS))
    return jnp.concatenate(out, axis=1)


def _mix_out_kernel(af_ref, ab_ref, ag_ref, bf_ref, bb_ref, bx_ref, bz0_ref, bz1_ref,
                    cf_ref, cb_ref, co_ref, wa_ref, wb_ref, wc_ref, dsk_ref, o_ref):
    ya = _group_rmsnorm(af_ref[...] + ab_ref[...], A_DK) * wa_ref[...] * _silu(ag_ref[...])
    o_ref[:, 0:A_WIDTH] = ya.astype(o_ref.dtype)
    z = jnp.concatenate([bz0_ref[...], bz1_ref[...]], axis=1)
    yb = (bf_ref[...] + bb_ref[...] + dsk_ref[...] * bx_ref[...]) * _silu(z)
    yb = _group_rmsnorm(yb, B_WIDTH // B_GROUPS) * wb_ref[...]
    o_ref[:, A_WIDTH:A_WIDTH + B_WIDTH] = yb.astype(o_ref.dtype)
    yc = (cf_ref[...] + cb_ref[...]) * jax.nn.sigmoid(co_ref[...])
    yc = _group_rmsnorm(yc, C_DK) * wc_ref[...]
    o_ref[:, A_WIDTH + B_WIDTH:] = yc.astype(o_ref.dtype)


def mix_out(ya_f, ya_b, yb_f, yb_b, yc_f, yc_b, u_ab, xc, u_c, wa, wb, wc, dskip_x, tm=128):
    t = ya_f.shape[0]
    row = lambda w, j=0: pl.BlockSpec((tm, w), lambda i: (i, j))
    vec = lambda w: pl.BlockSpec((1, w), lambda i: (0, 0))
    return pl.pallas_call(
        _mix_out_kernel,
        grid=(t // tm,),
        in_specs=[row(A_WIDTH), row(A_WIDTH), row(A_WIDTH, 4),
                  row(B_WIDTH), row(B_WIDTH), row(B_WIDTH), row(1024, 5), row(1024, 6),
                  row(C_WIDTH), row(C_WIDTH), row(C_WIDTH, 3),
                  vec(A_WIDTH), vec(B_WIDTH), vec(C_WIDTH), vec(B_WIDTH)],
        out_specs=pl.BlockSpec((tm, D_MODEL), lambda i: (i, 0)),
        out_shape=jax.ShapeDtypeStruct((t, D_MODEL), BF16),
        compiler_params=_cparams(("arbitrary",)),
        name="mix_out",
    )(ya_f, ya_b, u_ab, yb_f, yb_b, xc, u_ab, u_ab, yc_f, yc_b, u_c, wa, wb, wc, dskip_x)


def _first_argmax(vals, lane):
    m = jnp.max(vals, axis=1, keepdims=True)
    idx = jnp.min(jnp.where(vals == m, lane, 1 << 20), axis=1, keepdims=True)
    return m, idx


def _route_kernel(lg_ref, bias_ref, idx_ref, w_ref, rank_ref, cnt_ref, run_s):
    @pl.when(pl.program_id(0) == 0)
    def _():
        run_s[...] = jnp.zeros_like(run_s)

    logits = lg_ref[...]
    tm = logits.shape[0]
    lane = lax.broadcasted_iota(jnp.int32, (tm, 128), 1)
    is_expert = lane < N_EXPERTS
    scores = jax.nn.sigmoid(logits)
    biased = jnp.where(is_expert, scores + bias_ref[...], NEG_INF)
    per_group = N_EXPERTS // N_EXPERT_GROUPS
    grp = lane // per_group
    gscore = jnp.full((tm, 128), NEG_INF, F32)
    for g in range(N_EXPERT_GROUPS):
        vals = jnp.where(grp == g, biased, NEG_INF)
        m1, i1 = _first_argmax(vals, lane)
        m2 = jnp.max(jnp.where(lane == i1, NEG_INF, vals), axis=1, keepdims=True)
        gscore = jnp.where(lane == g, m1 + m2, gscore)
    allowed = jnp.zeros((tm, 128), jnp.bool_)
    for _ in range(TOPK_GROUPS):
        _, gi = _first_argmax(gscore, lane)
        allowed = allowed | (grp == gi)
        gscore = jnp.where(lane == gi, NEG_INF, gscore)
    masked = jnp.where(allowed & is_expert, biased, NEG_INF)
    idx_out = jnp.zeros((tm, 128), jnp.int32)
    w_out = jnp.zeros((tm, 128), F32)
    picks = []
    chosen = jnp.zeros((tm, 128), F32)
    for kk in range(TOP_K):
        _, ei = _first_argmax(masked, lane)
        sel = lane == ei
        picks.append(sel)
        chosen = chosen + sel.astype(F32)
        wk = jnp.sum(jnp.where(sel, scores, 0.0), axis=1, keepdims=True)
        idx_out = jnp.where(lane == kk, ei, idx_out)
        w_out = jnp.where(lane == kk, wk, w_out)
        masked = jnp.where(sel, NEG_INF, masked)
    w_out = w_out / jnp.sum(w_out, axis=1, keepdims=True) * ROUTED_SCALE
    idx_ref[...] = idx_out
    w_ref[...] = w_out
    strict = (lax.broadcasted_iota(jnp.int32, (tm, tm), 1)
              < lax.broadcasted_iota(jnp.int32, (tm, tm), 0)).astype(F32)
    before = _dot(strict, chosen) + run_s[...]
    rank_out = jnp.zeros((tm, 128), jnp.int32)
    for kk in range(TOP_K):
        rk = jnp.sum(jnp.where(picks[kk], before, 0.0), axis=1, keepdims=True)
        rank_out = jnp.where(lane == kk, rk.astype(jnp.int32), rank_out)
    rank_ref[...] = rank_out
    total = run_s[...] + jnp.sum(chosen, axis=0, keepdims=True)
    run_s[...] = total
    cnt_ref[...] = total.astype(jnp.int32)


def route(logits, router_bias, tm=1024):
    t = logits.shape[0]
    bias = jnp.concatenate([router_bias.astype(F32), jnp.zeros((128 - N_EXPERTS,), F32)])[None]
    row = pl.BlockSpec((tm, 128), lambda i: (i, 0))
    one = pl.BlockSpec((1, 128), lambda i: (0, 0))
    return pl.pallas_call(
        _route_kernel,
        grid=(t // tm,),
        in_specs=[row, one],
        out_specs=[row, row, row, one],
        out_shape=[jax.ShapeDtypeStruct((t, 128), jnp.int32), jax.ShapeDtypeStruct((t, 128), F32),
                   jax.ShapeDtypeStruct((t, 128), jnp.int32), jax.ShapeDtypeStruct((1, 128), jnp.int32)],
        scratch_shapes=[pltpu.VMEM((1, 128), F32)],
        compiler_params=_cparams(("arbitrary",)),
        name="route",
    )(logits, bias)


EXP_TM = 256
SCT_TM = 256


def _scatter_kernel(lt_ref, pos_ref, hp_ref, xs_hbm, zero_s, sem, zsem):
    i = pl.program_id(0)

    @pl.when(i == 0)
    def _():
        zero_s[...] = jnp.zeros_like(zero_s)
        for e in range(2 * N_EXPERTS):
            @pl.when(lt_ref[e] >= 0)
            def _():
                pltpu.make_async_copy(zero_s, xs_hbm.at[pl.ds(lt_ref[e] * EXP_TM, EXP_TM), :], zsem).start()
        for e in range(2 * N_EXPERTS):
            @pl.when(lt_ref[e] >= 0)
            def _():
                pltpu.make_async_copy(zero_s, xs_hbm.at[pl.ds(lt_ref[e] * EXP_TM, EXP_TM), :], zsem).wait()

    def body(r, carry):
        for kk in range(TOP_K):
            p = pos_ref[0, r * TOP_K + kk]
            pltpu.make_async_copy(hp_ref.at[pl.ds(r, 1), :], xs_hbm.at[pl.ds(p, 1), :], sem).start(
                priority=kk % 2)
        return carry
    lax.fori_loop(0, SCT_TM, body, 0)
    for kk in range(TOP_K):
        pltpu.make_async_copy(hp_ref, xs_hbm.at[pl.ds(0, SCT_TM), :], sem).wait()


def scatter_rows(hp, pos, clear_tiles, n_rows):
    t, half = hp.shape
    n = t // SCT_TM
    grid_spec = pltpu.PrefetchScalarGridSpec(
        num_scalar_prefetch=1,
        grid=(n,),
        in_specs=[
            pl.BlockSpec((None, 1, SCT_TM * TOP_K), lambda i, lt: (i, 0, 0), memory_space=pltpu.SMEM),
            pl.BlockSpec((SCT_TM, half), lambda i, lt: (i, 0)),
        ],
        out_specs=pl.BlockSpec(memory_space=pl.ANY),
        scratch_shapes=[
            pltpu.VMEM((EXP_TM, half), jnp.uint32),
            pltpu.SemaphoreType.DMA(()),
            pltpu.SemaphoreType.DMA(()),
        ],
    )
    return pl.pallas_call(
        _scatter_kernel,
        grid_spec=grid_spec,
        out_shape=jax.ShapeDtypeStruct((n_rows, half), jnp.uint32),
        compiler_params=_cparams(("arbitrary",)),
        name="scatter_rows",
    )(clear_tiles, pos.reshape(n, 1, SCT_TM * TOP_K), hp)


def _expert_weight_copies(e, layer, wg_hbm, wu_hbm, wd_hbm, wg_buf, wu_buf, wd_buf, sem, slot):
    return [pltpu.make_async_copy(wg_hbm.at[layer, e], wg_buf.at[slot], sem.at[slot]),
            pltpu.make_async_copy(wu_hbm.at[layer, e], wu_buf.at[slot], sem.at[slot]),
            pltpu.make_async_copy(wd_hbm.at[layer, e], wd_buf.at[slot], sem.at[slot])]


def _expert_kernel(te_ref, nu_ref, nx_ref, x_ref, wg_hbm, wu_hbm, wd_hbm, y_ref,
                   wg_buf, wu_buf, wd_buf, sem, wgu_s, wd_s, grp_s, *, layer):
    i = pl.program_id(0)
    n_used = nu_ref[0]
    half = x_ref.shape[1]
    copies = functools.partial(_expert_weight_copies, layer=layer, wg_hbm=wg_hbm, wu_hbm=wu_hbm,
                               wd_hbm=wd_hbm, wg_buf=wg_buf, wu_buf=wu_buf, wd_buf=wd_buf, sem=sem)

    @pl.when(i == 0)
    def _():
        grp_s[0] = 0
        for cp in copies(te_ref[0], slot=0):
            cp.start()

    new_expert = (i == 0) | (te_ref[i] != te_ref[jnp.maximum(i - 1, 0)])

    @pl.when(new_expert & (i < n_used))
    def _():
        slot = grp_s[0] % 2
        for cp in copies(te_ref[i], slot=slot):
            cp.wait()
        wgu_s[:, 0:D_EXPERT] = wg_buf[slot].astype(BF16)
        wgu_s[:, D_EXPERT:2 * D_EXPERT] = wu_buf[slot].astype(BF16)
        wd_s[...] = wd_buf[slot].astype(BF16)

        @pl.when(nx_ref[i] >= 0)
        def _():
            for cp in copies(nx_ref[i], slot=1 - slot):
                cp.start()
        grp_s[0] = grp_s[0] + 1

    @pl.when(i < n_used)
    def _():
        x_hi, x_lo = _unpack_bf16_pair(x_ref[...])
        h = (jnp.dot(x_hi.astype(BF16), wgu_s[0:half, :], preferred_element_type=F32)
             + jnp.dot(x_lo.astype(BF16), wgu_s[half:2 * half, :], preferred_element_type=F32))
        act = _silu(h[:, 0:D_EXPERT]) * h[:, D_EXPERT:2 * D_EXPERT]
        y = jnp.dot(act.astype(BF16), wd_s[...], preferred_element_type=F32)
        y_ref[...] = _pack_bf16_pair(y[:, 0:half], y[:, half:2 * half])

    @pl.when(i >= n_used)
    def _():
        y_ref[...] = jnp.zeros_like(y_ref)


def routed_experts(x_sorted, tile_expert, n_used, next_expert, w_gate, w_up, w_down, layer):
    n_rows, half = x_sorted.shape
    n_tiles = n_rows // EXP_TM
    d = 2 * half
    grid_spec = pltpu.PrefetchScalarGridSpec(
        num_scalar_prefetch=3,
        grid=(n_tiles,),
        in_specs=[
            pl.BlockSpec((EXP_TM, half), lambda i, te, nu, nx: (jnp.minimum(i, nu[0] - 1), 0)),
            pl.BlockSpec(memory_space=pl.ANY),
            pl.BlockSpec(memory_space=pl.ANY),
            pl.BlockSpec(memory_space=pl.ANY),
        ],
        out_specs=pl.BlockSpec((EXP_TM, half), lambda i, te, nu, nx: (i, 0)),
        scratch_shapes=[
            pltpu.VMEM((2, d, D_EXPERT), F32),
            pltpu.VMEM((2, d, D_EXPERT), F32),
            pltpu.VMEM((2, D_EXPERT, d), F32),
            pltpu.SemaphoreType.DMA((2,)),
            pltpu.VMEM((d, 2 * D_EXPERT), BF16),
            pltpu.VMEM((D_EXPERT, d), BF16),
            pltpu.SMEM((1,), jnp.int32),
        ],
    )
    return pl.pallas_call(
        functools.partial(_expert_kernel, layer=layer),
        grid_spec=grid_spec,
        out_shape=jax.ShapeDtypeStruct((n_rows, half), jnp.uint32),
        compiler_params=_cparams(("arbitrary",)),
        name="routed_experts",
    )(tile_expert, n_used, next_expert, x_sorted, w_gate, w_up, w_down)


CMB_TM = 128


def _start_combine_gather(pos_ref, y_hbm, dst, sem):
    def body(r, carry):
        for kk in range(TOP_K):
            p = pos_ref[0, r * TOP_K + kk]
            pltpu.make_async_copy(y_hbm.at[pl.ds(p, 1), :], dst.at[kk, pl.ds(r, 1), :], sem).start(
                priority=kk % 2)
        return carry
    lax.fori_loop(0, CMB_TM, body, 0)


def _combine_kernel(pos_ref, posn_ref, y_hbm, w_ref, hp_ref, x_ref, gate_ref, sgu_ref, sd_ref,
                    norm_ref, o_ref, ybuf, sem, *, final_norm):
    i = pl.program_id(0)
    n = pl.num_programs(0)
    slot = i % 2
    half = hp_ref.shape[1]

    @pl.when(i == 0)
    def _():
        _start_combine_gather(pos_ref, y_hbm, ybuf.at[0], sem.at[0])

    @pl.when(i + 1 < n)
    def _():
        _start_combine_gather(posn_ref, y_hbm, ybuf.at[1 - slot], sem.at[1 - slot])

    x_hi, x_lo = _unpack_bf16_pair(hp_ref[...])
    h = (jnp.dot(x_hi.astype(BF16), sgu_ref[0:half, :], preferred_element_type=F32)
         + jnp.dot(x_lo.astype(BF16), sgu_ref[half:2 * half, :], preferred_element_type=F32))
    act = _silu(h[:, 0:D_SHARED]) * h[:, D_SHARED:2 * D_SHARED]
    shared = jnp.dot(act.astype(BF16), sd_ref[...], preferred_element_type=F32)

    for kk in range(TOP_K):
        pltpu.make_async_copy(y_hbm.at[pl.ds(0, CMB_TM), :], ybuf.at[slot, kk], sem.at[slot]).wait()
    acc_hi = shared[:, 0:half]
    acc_lo = shared[:, half:2 * half]
    w = w_ref[...]
    for kk in range(TOP_K):
        y_hi, y_lo = _unpack_bf16_pair(ybuf[slot, kk])
        wk = w[:, kk:kk + 1]
        acc_hi = acc_hi + wk * y_hi
        acc_lo = acc_lo + wk * y_lo
    g = gate_ref[...]
    out_hi = x_ref[:, 0:half] + g[:, 0:half] * acc_hi
    out_lo = x_ref[:, half:2 * half] + g[:, half:2 * half] * acc_lo
    if final_norm:
        ms = (jnp.sum(out_hi * out_hi, axis=1, keepdims=True)
              + jnp.sum(out_lo * out_lo, axis=1, keepdims=True)) / (2 * half)
        inv = lax.rsqrt(ms + EPS)
        out_hi = out_hi * inv * norm_ref[:, 0:half]
        out_lo = out_lo * inv * norm_ref[:, half:2 * half]
    o_ref[:, 0:half] = out_hi
    o_ref[:, half:2 * half] = out_lo


def moe_combine(pos, y_sorted, w, hp, x, gate, seg_of_tile, sh_gate_up, sh_down, norm_w, final_norm):
    t, d = x.shape
    half = d // 2
    n = t // CMB_TM
    return pl.pallas_call(
        functools.partial(_combine_kernel, final_norm=final_norm),
        grid=(n,),
        in_specs=[
            pl.BlockSpec((None, 1, CMB_TM * TOP_K), lambda i: (i, 0, 0), memory_space=pltpu.SMEM),
            pl.BlockSpec((None, 1, CMB_TM * TOP_K), lambda i: (jnp.minimum(i + 1, n - 1), 0, 0),
                         memory_space=pltpu.SMEM),
            pl.BlockSpec(memory_space=pl.ANY),
            pl.BlockSpec((CMB_TM, 128), lambda i: (i, 0)),
            pl.BlockSpec((CMB_TM, half), lambda i: (i, 0)),
            pl.BlockSpec((CMB_TM, d), lambda i: (i, 0)),
            pl.BlockSpec((None, 1, d), lambda i: (seg_of_tile(i), 0, 0)),
            pl.BlockSpec((d, 2 * D_SHARED), lambda i: (0, 0)),
            pl.BlockSpec((D_SHARED, d), lambda i: (0, 0)),
            pl.BlockSpec((1, d), lambda i: (0, 0)),
        ],
        out_specs=pl.BlockSpec((CMB_TM, d), lambda i: (i, 0)),
        out_shape=jax.ShapeDtypeStruct((t, d), F32),
        scratch_shapes=[
            pltpu.VMEM((2, TOP_K, CMB_TM, half), jnp.uint32),
            pltpu.SemaphoreType.DMA((2,)),
        ],
        compiler_params=_cparams(("arbitrary",)),
        name="moe_combine",
    )(pos, pos, y_sorted, w, hp, x, gate, sh_gate_up, sh_down, norm_w)


def moe_plan(eidx, rank, counts, n_tiles):
    tiles_per = (counts + EXP_TM - 1) // EXP_TM
    tile_end = jnp.cumsum(tiles_per)
    base = (tile_end - tiles_per) * EXP_TM
    experts = jnp.arange(N_EXPERTS, dtype=jnp.int32)
    pos = rank + jnp.sum(jnp.where(eidx[..., None] == experts, base, 0), axis=-1)
    tile_expert = jnp.sum((tile_end[None, :] <= jnp.arange(n_tiles, dtype=jnp.int32)[:, None])
                          .astype(jnp.int32), axis=1)
    tile_expert = jnp.minimum(tile_expert, N_EXPERTS - 1)
    last_tile = jnp.where(tiles_per > 0, tile_end - 1, -1)
    unused = tile_end[-1] + experts
    clear_tiles = jnp.concatenate([last_tile, jnp.where(unused < n_tiles, unused, -1)])
    later = (experts[None, :] > experts[:, None]) & (tiles_per[None, :] > 0)
    nxt = jnp.min(jnp.where(later, experts[None, :], N_EXPERTS), axis=1)
    nxt = jnp.where(nxt < N_EXPERTS, nxt, -1)
    next_expert = jnp.sum(jnp.where(tile_expert[:, None] == experts[None, :], nxt[None, :], 0), axis=1)
    return tile_expert, tile_end[-1:], clear_tiles, next_expert, pos


def lb_table(lb):
    lb = lb.astype(F32)
    return jnp.stack([jnp.log(lb), jnp.log1p(-lb), 1.0 - lb])


def mlstm_gate_bias(i_bias, f_bias):
    z = jnp.zeros((64,), F32)
    return jnp.concatenate([z, i_bias.astype(F32).reshape(-1), f_bias.astype(F32).reshape(-1),
                            jnp.zeros((32,), F32)])[None]


def ssd_consts(dt_bias, a_log):
    bias = jnp.concatenate([dt_bias.astype(F32).reshape(-1), jnp.zeros((64,), F32)])[None]
    a = -jnp.exp(a_log.astype(F32))
    expand = (jnp.arange(B_HEADS)[:, None] == jnp.arange(B_WIDTH)[None, :] // B_HEADDIM).astype(BF16)
    return [(bias, a[d][None], expand) for d in range(2)]


def _mod_tables(mod, nb):
    d = mod.shape[1] // 6
    parts = mod.reshape(mod.shape[0], 6, d)
    lat = parts[:nb]
    ctx = jnp.broadcast_to(parts[nb][None], (nb, 6, d))
    tab = jnp.stack([ctx, lat], axis=1).reshape(2 * nb, 6, d)
    return jnp.transpose(tab, (1, 0, 2))[:, :, None, :]


def _seg_fn(rows_per_batch, ctx_rows, tm):
    tiles_per_batch = rows_per_batch // tm
    ctx_tiles = ctx_rows // tm

    def seg(i):
        return 2 * (i // tiles_per_batch) + jnp.where(i % tiles_per_batch >= ctx_tiles, 1, 0)
    return seg


def _to_scan_order(a, rows):
    nb, _, d = a.shape
    return a.reshape(nb, rows, GRID_W, d).transpose(0, 2, 1, 3).reshape(nb, rows * GRID_W, d)


def _from_scan_order(a, rows):
    nb, _, d = a.shape
    return a.reshape(nb, GRID_W, rows, d).transpose(0, 2, 1, 3).reshape(nb, rows * GRID_W, d)


def _moe_block(x2, tab, seg_mod, seg_cmb, out_norm_w, final_norm, layer, norm_w, router_w, router_bias,
               w_gate, w_up, w_down, sh_gate, sh_up, sh_down):
    t, d = x2.shape
    rw = jnp.concatenate([router_w.astype(F32), jnp.zeros((d, 128 - N_EXPERTS), F32)], axis=1)
    rw_hi = rw.astype(BF16)
    rw = jnp.concatenate([rw_hi, (rw - rw_hi.astype(F32)).astype(BF16)], axis=1)
    hp, logits = modulate(x2, norm_w, tab[3], tab[4], seg_mod, 256, router_w=rw)
    eidx, ew, rank, counts = route(logits, router_bias)
    n_tiles = t * TOP_K // EXP_TM + N_EXPERTS
    tile_expert, n_used, clear_tiles, next_expert, pos = moe_plan(
        eidx[:, :TOP_K], rank[:, :TOP_K], counts[0, :N_EXPERTS], n_tiles)
    x_sorted = scatter_rows(hp, pos, clear_tiles, n_tiles * EXP_TM)
    y_sorted = routed_experts(x_sorted, tile_expert, n_used, next_expert, w_gate, w_up, w_down, layer)
    sgu = jnp.concatenate([sh_gate, sh_up], axis=1).astype(BF16)
    return moe_combine(pos.reshape(t // CMB_TM, 1, CMB_TM * TOP_K), y_sorted, ew, hp, x2, tab[5],
                       seg_cmb, sgu, sh_down.astype(BF16), out_norm_w.reshape(1, d), final_norm)


def kernel(x, c, ctx, c_ctx, ada_w, ada_b, norm_mix, norm_ffn, norm_final, w_in, w_out, hgrn_lb,
           hgrn_norm, ssm_conv_w, ssm_conv_b, ssm_dt_bias, ssm_a_log, ssm_d, ssm_norm, mlstm_i_bias,
           mlstm_f_bias, mlstm_norm, router_w, router_bias, moe_w_gate, moe_w_up, moe_w_down,
           shared_w_gate, shared_w_up, shared_w_down):
    nb, seq, d = x.shape
    n_ctx = ctx.shape[1]
    length = n_ctx + seq
    rows = seq // GRID_W
    depth = ada_w.shape[0]
    t_all = nb * length
    t_lat = nb * seq

    lb_all = jnp.cumsum(jax.nn.softmax(hgrn_lb.astype(F32), axis=0), axis=0)
    lb_all = lb_all - lb_all[0]
    cond = jnp.concatenate([c, c_ctx[None], jnp.zeros((8 - nb - 1, d), F32)], axis=0)
    mod = ada_modulation(cond, ada_w, ada_b)

    xs = jnp.concatenate([ctx, x], axis=1)
    w_in_t = jnp.swapaxes(w_in, 1, 2)
    for l in range(depth):
        last = l == depth - 1
        tab = _mod_tables(mod[l], nb)
        h = modulate(xs.reshape(t_all, d), norm_mix[l], tab[0], tab[1], _seg_fn(length, n_ctx, 256), 256)
        if l % 2 == 1:
            h3 = h.reshape(nb, length, d)
            h = jnp.concatenate([h3[:, :n_ctx], _to_scan_order(h3[:, n_ctx:], rows)], axis=1)
            h = h.reshape(t_all, d)
        w_c = w_in_t[l:l + 1, U_C0:U_G0]
        w_small = jnp.concatenate([w_in_t[l:l + 1, U_DT0:U_C0], w_in_t[l:l + 1, U_G0:],
                                   jnp.zeros((1, U_SMALL - 96, d), F32)], axis=1)
        u_ab = matmul_nt(h, w_in_t, l, tm=1024, tn=512, n_tiles=U_AB // 512, name="in_proj_ab")
        u_c = matmul_nt(h, w_c, 0, tm=1024, tn=512, name="in_proj_c")
        u_s = matmul_nt(h, w_small, 0, tm=1024, tn=U_SMALL, name="in_proj_small")
        u_ab3 = u_ab.reshape(nb, length, U_AB)
        u_c3 = u_c.reshape(nb, length, 4 * C_WIDTH)
        u_s3 = u_s.reshape(nb, length, U_SMALL)

        xc = ssd_conv(u_ab3, ssm_conv_w[l], ssm_conv_b[l], n_ctx)
        consts = ssd_consts(ssm_dt_bias[l], ssm_a_log[l])
        gate_bias = mlstm_gate_bias(mlstm_i_bias[l], mlstm_f_bias[l])
        ya = [hgrn2_scan(u_ab3, lb_table(lb_all[l][dd]), n_ctx, rev=bool(dd)) for dd in range(2)]
        yb = [ssd_scan(xc, u_s3, *consts[dd], n_ctx, rev=bool(dd)) for dd in range(2)]
        yc = mlstm_scan(u_c3, u_s3, gate_bias, n_ctx)
        flat = lambda a: a.reshape(t_all, a.shape[-1])
        ymix = mix_out(flat(ya[0]), flat(ya[1]), flat(yb[0]), flat(yb[1]), flat(yc[0]), flat(yc[1]),
                       u_ab, flat(xc), u_c, hgrn_norm[l][None], ssm_norm[l][None], mlstm_norm[l][None],
                       jnp.repeat(ssm_d[l].astype(F32), B_HEADDIM)[None])
        if l % 2 == 1:
            y3 = ymix.reshape(nb, length, d)
            ymix = jnp.concatenate([y3[:, :n_ctx], _from_scan_order(y3[:, n_ctx:], rows)], axis=1)
            ymix = ymix.reshape(t_all, d)

        moe_w = (l, norm_ffn[l], router_w[l], router_bias[l], moe_w_gate, moe_w_up, moe_w_down,
                 shared_w_gate[l], shared_w_up[l], shared_w_down[l])
        ymix = ymix.reshape(nb, length, d)
        w_o = w_out[l].astype(BF16)
        if not last:
            x2 = matmul_residual(ymix, w_o, xs, tab[2], first_row=0, n_rows=length,
                                 ctx_rows=n_ctx, tm=256, tn=1024).reshape(t_all, d)
            x2 = _moe_block(x2, tab, _seg_fn(length, n_ctx, 256), _seg_fn(length, n_ctx, CMB_TM),
                            norm_final, False, *moe_w)
            xs = x2.reshape(nb, length, d)
        else:
            x2 = matmul_residual(ymix, w_o, xs, tab[2], first_row=n_ctx, n_rows=seq,
                                 ctx_rows=n_ctx, tm=256, tn=1024).reshape(t_lat, d)
            out = _moe_block(x2, tab, _seg_fn(seq, 0, 256), _seg_fn(seq, 0, CMB_TM),
                             norm_final, True, *moe_w)
            return out.reshape(nb, seq, d)
```

```python
import functools

import jax
import jax.numpy as jnp
from jax import lax
from jax.experimental import pallas as pl
from jax.experimental.pallas import tpu as pltpu

F32 = jnp.float32
BF16 = jnp.bfloat16
HIGHEST = lax.Precision.HIGHEST
NEG_INF = float("-inf")

D_MODEL = 4096
GRID_W = 64
CHUNK = 64
EPS = 1e-6
A_WIDTH = 1024
A_DK = 128
A_HEADS = 8
B_WIDTH = 2048
B_HEADDIM = 64
B_HEADS = 32
B_GROUPS = 8
B_RPG = 4
B_STATE = 128
B_CONV = 5
B_CONV_CH = 4096
C_WIDTH = 1024
C_DK = 128
C_HEADS = 8
N_EXPERTS = 64
N_EXPERT_GROUPS = 8
TOPK_GROUPS = 4
TOP_K = 8
D_EXPERT = 256
D_SHARED = 256
ROUTED_SCALE = 2.5

U_AB = 11264
U_DT0 = 11264
U_C0 = 11328
U_G0 = 15424
U_SMALL = 128

SUB = 16
PAD = 16
MAX_SAFE_EXPONENT = 80.0
VMEM_LIMIT = 56 * 1024 * 1024


def _cparams(sem):
    return pltpu.CompilerParams(dimension_semantics=sem, vmem_limit_bytes=VMEM_LIMIT)


def _dot(a, b):
    return jnp.dot(a.astype(BF16), b.astype(BF16), preferred_element_type=F32)


def _bdot(a, b, ca, cb):
    return lax.dot_general(a.astype(BF16), b.astype(BF16), (((ca,), (cb,)), ((0,), (0,))),
                           preferred_element_type=F32)


def _dot_hi(a, b):
    return jnp.dot(a, b, precision=HIGHEST, preferred_element_type=F32)


def _dot_nt_hi(a, b):
    return lax.dot_general(a, b, (((1,), (1,)), ((), ())), precision=HIGHEST,
                           preferred_element_type=F32)


def _log_sigmoid(z):
    return jnp.minimum(z, 0.0) - jnp.log(1.0 + jnp.exp(-jnp.abs(z)))


def _softplus(z):
    return jnp.maximum(z, 0.0) + jnp.log(1.0 + jnp.exp(-jnp.abs(z)))


def _silu(z):
    return z * jax.nn.sigmoid(z)


def _tri(rev):
    ri = lax.broadcasted_iota(jnp.int32, (CHUNK, CHUNK), 0)
    ci = lax.broadcasted_iota(jnp.int32, (CHUNK, CHUNK), 1)
    return (ci >= ri) if rev else (ci <= ri)


def _chunk_block(c, n_ctx_chunks, n_lat_chunks, rev):
    if not rev:
        return c
    return jnp.where(c < n_ctx_chunks, n_ctx_chunks - 1 - c, 2 * n_ctx_chunks + n_lat_chunks - 1 - c)


def _ada_kernel(c_ref, w_ref, b_ref, o_ref):
    s = _silu(c_ref[...])
    o_ref[...] = _dot(s, w_ref[...]) + b_ref[...]


def ada_modulation(cond, ada_w, ada_b):
    depth, d, n = ada_w.shape
    tn = 512
    return pl.pallas_call(
        _ada_kernel,
        grid=(depth, n // tn),
        in_specs=[
            pl.BlockSpec((8, d), lambda l, j: (0, 0)),
            pl.BlockSpec((None, d, tn), lambda l, j: (l, 0, j)),
            pl.BlockSpec((None, 1, tn), lambda l, j: (l, 0, j)),
        ],
        out_specs=pl.BlockSpec((None, 8, tn), lambda l, j: (l, 0, j)),
        out_shape=jax.ShapeDtypeStruct((depth, 8, n), F32),
        compiler_params=_cparams(("arbitrary", "arbitrary")),
        name="ada_modulation",
    )(cond, ada_w, ada_b.reshape(depth, 1, n))


def _pack_bf16_pair(a, b):
    hi = lax.bitcast_convert_type(a.astype(BF16).astype(F32), jnp.uint32)
    lo = lax.bitcast_convert_type(b.astype(BF16).astype(F32), jnp.uint32)
    return hi | (lo >> 16)


def _unpack_bf16_pair(w):
    hi = lax.bitcast_convert_type(w & jnp.uint32(0xFFFF0000), F32)
    lo = lax.bitcast_convert_type(w << 16, F32)
    return hi, lo


def _modulate_kernel(x_ref, g_ref, sh_ref, sc_ref, *rest, with_router):
    x = x_ref[...]
    y = x * lax.rsqrt(jnp.mean(x * x, axis=-1, keepdims=True) + EPS) * g_ref[...]
    h = y * (1.0 + sc_ref[...]) + sh_ref[...]
    if with_router:
        rw_ref, h_ref, lg_ref = rest
        h_hi = h.astype(BF16)
        h_lo = (h - h_hi.astype(F32)).astype(BF16)
        both = jnp.dot(h_hi, rw_ref[...], preferred_element_type=F32)
        lg_ref[...] = (both[:, 0:128] + both[:, 128:256]
                       + jnp.dot(h_lo, rw_ref[:, 0:128], preferred_element_type=F32))
        half = h.shape[1] // 2
        h_ref[...] = _pack_bf16_pair(h[:, :half], h[:, half:])
    else:
        (h_ref,) = rest
        h_ref[...] = h.astype(h_ref.dtype)


def modulate(x, g, shift, scale, seg_of_tile, tm, router_w=None):
    t, d = x.shape
    with_router = router_w is not None
    vec = pl.BlockSpec((None, 1, d), lambda i: (seg_of_tile(i), 0, 0))
    in_specs = [pl.BlockSpec((tm, d), lambda i: (i, 0)),
                pl.BlockSpec((1, d), lambda i: (0, 0)), vec, vec]
    args = [x, g.reshape(1, d), shift, scale]
    if with_router:
        in_specs.append(pl.BlockSpec((d, 256), lambda i: (0, 0)))
        args.append(router_w)
        out_specs = [pl.BlockSpec((tm, d // 2), lambda i: (i, 0)),
                     pl.BlockSpec((tm, 128), lambda i: (i, 0))]
        out_shape = [jax.ShapeDtypeStruct((t, d // 2), jnp.uint32),
                     jax.ShapeDtypeStruct((t, 128), F32)]
    else:
        out_specs = pl.BlockSpec((tm, d), lambda i: (i, 0))
        out_shape = jax.ShapeDtypeStruct((t, d), BF16)
    return pl.pallas_call(
        functools.partial(_modulate_kernel, with_router=with_router),
        grid=(t // tm,),
        in_specs=in_specs, out_specs=out_specs, out_shape=out_shape,
        compiler_params=_cparams(("arbitrary",)),
        name="modulate_router" if with_router else "modulate",
    )(*args)


def _mm_nt_kernel(a_ref, b_ref, o_ref):
    o_ref[...] = lax.dot_general(a_ref[...], b_ref[...].astype(BF16), (((1,), (1,)), ((), ())),
                                 preferred_element_type=F32)


def matmul_nt(a, b_t, layer, *, tm, tn, n_tiles=None, name="matmul_nt"):
    m, k = a.shape
    n_tiles = b_t.shape[1] // tn if n_tiles is None else n_tiles
    return pl.pallas_call(
        _mm_nt_kernel,
        grid=(m // tm, n_tiles),
        in_specs=[pl.BlockSpec((tm, k), lambda i, j: (i, 0)),
                  pl.BlockSpec((None, tn, k), lambda i, j: (layer, j, 0))],
        out_specs=pl.BlockSpec((tm, tn), lambda i, j: (i, j)),
        out_shape=jax.ShapeDtypeStruct((m, n_tiles * tn), F32),
        compiler_params=_cparams(("arbitrary", "arbitrary")),
        name=name,
    )(a, b_t)


def _mm_res_kernel(a_ref, b_ref, r_ref, g_ref, o_ref):
    acc = jnp.dot(a_ref[...], b_ref[...], preferred_element_type=F32)
    o_ref[...] = r_ref[...] + g_ref[...] * acc


def matmul_residual(a, b, res, gate, *, first_row, n_rows, ctx_rows, tm, tn):
    nb, _, k = a.shape
    n = b.shape[1]
    t0 = first_row // tm
    ctx_tiles = ctx_rows // tm
    seg = lambda bb, it: 2 * bb + jnp.where(t0 + it >= ctx_tiles, 1, 0)
    return pl.pallas_call(
        _mm_res_kernel,
        grid=(n // tn, nb, n_rows // tm),
        in_specs=[pl.BlockSpec((None, tm, k), lambda j, bb, it: (bb, t0 + it, 0)),
                  pl.BlockSpec((k, tn), lambda j, bb, it: (0, j)),
                  pl.BlockSpec((None, tm, tn), lambda j, bb, it: (bb, t0 + it, j)),
                  pl.BlockSpec((None, 1, tn), lambda j, bb, it: (seg(bb, it), 0, j))],
        out_specs=pl.BlockSpec((None, tm, tn), lambda j, bb, it: (bb, it, j)),
        out_shape=jax.ShapeDtypeStruct((nb, n_rows, n), F32),
        compiler_params=_cparams(("arbitrary", "arbitrary", "arbitrary")),
        name="matmul_residual",
    )(a, b, res, gate)


def _hgrn2_kernel(q_ref, f_ref, v_ref, lb_ref, y_ref, k_s, v_s, gc_s, st_s, *, rev):
    c = pl.program_id(1)

    @pl.when(c == 0)
    def _():
        st_s[...] = jnp.zeros_like(st_s)
        k_s[...] = jnp.zeros_like(k_s)
        v_s[...] = jnp.zeros_like(v_s)
        gc_s[...] = jnp.zeros_like(gc_s)

    nh = A_HEADS
    q = q_ref[0]
    z = f_ref[0]
    log_lb = lb_ref[0:1, :]
    log1m_lb = lb_ref[1:2, :]
    one_m_lb = lb_ref[2:3, :]
    b2 = log1m_lb + _log_sigmoid(z)
    logf = jnp.maximum(log_lb, b2) + jnp.log(1.0 + jnp.exp(-jnp.abs(log_lb - b2)))
    gc2 = _dot_hi(_tri(rev).astype(F32), logf)
    heads = lambda a: jnp.stack([a[:, h * A_DK:(h + 1) * A_DK] for h in range(nh)])
    qh = heads(q * jax.nn.sigmoid(q) * (A_DK ** -0.5))
    kh = heads(one_m_lb * jax.nn.sigmoid(-z))
    vh = heads(v_ref[0])
    gcm = heads(gc2)
    gxm = gcm - heads(logf)
    st = st_s[...]
    y_inter = _bdot(qh * jnp.exp(gcm), st, 2, 2)
    mid = gcm[:, CHUNK // 2:CHUNK // 2 + 1, :]

    def factored(_):
        sc = _bdot(qh * jnp.exp(gcm - mid), kh * jnp.exp(mid - gcm), 2, 2)
        return _bdot(jnp.where(_tri(rev)[None], sc, 0.0), vh, 2, 1)

    def exact(_):
        k_s[:, PAD:PAD + CHUNK, :] = kh
        v_s[:, PAD:PAD + CHUNK, :] = vh
        gc_s[:, PAD:PAD + CHUNK, :] = gcm
        rows = lax.broadcasted_iota(jnp.int32, (1, CHUNK, 1), 1)
        rowmod = rows % SUB
        n_sub = CHUNK // SUB
        y = jnp.zeros((nh, CHUNK, A_DK), F32)
        for dlt in range(SUB):
            off = PAD + dlt if rev else PAD - dlt
            ksh = k_s[:, off:off + CHUNK, :]
            vsh = v_s[:, off:off + CHUNK, :]
            gsh = gc_s[:, off:off + CHUNK, :]
            valid = (rowmod + dlt < SUB) if rev else (rowmod >= dlt)
            e = jnp.exp(jnp.where(valid, gcm - gsh, NEG_INF))
            r = jnp.sum(qh * ksh * e, axis=2, keepdims=True)
            y = y + r * vsh
        pieces = []
        for blk in range(n_sub):
            r0, r1 = blk * SUB, (blk + 1) * SUB
            first = (blk == n_sub - 1) if rev else (blk == 0)
            if first:
                pieces.append(jnp.zeros((nh, SUB, A_DK), F32))
                continue
            ref = gxm[:, r1 - 1:r1, :] if rev else gxm[:, r0:r0 + 1, :]
            earlier = (rows >= r1) if rev else (rows < r0)
            qhat = qh[:, r0:r1, :] * jnp.exp(gcm[:, r0:r1, :] - ref)
            khat = kh * jnp.exp(jnp.where(earlier, ref - gcm, NEG_INF))
            sc = _bdot(qhat, khat, 2, 2)
            pieces.append(_bdot(sc, vh, 2, 1))
        return y + jnp.concatenate(pieces, axis=1)

    worst = jnp.max(jnp.abs(gcm - mid))
    y = y_inter + lax.cond(worst < MAX_SAFE_EXPONENT, factored, exact, 0)
    gtot = gcm[:, 0:1, :] if rev else gcm[:, CHUNK - 1:CHUNK, :]
    st_s[...] = st * jnp.exp(gtot) + _bdot(vh, kh * jnp.exp(gtot - gcm), 1, 1)
    for h in range(nh):
        y_ref[0, :, h * A_DK:(h + 1) * A_DK] = y[h]


def hgrn2_scan(u_ab, lb_tab, n_ctx, rev):
    nb, length, _ = u_ab.shape
    n_chunks = length // CHUNK
    n_ctx_chunks = n_ctx // CHUNK
    blk = functools.partial(_chunk_block, n_ctx_chunks=n_ctx_chunks,
                            n_lat_chunks=n_chunks - n_ctx_chunks, rev=rev)
    fcol = 2 if rev else 1

    def col(j):
        return pl.BlockSpec((1, CHUNK, A_WIDTH), lambda b, c: (b, blk(c), j))

    return pl.pallas_call(
        functools.partial(_hgrn2_kernel, rev=rev),
        grid=(nb, n_chunks),
        in_specs=[col(0), col(fcol), col(3), pl.BlockSpec((3, A_WIDTH), lambda b, c: (0, 0))],
        out_specs=pl.BlockSpec((1, CHUNK, A_WIDTH), lambda b, c: (b, blk(c), 0)),
        out_shape=jax.ShapeDtypeStruct((nb, length, A_WIDTH), F32),
        scratch_shapes=[
            pltpu.VMEM((A_HEADS, CHUNK + 2 * PAD, A_DK), F32),
            pltpu.VMEM((A_HEADS, CHUNK + 2 * PAD, A_DK), F32),
            pltpu.VMEM((A_HEADS, CHUNK + 2 * PAD, A_DK), F32),
            pltpu.VMEM((A_HEADS, A_DK, A_DK), F32),
        ],
        compiler_params=_cparams(("arbitrary", "arbitrary")),
        name="hgrn2_bwd" if rev else "hgrn2_fwd",
    )(u_ab, u_ab, u_ab, lb_tab)


def _mlstm_kernel(qf_ref, kf_ref, vf_ref, gf_ref, qb_ref, kb_ref, vb_ref, gb_ref, bias_ref,
                  hf_ref, hb_ref, c_s, n_s, m_s):
    @pl.when(pl.program_id(1) == 0)
    def _():
        c_s[...] = jnp.zeros_like(c_s)
        n_s[...] = jnp.zeros_like(n_s)
        m_s[...] = jnp.zeros_like(m_s)

    nh = C_HEADS
    eye = (lax.broadcasted_iota(jnp.int32, (nh, nh), 0)
           == lax.broadcasted_iota(jnp.int32, (nh, nh), 1)).astype(F32)
    heads = lambda ref: [ref[0, :, h * C_DK:(h + 1) * C_DK] for h in range(nh)]
    bcols, igcols, arows, masks = [], [], [], []
    for d, g_ref in enumerate((gf_ref, gb_ref)):
        gates = g_ref[0] + bias_ref[...]
        ig_all = gates[:, 64 + 8 * d:72 + 8 * d]
        lf_all = _log_sigmoid(gates[:, 80 + 8 * d:88 + 8 * d])
        mask = _tri(bool(d))
        b_all = _dot_hi(mask.astype(F32), lf_all)
        arows.append(_dot_nt_hi(eye, ig_all - b_all))
        bcols += [b_all[:, h:h + 1] for h in range(nh)]
        igcols += [ig_all[:, h:h + 1] for h in range(nh)]
        masks.append(jnp.broadcast_to(mask[None], (nh, CHUNK, CHUNK)))
    q = jnp.stack(heads(qf_ref) + heads(qb_ref)) * (C_DK ** -0.5)
    k = jnp.stack(heads(kf_ref) + heads(kb_ref))
    v = jnp.stack(heads(vf_ref) + heads(vb_ref))
    bcol = jnp.stack(bcols)
    igcol = jnp.stack(igcols)
    arow = jnp.concatenate(arows, axis=0)[:, None, :]
    mask = jnp.concatenate(masks, axis=0)
    dmat = jnp.where(mask, bcol + arow, NEG_INF)
    m0 = m_s[...]
    m_inter = bcol + m0
    m_i = jnp.maximum(m_inter, jnp.max(dmat, axis=2, keepdims=True))
    w_intra = jnp.exp(dmat - m_i)
    w_inter = jnp.exp(m_inter - m_i)
    s = _bdot(q, k, 2, 2) * w_intra
    cmat = c_s[...]
    nvec = n_s[...]
    num = _bdot(s, v, 2, 1) + w_inter * _bdot(q, cmat, 2, 1)
    den = jnp.sum(s, axis=2, keepdims=True) + w_inter * jnp.sum(q * nvec, axis=2, keepdims=True)
    out = num / jnp.maximum(jnp.abs(den), jnp.exp(-m_i))
    for h in range(nh):
        hf_ref[0, :, h * C_DK:(h + 1) * C_DK] = out[h]
        hb_ref[0, :, h * C_DK:(h + 1) * C_DK] = out[nh + h]
    b_last = jnp.concatenate([bcol[:nh, CHUNK - 1:CHUNK, :], bcol[nh:, 0:1, :]], axis=0)
    log_wj = b_last - bcol + igcol
    m_new = jnp.maximum(b_last + m0, jnp.max(log_wj, axis=1, keepdims=True))
    w0 = jnp.exp(b_last + m0 - m_new)
    wk = jnp.exp(log_wj - m_new) * k
    c_s[...] = w0 * cmat + _bdot(wk, v, 1, 1)
    n_s[...] = w0 * nvec + jnp.sum(wk, axis=1, keepdims=True)
    m_s[...] = m_new


def mlstm_scan(u_c, u_s, gate_bias, n_ctx):
    nb, length, _ = u_c.shape
    n_chunks = length // CHUNK
    n_ctx_chunks = n_ctx // CHUNK
    blks = [functools.partial(_chunk_block, n_ctx_chunks=n_ctx_chunks,
                              n_lat_chunks=n_chunks - n_ctx_chunks, rev=rev) for rev in (False, True)]
    in_specs = []
    for blk in blks:
        in_specs += [pl.BlockSpec((1, CHUNK, C_WIDTH), lambda b, c, j=j, blk=blk: (b, blk(c), j))
                     for j in range(3)]
        in_specs.append(pl.BlockSpec((1, CHUNK, U_SMALL), lambda b, c, blk=blk: (b, blk(c), 0)))
    in_specs.append(pl.BlockSpec((1, U_SMALL), lambda b, c: (0, 0)))
    return pl.pallas_call(
        _mlstm_kernel,
        grid=(nb, n_chunks),
        in_specs=in_specs,
        out_specs=[pl.BlockSpec((1, CHUNK, C_WIDTH), lambda b, c, blk=blk: (b, blk(c), 0)) for blk in blks],
        out_shape=[jax.ShapeDtypeStruct((nb, length, C_WIDTH), F32)] * 2,
        scratch_shapes=[
            pltpu.VMEM((2 * C_HEADS, C_DK, C_DK), F32),
            pltpu.VMEM((2 * C_HEADS, 1, C_DK), F32),
            pltpu.VMEM((2 * C_HEADS, 1, 1), F32),
        ],
        compiler_params=_cparams(("arbitrary", "arbitrary")),
        name="mlstm_scan",
    )(u_c, u_c, u_c, u_s, u_c, u_c, u_c, u_s, gate_bias)


CONV_TM = 256
CONV_TN = 1024
HALO = 8


def _conv_kernel(prev_ref, cur_ref, next_ref, w_ref, b_ref, o_ref, win_s, *, tiles_ctx, tiles_all):
    i = pl.program_id(1)
    seg_start = (i == 0) | (i == tiles_ctx)
    seg_end = (i == tiles_ctx - 1) | (i == tiles_all - 1)
    win_s[0:HALO, :] = jnp.where(seg_start, 0.0, prev_ref[0])
    win_s[HALO:HALO + CONV_TM, :] = cur_ref[0]
    win_s[HALO + CONV_TM:2 * HALO + CONV_TM, :] = jnp.where(seg_end, 0.0, next_ref[0])
    acc = b_ref[...] + jnp.zeros((CONV_TM, CONV_TN), F32)
    for t in range(B_CONV):
        o = HALO + t - B_CONV // 2
        acc = acc + w_ref[t:t + 1, :] * win_s[o:o + CONV_TM, :]
    o_ref[0] = _silu(acc)


def ssd_conv(u_ab, conv_w, conv_b, n_ctx):
    nb, length, _ = u_ab.shape
    tiles_all = length // CONV_TM
    tiles_ctx = n_ctx // CONV_TM
    col0 = (U_AB - B_CONV_CH) // CONV_TN
    per = CONV_TM // HALO
    n_halo = length // HALO
    return pl.pallas_call(
        functools.partial(_conv_kernel, tiles_ctx=tiles_ctx, tiles_all=tiles_all),
        grid=(nb, tiles_all, B_CONV_CH // CONV_TN),
        in_specs=[
            pl.BlockSpec((1, HALO, CONV_TN), lambda b, i, j: (b, jnp.maximum(i * per - 1, 0), col0 + j)),
            pl.BlockSpec((1, CONV_TM, CONV_TN), lambda b, i, j: (b, i, col0 + j)),
            pl.BlockSpec((1, HALO, CONV_TN),
                         lambda b, i, j: (b, jnp.minimum((i + 1) * per, n_halo - 1), col0 + j)),
            pl.BlockSpec((B_CONV, CONV_TN), lambda b, i, j: (0, j)),
            pl.BlockSpec((1, CONV_TN), lambda b, i, j: (0, j)),
        ],
        out_specs=pl.BlockSpec((1, CONV_TM, CONV_TN), lambda b, i, j: (b, i, j)),
        out_shape=jax.ShapeDtypeStruct((nb, length, B_CONV_CH), F32),
        scratch_shapes=[pltpu.VMEM((CONV_TM + 2 * HALO, CONV_TN), F32)],
        compiler_params=_cparams(("arbitrary", "arbitrary", "arbitrary")),
        name="ssd_conv",
    )(u_ab, u_ab, u_ab, conv_w, conv_b.reshape(1, B_CONV_CH))


def _ssd_kernel(xc_ref, us_ref, bias_ref, an_ref, e_ref, y_ref, s_s, *, rev):
    c = pl.program_id(1)

    @pl.when(c == 0)
    def _():
        s_s[...] = jnp.zeros_like(s_s)

    d = 1 if rev else 0
    ng = B_GROUPS
    gw = B_RPG * B_HEADDIM
    mask = _tri(rev)
    last = 0 if rev else CHUNK - 1
    dt_n = _softplus(us_ref[0] + bias_ref[...])[:, 32 * d:32 * d + B_HEADS]
    cum_n = _dot_hi(mask.astype(F32), dt_n * an_ref[...])
    eye = (lax.broadcasted_iota(jnp.int32, (B_HEADS, B_HEADS), 0)
           == lax.broadcasted_iota(jnp.int32, (B_HEADS, B_HEADS), 1)).astype(F32)
    cum_t = _dot_nt_hi(eye, cum_n)
    decay_in = jnp.exp(cum_n)
    decay_out = jnp.exp(cum_n[last:last + 1, :] - cum_n)
    spread = _dot(jnp.concatenate([dt_n, decay_in, decay_out], axis=0), e_ref[...])
    groups = lambda a: jnp.stack([a[:, g * gw:(g + 1) * gw] for g in range(ng)])
    dt_x = groups(spread[0:CHUNK])
    in_x = groups(spread[CHUNK:2 * CHUNK])
    out_x = groups(spread[2 * CHUNK:3 * CHUNK])
    xdt = jnp.stack([xc_ref[0, :, g * gw:(g + 1) * gw] for g in range(ng)]) * dt_x
    bm = jnp.stack([xc_ref[0, :, B_WIDTH + g * B_STATE:B_WIDTH + (g + 1) * B_STATE]
                    for g in range(ng)])
    c0 = B_WIDTH + B_GROUPS * B_STATE
    cm = jnp.stack([xc_ref[0, :, c0 + g * B_STATE:c0 + (g + 1) * B_STATE] for g in range(ng)])
    state = s_s[...]
    cb = _bdot(cm, bm, 2, 2)
    y = in_x * _bdot(cm, state, 2, 1)
    head_of_lane = lax.broadcasted_iota(jnp.int32, (1, 1, gw), 2) // B_HEADDIM
    for r in range(B_RPG):
        ccol = jnp.stack([cum_n[:, g * B_RPG + r:g * B_RPG + r + 1] for g in range(ng)])
        crow = jnp.stack([cum_t[g * B_RPG + r:g * B_RPG + r + 1, :] for g in range(ng)])
        lmat = cb * jnp.exp(jnp.where(mask[None], ccol - crow, NEG_INF))
        y = y + _bdot(lmat, jnp.where(head_of_lane == r, xdt, 0.0), 2, 1)
    s_s[...] = in_x[:, last:last + 1, :] * state + _bdot(bm, xdt * out_x, 1, 1)
    for g in range(ng):
        y_ref[0, :, g * gw:(g + 1) * gw] = y[g]


def ssd_scan(xc, u_s, dt_bias, a_n, expand, n_ctx, rev):
    nb, length, _ = xc.shape
    n_chunks = length // CHUNK
    n_ctx_chunks = n_ctx // CHUNK
    blk = functools.partial(_chunk_block, n_ctx_chunks=n_ctx_chunks,
                            n_lat_chunks=n_chunks - n_ctx_chunks, rev=rev)
    const = lambda shape: pl.BlockSpec(shape, lambda b, c: (0, 0))
    return pl.pallas_call(
        functools.partial(_ssd_kernel, rev=rev),
        grid=(nb, n_chunks),
        in_specs=[pl.BlockSpec((1, CHUNK, B_CONV_CH), lambda b, c: (b, blk(c), 0)),
                  pl.BlockSpec((1, CHUNK, U_SMALL), lambda b, c: (b, blk(c), 0)),
                  const((1, U_SMALL)), const((1, B_HEADS)), const((B_HEADS, B_WIDTH))],
        out_specs=pl.BlockSpec((1, CHUNK, B_WIDTH), lambda b, c: (b, blk(c), 0)),
        out_shape=jax.ShapeDtypeStruct((nb, length, B_WIDTH), F32),
        scratch_shapes=[pltpu.VMEM((B_GROUPS, B_STATE, B_RPG * B_HEADDIM), F32)],
        compiler_params=_cparams(("arbitrary", "arbitrary")),
        name="ssd_bwd" if rev else "ssd_fwd",
    )(xc, u_s, dt_bias, a_n, expand)


def _group_rmsnorm(y, width):
    out = []
    for g in range(y.shape[1] // width):
        yg = y[:, g * width:(g + 1) * width]
        out.append(yg * lax.rsqrt(jnp.mean(yg * yg, axis=1, keepdims=True) + EPS))
    return jnp.concatenate(out, axis=1)


def _mix_out_kernel(af_ref, ab_ref, ag_ref, bf_ref, bb_ref, bx_ref, bz0_ref, bz1_ref,
                    cf_ref, cb_ref, co_ref, wa_ref, wb_ref, wc_ref, dsk_ref, o_ref):
    ya = _group_rmsnorm(af_ref[...] + ab_ref[...], A_DK) * wa_ref[...] * _silu(ag_ref[...])
    o_ref[:, 0:A_WIDTH] = ya.astype(o_ref.dtype)
    z = jnp.concatenate([bz0_ref[...], bz1_ref[...]], axis=1)
    yb = (bf_ref[...] + bb_ref[...] + dsk_ref[...] * bx_ref[...]) * _silu(z)
    yb = _group_rmsnorm(yb, B_WIDTH // B_GROUPS) * wb_ref[...]
    o_ref[:, A_WIDTH:A_WIDTH + B_WIDTH] = yb.astype(o_ref.dtype)
    yc = (cf_ref[...] + cb_ref[...]) * jax.nn.sigmoid(co_ref[...])
    yc = _group_rmsnorm(yc, C_DK) * wc_ref[...]
    o_ref[:, A_WIDTH + B_WIDTH:] = yc.astype(o_ref.dtype)


def mix_out(ya_f, ya_b, yb_f, yb_b, yc_f, yc_b, u_ab, xc, u_c, wa, wb, wc, dskip_x, tm=128):
    t = ya_f.shape[0]
    row = lambda w, j=0: pl.BlockSpec((tm, w), lambda i: (i, j))
    vec = lambda w: pl.BlockSpec((1, w), lambda i: (0, 0))
    return pl.pallas_call(
        _mix_out_kernel,
        grid=(t // tm,),
        in_specs=[row(A_WIDTH), row(A_WIDTH), row(A_WIDTH, 4),
                  row(B_WIDTH), row(B_WIDTH), row(B_WIDTH), row(1024, 5), row(1024, 6),
                  row(C_WIDTH), row(C_WIDTH), row(C_WIDTH, 3),
                  vec(A_WIDTH), vec(B_WIDTH), vec(C_WIDTH), vec(B_WIDTH)],
        out_specs=pl.BlockSpec((tm, D_MODEL), lambda i: (i, 0)),
        out_shape=jax.ShapeDtypeStruct((t, D_MODEL), BF16),
        compiler_params=_cparams(("arbitrary",)),
        name="mix_out",
    )(ya_f, ya_b, u_ab, yb_f, yb_b, xc, u_ab, u_ab, yc_f, yc_b, u_c, wa, wb, wc, dskip_x)


def _first_argmax(vals, lane):
    m = jnp.max(vals, axis=1, keepdims=True)
    idx = jnp.min(jnp.where(vals == m, lane, 1 << 20), axis=1, keepdims=True)
    return m, idx


def _route_kernel(lg_ref, bias_ref, idx_ref, w_ref, rank_ref, cnt_ref, run_s):
    @pl.when(pl.program_id(0) == 0)
    def _():
        run_s[...] = jnp.zeros_like(run_s)

    logits = lg_ref[...]
    tm = logits.shape[0]
    lane = lax.broadcasted_iota(jnp.int32, (tm, 128), 1)
    is_expert = lane < N_EXPERTS
    scores = jax.nn.sigmoid(logits)
    biased = jnp.where(is_expert, scores + bias_ref[...], NEG_INF)
    per_group = N_EXPERTS // N_EXPERT_GROUPS
    grp = lane // per_group
    gscore = jnp.full((tm, 128), NEG_INF, F32)
    for g in range(N_EXPERT_GROUPS):
        vals = jnp.where(grp == g, biased, NEG_INF)
        m1, i1 = _first_argmax(vals, lane)
        m2 = jnp.max(jnp.where(lane == i1, NEG_INF, vals), axis=1, keepdims=True)
        gscore = jnp.where(lane == g, m1 + m2, gscore)
    allowed = jnp.zeros((tm, 128), jnp.bool_)
    for _ in range(TOPK_GROUPS):
        _, gi = _first_argmax(gscore, lane)
        allowed = allowed | (grp == gi)
        gscore = jnp.where(lane == gi, NEG_INF, gscore)
    masked = jnp.where(allowed & is_expert, biased, NEG_INF)
    idx_out = jnp.zeros((tm, 128), jnp.int32)
    w_out = jnp.zeros((tm, 128), F32)
    picks = []
    chosen = jnp.zeros((tm, 128), F32)
    for kk in range(TOP_K):
        _, ei = _first_argmax(masked, lane)
        sel = lane == ei
        picks.append(sel)
        chosen = chosen + sel.astype(F32)
        wk = jnp.sum(jnp.where(sel, scores, 0.0), axis=1, keepdims=True)
        idx_out = jnp.where(lane == kk, ei, idx_out)
        w_out = jnp.where(lane == kk, wk, w_out)
        masked = jnp.where(sel, NEG_INF, masked)
    w_out = w_out / jnp.sum(w_out, axis=1, keepdims=True) * ROUTED_SCALE
    idx_ref[...] = idx_out
    w_ref[...] = w_out
    strict = (lax.broadcasted_iota(jnp.int32, (tm, tm), 1)
              < lax.broadcasted_iota(jnp.int32, (tm, tm), 0)).astype(F32)
    before = _dot(strict, chosen) + run_s[...]
    rank_out = jnp.zeros((tm, 128), jnp.int32)
    for kk in range(TOP_K):
        rk = jnp.sum(jnp.where(picks[kk], before, 0.0), axis=1, keepdims=True)
        rank_out = jnp.where(lane == kk, rk.astype(jnp.int32), rank_out)
    rank_ref[...] = rank_out
    total = run_s[...] + jnp.sum(chosen, axis=0, keepdims=True)
    run_s[...] = total
    cnt_ref[...] = total.astype(jnp.int32)


def route(logits, router_bias, tm=1024):
    t = logits.shape[0]
    bias = jnp.concatenate([router_bias.astype(F32), jnp.zeros((128 - N_EXPERTS,), F32)])[None]
    row = pl.BlockSpec((tm, 128), lambda i: (i, 0))
    one = pl.BlockSpec((1, 128), lambda i: (0, 0))
    return pl.pallas_call(
        _route_kernel,
        grid=(t // tm,),
        in_specs=[row, one],
        out_specs=[row, row, row, one],
        out_shape=[jax.ShapeDtypeStruct((t, 128), jnp.int32), jax.ShapeDtypeStruct((t, 128), F32),
                   jax.ShapeDtypeStruct((t, 128), jnp.int32), jax.ShapeDtypeStruct((1, 128), jnp.int32)],
        scratch_shapes=[pltpu.VMEM((1, 128), F32)],
        compiler_params=_cparams(("arbitrary",)),
        name="route",
    )(logits, bias)


EXP_TM = 256
SCT_TM = 256


def _scatter_kernel(lt_ref, pos_ref, hp_ref, xs_hbm, zero_s, sem, zsem):
    i = pl.program_id(0)

    @pl.when(i == 0)
    def _():
        zero_s[...] = jnp.zeros_like(zero_s)
        for e in range(2 * N_EXPERTS):
            @pl.when(lt_ref[e] >= 0)
            def _():
                pltpu.make_async_copy(zero_s, xs_hbm.at[pl.ds(lt_ref[e] * EXP_TM, EXP_TM), :], zsem).start()
        for e in range(2 * N_EXPERTS):
            @pl.when(lt_ref[e] >= 0)
            def _():
                pltpu.make_async_copy(zero_s, xs_hbm.at[pl.ds(lt_ref[e] * EXP_TM, EXP_TM), :], zsem).wait()

    def body(r, carry):
        for kk in range(TOP_K):
            p = pos_ref[0, r * TOP_K + kk]
            pltpu.make_async_copy(hp_ref.at[pl.ds(r, 1), :], xs_hbm.at[pl.ds(p, 1), :], sem).start(
                priority=kk % 2)
        return carry
    lax.fori_loop(0, SCT_TM, body, 0)
    for kk in range(TOP_K):
        pltpu.make_async_copy(hp_ref, xs_hbm.at[pl.ds(0, SCT_TM), :], sem).wait()


def scatter_rows(hp, pos, clear_tiles, n_rows):
    t, half = hp.shape
    n = t // SCT_TM
    grid_spec = pltpu.PrefetchScalarGridSpec(
        num_scalar_prefetch=1,
        grid=(n,),
        in_specs=[
            pl.BlockSpec((None, 1, SCT_TM * TOP_K), lambda i, lt: (i, 0, 0), memory_space=pltpu.SMEM),
            pl.BlockSpec((SCT_TM, half), lambda i, lt: (i, 0)),
        ],
        out_specs=pl.BlockSpec(memory_space=pl.ANY),
        scratch_shapes=[
            pltpu.VMEM((EXP_TM, half), jnp.uint32),
            pltpu.SemaphoreType.DMA(()),
            pltpu.SemaphoreType.DMA(()),
        ],
    )
    return pl.pallas_call(
        _scatter_kernel,
        grid_spec=grid_spec,
        out_shape=jax.ShapeDtypeStruct((n_rows, half), jnp.uint32),
        compiler_params=_cparams(("arbitrary",)),
        name="scatter_rows",
    )(clear_tiles, pos.reshape(n, 1, SCT_TM * TOP_K), hp)


def _expert_weight_copies(e, layer, wg_hbm, wu_hbm, wd_hbm, wg_buf, wu_buf, wd_buf, sem, slot):
    return [pltpu.make_async_copy(wg_hbm.at[layer, e], wg_buf.at[slot], sem.at[slot]),
            pltpu.make_async_copy(wu_hbm.at[layer, e], wu_buf.at[slot], sem.at[slot]),
            pltpu.make_async_copy(wd_hbm.at[layer, e], wd_buf.at[slot], sem.at[slot])]


def _expert_kernel(te_ref, nu_ref, nx_ref, x_ref, wg_hbm, wu_hbm, wd_hbm, y_ref,
                   wg_buf, wu_buf, wd_buf, sem, wgu_s, wd_s, grp_s, *, layer):
    i = pl.program_id(0)
    n_used = nu_ref[0]
    half = x_ref.shape[1]
    copies = functools.partial(_expert_weight_copies, layer=layer, wg_hbm=wg_hbm, wu_hbm=wu_hbm,
                               wd_hbm=wd_hbm, wg_buf=wg_buf, wu_buf=wu_buf, wd_buf=wd_buf, sem=sem)

    @pl.when(i == 0)
    def _():
        grp_s[0] = 0
        for cp in copies(te_ref[0], slot=0):
            cp.start()

    new_expert = (i == 0) | (te_ref[i] != te_ref[jnp.maximum(i - 1, 0)])

    @pl.when(new_expert & (i < n_used))
    def _():
        slot = grp_s[0] % 2
        for cp in copies(te_ref[i], slot=slot):
            cp.wait()
        wgu_s[:, 0:D_EXPERT] = wg_buf[slot].astype(BF16)
        wgu_s[:, D_EXPERT:2 * D_EXPERT] = wu_buf[slot].astype(BF16)
        wd_s[...] = wd_buf[slot].astype(BF16)

        @pl.when(nx_ref[i] >= 0)
        def _():
            for cp in copies(nx_ref[i], slot=1 - slot):
                cp.start()
        grp_s[0] = grp_s[0] + 1

    @pl.when(i < n_used)
    def _():
        x_hi, x_lo = _unpack_bf16_pair(x_ref[...])
        h = (jnp.dot(x_hi.astype(BF16), wgu_s[0:half, :], preferred_element_type=F32)
             + jnp.dot(x_lo.astype(BF16), wgu_s[half:2 * half, :], preferred_element_type=F32))
        act = _silu(h[:, 0:D_EXPERT]) * h[:, D_EXPERT:2 * D_EXPERT]
        y = jnp.dot(act.astype(BF16), wd_s[...], preferred_element_type=F32)
        y_ref[...] = _pack_bf16_pair(y[:, 0:half], y[:, half:2 * half])

    @pl.when(i >= n_used)
    def _():
        y_ref[...] = jnp.zeros_like(y_ref)


def routed_experts(x_sorted, tile_expert, n_used, next_expert, w_gate, w_up, w_down, layer):
    n_rows, half = x_sorted.shape
    n_tiles = n_rows // EXP_TM
    d = 2 * half
    grid_spec = pltpu.PrefetchScalarGridSpec(
        num_scalar_prefetch=3,
        grid=(n_tiles,),
        in_specs=[
            pl.BlockSpec((EXP_TM, half), lambda i, te, nu, nx: (jnp.minimum(i, nu[0] - 1), 0)),
            pl.BlockSpec(memory_space=pl.ANY),
            pl.BlockSpec(memory_space=pl.ANY),
            pl.BlockSpec(memory_space=pl.ANY),
        ],
        out_specs=pl.BlockSpec((EXP_TM, half), lambda i, te, nu, nx: (i, 0)),
        scratch_shapes=[
            pltpu.VMEM((2, d, D_EXPERT), F32),
            pltpu.VMEM((2, d, D_EXPERT), F32),
            pltpu.VMEM((2, D_EXPERT, d), F32),
            pltpu.SemaphoreType.DMA((2,)),
            pltpu.VMEM((d, 2 * D_EXPERT), BF16),
            pltpu.VMEM((D_EXPERT, d), BF16),
            pltpu.SMEM((1,), jnp.int32),
        ],
    )
    return pl.pallas_call(
        functools.partial(_expert_kernel, layer=layer),
        grid_spec=grid_spec,
        out_shape=jax.ShapeDtypeStruct((n_rows, half), jnp.uint32),
        compiler_params=_cparams(("arbitrary",)),
        name="routed_experts",
    )(tile_expert, n_used, next_expert, x_sorted, w_gate, w_up, w_down)


CMB_TM = 128


def _start_combine_gather(pos_ref, y_hbm, dst, sem):
    def body(r, carry):
        for kk in range(TOP_K):
            p = pos_ref[0, r * TOP_K + kk]
            pltpu.make_async_copy(y_hbm.at[pl.ds(p, 1), :], dst.at[kk, pl.ds(r, 1), :], sem).start(
                priority=kk % 2)
        return carry
    lax.fori_loop(0, CMB_TM, body, 0)


def _combine_kernel(pos_ref, posn_ref, y_hbm, w_ref, hp_ref, x_ref, gate_ref, sgu_ref, sd_ref,
                    norm_ref, o_ref, ybuf, sem, *, final_norm):
    i = pl.program_id(0)
    n = pl.num_programs(0)
    slot = i % 2
    half = hp_ref.shape[1]

    @pl.when(i == 0)
    def _():
        _start_combine_gather(pos_ref, y_hbm, ybuf.at[0], sem.at[0])

    @pl.when(i + 1 < n)
    def _():
        _start_combine_gather(posn_ref, y_hbm, ybuf.at[1 - slot], sem.at[1 - slot])

    x_hi, x_lo = _unpack_bf16_pair(hp_ref[...])
    h = (jnp.dot(x_hi.astype(BF16), sgu_ref[0:half, :], preferred_element_type=F32)
         + jnp.dot(x_lo.astype(BF16), sgu_ref[half:2 * half, :], preferred_element_type=F32))
    act = _silu(h[:, 0:D_SHARED]) * h[:, D_SHARED:2 * D_SHARED]
    shared = jnp.dot(act.astype(BF16), sd_ref[...], preferred_element_type=F32)

    for kk in range(TOP_K):
        pltpu.make_async_copy(y_hbm.at[pl.ds(0, CMB_TM), :], ybuf.at[slot, kk], sem.at[slot]).wait()
    acc_hi = shared[:, 0:half]
    acc_lo = shared[:, half:2 * half]
    w = w_ref[...]
    for kk in range(TOP_K):
        y_hi, y_lo = _unpack_bf16_pair(ybuf[slot, kk])
        wk = w[:, kk:kk + 1]
        acc_hi = acc_hi + wk * y_hi
        acc_lo = acc_lo + wk * y_lo
    g = gate_ref[...]
    out_hi = x_ref[:, 0:half] + g[:, 0:half] * acc_hi
    out_lo = x_ref[:, half:2 * half] + g[:, half:2 * half] * acc_lo
    if final_norm:
        ms = (jnp.sum(out_hi * out_hi, axis=1, keepdims=True)
              + jnp.sum(out_lo * out_lo, axis=1, keepdims=True)) / (2 * half)
        inv = lax.rsqrt(ms + EPS)
        out_hi = out_hi * inv * norm_ref[:, 0:half]
        out_lo = out_lo * inv * norm_ref[:, half:2 * half]
    o_ref[:, 0:half] = out_hi
    o_ref[:, half:2 * half] = out_lo


def moe_combine(pos, y_sorted, w, hp, x, gate, seg_of_tile, sh_gate_up, sh_down, norm_w, final_norm):
    t, d = x.shape
    half = d // 2
    n = t // CMB_TM
    return pl.pallas_call(
        functools.partial(_combine_kernel, final_norm=final_norm),
        grid=(n,),
        in_specs=[
            pl.BlockSpec((None, 1, CMB_TM * TOP_K), lambda i: (i, 0, 0), memory_space=pltpu.SMEM),
            pl.BlockSpec((None, 1, CMB_TM * TOP_K), lambda i: (jnp.minimum(i + 1, n - 1), 0, 0),
                         memory_space=pltpu.SMEM),
            pl.BlockSpec(memory_space=pl.ANY),
            pl.BlockSpec((CMB_TM, 128), lambda i: (i, 0)),
            pl.BlockSpec((CMB_TM, half), lambda i: (i, 0)),
            pl.BlockSpec((CMB_TM, d), lambda i: (i, 0)),
            pl.BlockSpec((None, 1, d), lambda i: (seg_of_tile(i), 0, 0)),
            pl.BlockSpec((d, 2 * D_SHARED), lambda i: (0, 0)),
            pl.BlockSpec((D_SHARED, d), lambda i: (0, 0)),
            pl.BlockSpec((1, d), lambda i: (0, 0)),
        ],
        out_specs=pl.BlockSpec((CMB_TM, d), lambda i: (i, 0)),
        out_shape=jax.ShapeDtypeStruct((t, d), F32),
        scratch_shapes=[
            pltpu.VMEM((2, TOP_K, CMB_TM, half), jnp.uint32),
            pltpu.SemaphoreType.DMA((2,)),
        ],
        compiler_params=_cparams(("arbitrary",)),
        name="moe_combine",
    )(pos, pos, y_sorted, w, hp, x, gate, sh_gate_up, sh_down, norm_w)


def moe_plan(eidx, rank, counts, n_tiles):
    tiles_per = (counts + EXP_TM - 1) // EXP_TM
    tile_end = jnp.cumsum(tiles_per)
    base = (tile_end - tiles_per) * EXP_TM
    experts = jnp.arange(N_EXPERTS, dtype=jnp.int32)
    pos = rank + jnp.sum(jnp.where(eidx[..., None] == experts, base, 0), axis=-1)
    tile_expert = jnp.sum((tile_end[None, :] <= jnp.arange(n_tiles, dtype=jnp.int32)[:, None])
                          .astype(jnp.int32), axis=1)
    tile_expert = jnp.minimum(tile_expert, N_EXPERTS - 1)
    last_tile = jnp.where(tiles_per > 0, tile_end - 1, -1)
    unused = tile_end[-1] + experts
    clear_tiles = jnp.concatenate([last_tile, jnp.where(unused < n_tiles, unused, -1)])
    later = (experts[None, :] > experts[:, None]) & (tiles_per[None, :] > 0)
    nxt = jnp.min(jnp.where(later, experts[None, :], N_EXPERTS), axis=1)
    nxt = jnp.where(nxt < N_EXPERTS, nxt, -1)
    next_expert = jnp.sum(jnp.where(tile_expert[:, None] == experts[None, :], nxt[None, :], 0), axis=1)
    return tile_expert, tile_end[-1:], clear_tiles, next_expert, pos


def lb_table(lb):
    lb = lb.astype(F32)
    return jnp.stack([jnp.log(lb), jnp.log1p(-lb), 1.0 - lb])


def mlstm_gate_bias(i_bias, f_bias):
    z = jnp.zeros((64,), F32)
    return jnp.concatenate([z, i_bias.astype(F32).reshape(-1), f_bias.astype(F32).reshape(-1),
                            jnp.zeros((32,), F32)])[None]


def ssd_consts(dt_bias, a_log):
    bias = jnp.concatenate([dt_bias.astype(F32).reshape(-1), jnp.zeros((64,), F32)])[None]
    a = -jnp.exp(a_log.astype(F32))
    expand = (jnp.arange(B_HEADS)[:, None] == jnp.arange(B_WIDTH)[None, :] // B_HEADDIM).astype(BF16)
    return [(bias, a[d][None], expand) for d in range(2)]


def _mod_tables(mod, nb):
    d = mod.shape[1] // 6
    parts = mod.reshape(mod.shape[0], 6, d)
    lat = parts[:nb]
    ctx = jnp.broadcast_to(parts[nb][None], (nb, 6, d))
    tab = jnp.stack([ctx, lat], axis=1).reshape(2 * nb, 6, d)
    return jnp.transpose(tab, (1, 0, 2))[:, :, None, :]


def _seg_fn(rows_per_batch, ctx_rows, tm):
    tiles_per_batch = rows_per_batch // tm
    ctx_tiles = ctx_rows // tm

    def seg(i):
        return 2 * (i // tiles_per_batch) + jnp.where(i % tiles_per_batch >= ctx_tiles, 1, 0)
    return seg


def _to_scan_order(a, rows):
    nb, _, d = a.shape
    return a.reshape(nb, rows, GRID_W, d).transpose(0, 2, 1, 3).reshape(nb, rows * GRID_W, d)


def _from_scan_order(a, rows):
    nb, _, d = a.shape
    return a.reshape(nb, GRID_W, rows, d).transpose(0, 2, 1, 3).reshape(nb, rows * GRID_W, d)


def _moe_block(x2, tab, seg_mod, seg_cmb, out_norm_w, final_norm, layer, norm_w, router_w, router_bias,
               w_gate, w_up, w_down, sh_gate, sh_up, sh_down):
    t, d = x2.shape
    rw = jnp.concatenate([router_w.astype(F32), jnp.zeros((d, 128 - N_EXPERTS), F32)], axis=1)
    rw_hi = rw.astype(BF16)
    rw = jnp.concatenate([rw_hi, (rw - rw_hi.astype(F32)).astype(BF16)], axis=1)
    hp, logits = modulate(x2, norm_w, tab[3], tab[4], seg_mod, 256, router_w=rw)
    eidx, ew, rank, counts = route(logits, router_bias)
    n_tiles = t * TOP_K // EXP_TM + N_EXPERTS
    tile_expert, n_used, clear_tiles, next_expert, pos = moe_plan(
        eidx[:, :TOP_K], rank[:, :TOP_K], counts[0, :N_EXPERTS], n_tiles)
    x_sorted = scatter_rows(hp, pos, clear_tiles, n_tiles * EXP_TM)
    y_sorted = routed_experts(x_sorted, tile_expert, n_used, next_expert, w_gate, w_up, w_down, layer)
    sgu = jnp.concatenate([sh_gate, sh_up], axis=1).astype(BF16)
    return moe_combine(pos.reshape(t // CMB_TM, 1, CMB_TM * TOP_K), y_sorted, ew, hp, x2, tab[5],
                       seg_cmb, sgu, sh_down.astype(BF16), out_norm_w.reshape(1, d), final_norm)


def kernel(x, c, ctx, c_ctx, ada_w, ada_b, norm_mix, norm_ffn, norm_final, w_in, w_out, hgrn_lb,
           hgrn_norm, ssm_conv_w, ssm_conv_b, ssm_dt_bias, ssm_a_log, ssm_d, ssm_norm, mlstm_i_bias,
           mlstm_f_bias, mlstm_norm, router_w, router_bias, moe_w_gate, moe_w_up, moe_w_down,
           shared_w_gate, shared_w_up, shared_w_down):
    nb, seq, d = x.shape
    n_ctx = ctx.shape[1]
    length = n_ctx + seq
    rows = seq // GRID_W
    depth = ada_w.shape[0]
    t_all = nb * length
    t_lat = nb * seq

    lb_all = jnp.cumsum(jax.nn.softmax(hgrn_lb.astype(F32), axis=0), axis=0)
    lb_all = lb_all - lb_all[0]
    cond = jnp.concatenate([c, c_ctx[None], jnp.zeros((8 - nb - 1, d), F32)], axis=0)
    mod = ada_modulation(cond, ada_w, ada_b)

    xs = jnp.concatenate([ctx, x], axis=1)
    w_in_t = jnp.swapaxes(w_in, 1, 2)
    for l in range(depth):
        last = l == depth - 1
        tab = _mod_tables(mod[l], nb)
        h = modulate(xs.reshape(t_all, d), norm_mix[l], tab[0], tab[1], _seg_fn(length, n_ctx, 256), 256)
        if l % 2 == 1:
            h3 = h.reshape(nb, length, d)
            h = jnp.concatenate([h3[:, :n_ctx], _to_scan_order(h3[:, n_ctx:], rows)], axis=1)
            h = h.reshape(t_all, d)
        w_c = w_in_t[l:l + 1, U_C0:U_G0]
        w_small = jnp.concatenate([w_in_t[l:l + 1, U_DT0:U_C0], w_in_t[l:l + 1, U_G0:],
                                   jnp.zeros((1, U_SMALL - 96, d), F32)], axis=1)
        u_ab = matmul_nt(h, w_in_t, l, tm=1024, tn=512, n_tiles=U_AB // 512, name="in_proj_ab")
        u_c = matmul_nt(h, w_c, 0, tm=1024, tn=512, name="in_proj_c")
        u_s = matmul_nt(h, w_small, 0, tm=1024, tn=U_SMALL, name="in_proj_small")
        u_ab3 = u_ab.reshape(nb, length, U_AB)
        u_c3 = u_c.reshape(nb, length, 4 * C_WIDTH)
        u_s3 = u_s.reshape(nb, length, U_SMALL)

        xc = ssd_conv(u_ab3, ssm_conv_w[l], ssm_conv_b[l], n_ctx)
        consts = ssd_consts(ssm_dt_bias[l], ssm_a_log[l])
        gate_bias = mlstm_gate_bias(mlstm_i_bias[l], mlstm_f_bias[l])
        ya = [hgrn2_scan(u_ab3, lb_table(lb_all[l][dd]), n_ctx, rev=bool(dd)) for dd in range(2)]
        yb = [ssd_scan(xc, u_s3, *consts[dd], n_ctx, rev=bool(dd)) for dd in range(2)]
        yc = mlstm_scan(u_c3, u_s3, gate_bias, n_ctx)
        flat = lambda a: a.reshape(t_all, a.shape[-1])
        ymix = mix_out(flat(ya[0]), flat(ya[1]), flat(yb[0]), flat(yb[1]), flat(yc[0]), flat(yc[1]),
                       u_ab, flat(xc), u_c, hgrn_norm[l][None], ssm_norm[l][None], mlstm_norm[l][None],
                       jnp.repeat(ssm_d[l].astype(F32), B_HEADDIM)[None])
        if l % 2 == 1:
            y3 = ymix.reshape(nb, length, d)
            ymix = jnp.concatenate([y3[:, :n_ctx], _from_scan_order(y3[:, n_ctx:], rows)], axis=1)
            ymix = ymix.reshape(t_all, d)

        moe_w = (l, norm_ffn[l], router_w[l], router_bias[l], moe_w_gate, moe_w_up, moe_w_down,
                 shared_w_gate[l], shared_w_up[l], shared_w_down[l])
        ymix = ymix.reshape(nb, length, d)
        w_o = w_out[l].astype(BF16)
        if not last:
            x2 = matmul_residual(ymix, w_o, xs, tab[2], first_row=0, n_rows=length,
                                 ctx_rows=n_ctx, tm=256, tn=1024).reshape(t_all, d)
            x2 = _moe_block(x2, tab, _seg_fn(length, n_ctx, 256), _seg_fn(length, n_ctx, CMB_TM),
                            norm_final, False, *moe_w)
            xs = x2.reshape(nb, length, d)
        else:
            x2 = matmul_residual(ymix, w_o, xs, tab[2], first_row=n_ctx, n_rows=seq,
                                 ctx_rows=n_ctx, tm=256, tn=1024).reshape(t_lat, d)
            out = _moe_block(x2, tab, _seg_fn(seq, 0, 256), _seg_fn(seq, 0, CMB_TM),
                             norm_final, True, *moe_w)
            return out.reshape(nb, seq, d)
```

```python
import functools

import jax
import jax.numpy as jnp
from jax import lax
from jax.experimental import pallas as pl
from jax.experimental.pallas import tpu as pltpu

F32 = jnp.float32
BF16 = jnp.bfloat16
HIGHEST = lax.Precision.HIGHEST
NEG_INF = float("-inf")

D_MODEL = 4096
GRID_W = 64
CHUNK = 64
EPS = 1e-6
A_WIDTH = 1024
A_DK = 128
A_HEADS = 8
B_WIDTH = 2048
B_HEADDIM = 64
B_HEADS = 32
B_GROUPS = 8
B_RPG = 4
B_STATE = 128
B_CONV = 5
B_CONV_CH = 4096
C_WIDTH = 1024
C_DK = 128
C_HEADS = 8
N_EXPERTS = 64
N_EXPERT_GROUPS = 8
TOPK_GROUPS = 4
TOP_K = 8
D_EXPERT = 256
D_SHARED = 256
ROUTED_SCALE = 2.5

U_AB = 11264
U_DT0 = 11264
U_C0 = 11328
U_G0 = 15424
U_SMALL = 128

SUB = 16
PAD = 16
MAX_SAFE_EXPONENT = 80.0
VMEM_LIMIT = 56 * 1024 * 1024


def _cparams(sem):
    return pltpu.CompilerParams(dimension_semantics=sem, vmem_limit_bytes=VMEM_LIMIT)


def _dot(a, b):
    return jnp.dot(a.astype(BF16), b.astype(BF16), preferred_element_type=F32)


def _bdot(a, b, ca, cb):
    return lax.dot_general(a.astype(BF16), b.astype(BF16), (((ca,), (cb,)), ((0,), (0,))),
                           preferred_element_type=F32)


def _dot_hi(a, b):
    return jnp.dot(a, b, precision=HIGHEST, preferred_element_type=F32)


def _dot_nt_hi(a, b):
    return lax.dot_general(a, b, (((1,), (1,)), ((), ())), precision=HIGHEST,
                           preferred_element_type=F32)


def _log_sigmoid(z):
    return jnp.minimum(z, 0.0) - jnp.log(1.0 + jnp.exp(-jnp.abs(z)))


def _softplus(z):
    return jnp.maximum(z, 0.0) + jnp.log(1.0 + jnp.exp(-jnp.abs(z)))


def _silu(z):
    return z * jax.nn.sigmoid(z)


def _tri(rev):
    ri = lax.broadcasted_iota(jnp.int32, (CHUNK, CHUNK), 0)
    ci = lax.broadcasted_iota(jnp.int32, (CHUNK, CHUNK), 1)
    return (ci >= ri) if rev else (ci <= ri)


def _chunk_block(c, n_ctx_chunks, n_lat_chunks, rev):
    if not rev:
        return c
    return jnp.where(c < n_ctx_chunks, n_ctx_chunks - 1 - c, 2 * n_ctx_chunks + n_lat_chunks - 1 - c)


def _ada_kernel(c_ref, w_ref, b_ref, o_ref):
    s = _silu(c_ref[...])
    o_ref[...] = _dot(s, w_ref[...]) + b_ref[...]


def ada_modulation(cond, ada_w, ada_b):
    depth, d, n = ada_w.shape
    tn = 512
    return pl.pallas_call(
        _ada_kernel,
        grid=(depth, n // tn),
        in_specs=[
            pl.BlockSpec((8, d), lambda l, j: (0, 0)),
            pl.BlockSpec((None, d, tn), lambda l, j: (l, 0, j)),
            pl.BlockSpec((None, 1, tn), lambda l, j: (l, 0, j)),
        ],
        out_specs=pl.BlockSpec((None, 8, tn), lambda l, j: (l, 0, j)),
        out_shape=jax.ShapeDtypeStruct((depth, 8, n), F32),
        compiler_params=_cparams(("arbitrary", "arbitrary")),
        name="ada_modulation",
    )(cond, ada_w, ada_b.reshape(depth, 1, n))


def _pack_bf16_pair(a, b):
    hi = lax.bitcast_convert_type(a.astype(BF16).astype(F32), jnp.uint32)
    lo = lax.bitcast_convert_type(b.astype(BF16).astype(F32), jnp.uint32)
    return hi | (lo >> 16)


def _unpack_bf16_pair(w):
    hi = lax.bitcast_convert_type(w & jnp.uint32(0xFFFF0000), F32)
    lo = lax.bitcast_convert_type(w << 16, F32)
    return hi, lo


def _modulate_kernel(x_ref, g_ref, sh_ref, sc_ref, *rest, with_router):
    x = x_ref[...]
    y = x * lax.rsqrt(jnp.mean(x * x, axis=-1, keepdims=True) + EPS) * g_ref[...]
    h = y * (1.0 + sc_ref[...]) + sh_ref[...]
    if with_router:
        rw_ref, h_ref, lg_ref = rest
        h_hi = h.astype(BF16)
        h_lo = (h - h_hi.astype(F32)).astype(BF16)
        both = jnp.dot(h_hi, rw_ref[...], preferred_element_type=F32)
        lg_ref[...] = (both[:, 0:128] + both[:, 128:256]
                       + jnp.dot(h_lo, rw_ref[:, 0:128], preferred_element_type=F32))
        half = h.shape[1] // 2
        h_ref[...] = _pack_bf16_pair(h[:, :half], h[:, half:])
    else:
        (h_ref,) = rest
        h_ref[...] = h.astype(h_ref.dtype)


def modulate(x, g, shift, scale, seg_of_tile, tm, router_w=None):
    t, d = x.shape
    with_router = router_w is not None
    vec = pl.BlockSpec((None, 1, d), lambda i: (seg_of_tile(i), 0, 0))
    in_specs = [pl.BlockSpec((tm, d), lambda i: (i, 0)),
                pl.BlockSpec((1, d), lambda i: (0, 0)), vec, vec]
    args = [x, g.reshape(1, d), shift, scale]
    if with_router:
        in_specs.append(pl.BlockSpec((d, 256), lambda i: (0, 0)))
        args.append(router_w)
        out_specs = [pl.BlockSpec((tm, d // 2), lambda i: (i, 0)),
                     pl.BlockSpec((tm, 128), lambda i: (i, 0))]
        out_shape = [jax.ShapeDtypeStruct((t, d // 2), jnp.uint32),
                     jax.ShapeDtypeStruct((t, 128), F32)]
    else:
        out_specs = pl.BlockSpec((tm, d), lambda i: (i, 0))
        out_shape = jax.ShapeDtypeStruct((t, d), BF16)
    return pl.pallas_call(
        functools.partial(_modulate_kernel, with_router=with_router),
        grid=(t // tm,),
        in_specs=in_specs, out_specs=out_specs, out_shape=out_shape,
        compiler_params=_cparams(("arbitrary",)),
        name="modulate_router" if with_router else "modulate",
    )(*args)


def _mm_nt_kernel(a_ref, b_ref, o_ref):
    o_ref[...] = lax.dot_general(a_ref[...], b_ref[...].astype(BF16), (((1,), (1,)), ((), ())),
                                 preferred_element_type=F32)


def matmul_nt(a, b_t, layer, *, tm, tn, n_tiles=None, name="matmul_nt"):
    m, k = a.shape
    n_tiles = b_t.shape[1] // tn if n_tiles is None else n_tiles
    return pl.pallas_call(
        _mm_nt_kernel,
        grid=(m // tm, n_tiles),
        in_specs=[pl.BlockSpec((tm, k), lambda i, j: (i, 0)),
                  pl.BlockSpec((None, tn, k), lambda i, j: (layer, j, 0))],
        out_specs=pl.BlockSpec((tm, tn), lambda i, j: (i, j)),
        out_shape=jax.ShapeDtypeStruct((m, n_tiles * tn), F32),
        compiler_params=_cparams(("arbitrary", "arbitrary")),
        name=name,
    )(a, b_t)


def _mm_res_kernel(a_ref, b_ref, r_ref, g_ref, o_ref):
    acc = jnp.dot(a_ref[...], b_ref[...], preferred_element_type=F32)
    o_ref[...] = r_ref[...] + g_ref[...] * acc


def matmul_residual(a, b, res, gate, *, first_row, n_rows, ctx_rows, tm, tn):
    nb, _, k = a.shape
    n = b.shape[1]
    t0 = first_row // tm
    ctx_tiles = ctx_rows // tm
    seg = lambda bb, it: 2 * bb + jnp.where(t0 + it >= ctx_tiles, 1, 0)
    return pl.pallas_call(
        _mm_res_kernel,
        grid=(n // tn, nb, n_rows // tm),
        in_specs=[pl.BlockSpec((None, tm, k), lambda j, bb, it: (bb, t0 + it, 0)),
                  pl.BlockSpec((k, tn), lambda j, bb, it: (0, j)),
                  pl.BlockSpec((None, tm, tn), lambda j, bb, it: (bb, t0 + it, j)),
                  pl.BlockSpec((None, 1, tn), lambda j, bb, it: (seg(bb, it), 0, j))],
        out_specs=pl.BlockSpec((None, tm, tn), lambda j, bb, it: (bb, it, j)),
        out_shape=jax.ShapeDtypeStruct((nb, n_rows, n), F32),
        compiler_params=_cparams(("arbitrary", "arbitrary", "arbitrary")),
        name="matmul_residual",
    )(a, b, res, gate)


def _hgrn2_kernel(q_ref, f_ref, v_ref, lb_ref, y_ref, k_s, v_s, gc_s, st_s, *, rev):
    c = pl.program_id(1)

    @pl.when(c == 0)
    def _():
        st_s[...] = jnp.zeros_like(st_s)
        k_s[...] = jnp.zeros_like(k_s)
        v_s[...] = jnp.zeros_like(v_s)
        gc_s[...] = jnp.zeros_like(gc_s)

    nh = A_HEADS
    q = q_ref[0]
    z = f_ref[0]
    log_lb = lb_ref[0:1, :]
    log1m_lb = lb_ref[1:2, :]
    one_m_lb = lb_ref[2:3, :]
    b2 = log1m_lb + _log_sigmoid(z)
    logf = jnp.maximum(log_lb, b2) + jnp.log(1.0 + jnp.exp(-jnp.abs(log_lb - b2)))
    gc2 = _dot_hi(_tri(rev).astype(F32), logf)
    heads = lambda a: jnp.stack([a[:, h * A_DK:(h + 1) * A_DK] for h in range(nh)])
    qh = heads(q * jax.nn.sigmoid(q) * (A_DK ** -0.5))
    kh = heads(one_m_lb * jax.nn.sigmoid(-z))
    vh = heads(v_ref[0])
    gcm = heads(gc2)
    gxm = gcm - heads(logf)
    st = st_s[...]
    y_inter = _bdot(qh * jnp.exp(gcm), st, 2, 2)
    mid = gcm[:, CHUNK // 2:CHUNK // 2 + 1, :]

    def factored(_):
        sc = _bdot(qh * jnp.exp(gcm - mid), kh * jnp.exp(mid - gcm), 2, 2)
        return _bdot(jnp.where(_tri(rev)[None], sc, 0.0), vh, 2, 1)

    def exact(_):
        k_s[:, PAD:PAD + CHUNK, :] = kh
        v_s[:, PAD:PAD + CHUNK, :] = vh
        gc_s[:, PAD:PAD + CHUNK, :] = gcm
        rows = lax.broadcasted_iota(jnp.int32, (1, CHUNK, 1), 1)
        rowmod = rows % SUB
        n_sub = CHUNK // SUB
        y = jnp.zeros((nh, CHUNK, A_DK), F32)
        for dlt in range(SUB):
            off = PAD + dlt if rev else PAD - dlt
            ksh = k_s[:, off:off + CHUNK, :]
            vsh = v_s[:, off:off + CHUNK, :]
            gsh = gc_s[:, off:off + CHUNK, :]
            valid = (rowmod + dlt < SUB) if rev else (rowmod >= dlt)
            e = jnp.exp(jnp.where(valid, gcm - gsh, NEG_INF))
            r = jnp.sum(qh * ksh * e, axis=2, keepdims=True)
            y = y + r * vsh
        pieces = []
        for blk in range(n_sub):
            r0, r1 = blk * SUB, (blk + 1) * SUB
            first = (blk == n_sub - 1) if rev else (blk == 0)
            if first:
                pieces.append(jnp.zeros((nh, SUB, A_DK), F32))
                continue
            ref = gxm[:, r1 - 1:r1, :] if rev else gxm[:, r0:r0 + 1, :]
            earlier = (rows >= r1) if rev else (rows < r0)
            qhat = qh[:, r0:r1, :] * jnp.exp(gcm[:, r0:r1, :] - ref)
            khat = kh * jnp.exp(jnp.where(earlier, ref - gcm, NEG_INF))
            sc = _bdot(qhat, khat, 2, 2)
            pieces.append(_bdot(sc, vh, 2, 1))
        return y + jnp.concatenate(pieces, axis=1)

    worst = jnp.max(jnp.abs(gcm - mid))
    y = y_inter + lax.cond(worst < MAX_SAFE_EXPONENT, factored, exact, 0)
    gtot = gcm[:, 0:1, :] if rev else gcm[:, CHUNK - 1:CHUNK, :]
    st_s[...] = st * jnp.exp(gtot) + _bdot(vh, kh * jnp.exp(gtot - gcm), 1, 1)
    for h in range(nh):
        y_ref[0, :, h * A_DK:(h + 1) * A_DK] = y[h]


def hgrn2_scan(u_ab, lb_tab, n_ctx, rev):
    nb, length, _ = u_ab.shape
    n_chunks = length // CHUNK
    n_ctx_chunks = n_ctx // CHUNK
    blk = functools.partial(_chunk_block, n_ctx_chunks=n_ctx_chunks,
                            n_lat_chunks=n_chunks - n_ctx_chunks, rev=rev)
    fcol = 2 if rev else 1

    def col(j):
        return pl.BlockSpec((1, CHUNK, A_WIDTH), lambda b, c: (b, blk(c), j))

    return pl.pallas_call(
        functools.partial(_hgrn2_kernel, rev=rev),
        grid=(nb, n_chunks),
        in_specs=[col(0), col(fcol), col(3), pl.BlockSpec((3, A_WIDTH), lambda b, c: (0, 0))],
        out_specs=pl.BlockSpec((1, CHUNK, A_WIDTH), lambda b, c: (b, blk(c), 0)),
        out_shape=jax.ShapeDtypeStruct((nb, length, A_WIDTH), F32),
        scratch_shapes=[
            pltpu.VMEM((A_HEADS, CHUNK + 2 * PAD, A_DK), F32),
            pltpu.VMEM((A_HEADS, CHUNK + 2 * PAD, A_DK), F32),
            pltpu.VMEM((A_HEADS, CHUNK + 2 * PAD, A_DK), F32),
            pltpu.VMEM((A_HEADS, A_DK, A_DK), F32),
        ],
        compiler_params=_cparams(("arbitrary", "arbitrary")),
        name="hgrn2_bwd" if rev else "hgrn2_fwd",
    )(u_ab, u_ab, u_ab, lb_tab)


def _mlstm_kernel(qf_ref, kf_ref, vf_ref, gf_ref, qb_ref, kb_ref, vb_ref, gb_ref, bias_ref,
                  hf_ref, hb_ref, c_s, n_s, m_s):
    @pl.when(pl.program_id(1) == 0)
    def _():
        c_s[...] = jnp.zeros_like(c_s)
        n_s[...] = jnp.zeros_like(n_s)
        m_s[...] = jnp.zeros_like(m_s)

    nh = C_HEADS
    eye = (lax.broadcasted_iota(jnp.int32, (nh, nh), 0)
           == lax.broadcasted_iota(jnp.int32, (nh, nh), 1)).astype(F32)
    heads = lambda ref: [ref[0, :, h * C_DK:(h + 1) * C_DK] for h in range(nh)]
    bcols, igcols, arows, masks = [], [], [], []
    for d, g_ref in enumerate((gf_ref, gb_ref)):
        gates = g_ref[0] + bias_ref[...]
        ig_all = gates[:, 64 + 8 * d:72 + 8 * d]
        lf_all = _log_sigmoid(gates[:, 80 + 8 * d:88 + 8 * d])
        mask = _tri(bool(d))
        b_all = _dot_hi(mask.astype(F32), lf_all)
        arows.append(_dot_nt_hi(eye, ig_all - b_all))
        bcols += [b_all[:, h:h + 1] for h in range(nh)]
        igcols += [ig_all[:, h:h + 1] for h in range(nh)]
        masks.append(jnp.broadcast_to(mask[None], (nh, CHUNK, CHUNK)))
    q = jnp.stack(heads(qf_ref) + heads(qb_ref)) * (C_DK ** -0.5)
    k = jnp.stack(heads(kf_ref) + heads(kb_ref))
    v = jnp.stack(heads(vf_ref) + heads(vb_ref))
    bcol = jnp.stack(bcols)
    igcol = jnp.stack(igcols)
    arow = jnp.concatenate(arows, axis=0)[:, None, :]
    mask = jnp.concatenate(masks, axis=0)
    dmat = jnp.where(mask, bcol + arow, NEG_INF)
    m0 = m_s[...]
    m_inter = bcol + m0
    m_i = jnp.maximum(m_inter, jnp.max(dmat, axis=2, keepdims=True))
    w_intra = jnp.exp(dmat - m_i)
    w_inter = jnp.exp(m_inter - m_i)
    s = _bdot(q, k, 2, 2) * w_intra
    cmat = c_s[...]
    nvec = n_s[...]
    num = _bdot(s, v, 2, 1) + w_inter * _bdot(q, cmat, 2, 1)
    den = jnp.sum(s, axis=2, keepdims=True) + w_inter * jnp.sum(q * nvec, axis=2, keepdims=True)
    out = num / jnp.maximum(jnp.abs(den), jnp.exp(-m_i))
    for h in range(nh):
        hf_ref[0, :, h * C_DK:(h + 1) * C_DK] = out[h]
        hb_ref[0, :, h * C_DK:(h + 1) * C_DK] = out[nh + h]
    b_last = jnp.concatenate([bcol[:nh, CHUNK - 1:CHUNK, :], bcol[nh:, 0:1, :]], axis=0)
    log_wj = b_last - bcol + igcol
    m_new = jnp.maximum(b_last + m0, jnp.max(log_wj, axis=1, keepdims=True))
    w0 = jnp.exp(b_last + m0 - m_new)
    wk = jnp.exp(log_wj - m_new) * k
    c_s[...] = w0 * cmat + _bdot(wk, v, 1, 1)
    n_s[...] = w0 * nvec + jnp.sum(wk, axis=1, keepdims=True)
    m_s[...] = m_new


def mlstm_scan(u_c, u_s, gate_bias, n_ctx):
    nb, length, _ = u_c.shape
    n_chunks = length // CHUNK
    n_ctx_chunks = n_ctx // CHUNK
    blks = [functools.partial(_chunk_block, n_ctx_chunks=n_ctx_chunks,
                              n_lat_chunks=n_chunks - n_ctx_chunks, rev=rev) for rev in (False, True)]
    in_specs = []
    for blk in blks:
        in_specs += [pl.BlockSpec((1, CHUNK, C_WIDTH), lambda b, c, j=j, blk=blk: (b, blk(c), j))
                     for j in range(3)]
        in_specs.append(pl.BlockSpec((1, CHUNK, U_SMALL), lambda b, c, blk=blk: (b, blk(c), 0)))
    in_specs.append(pl.BlockSpec((1, U_SMALL), lambda b, c: (0, 0)))
    return pl.pallas_call(
        _mlstm_kernel,
        grid=(nb, n_chunks),
        in_specs=in_specs,
        out_specs=[pl.BlockSpec((1, CHUNK, C_WIDTH), lambda b, c, blk=blk: (b, blk(c), 0)) for blk in blks],
        out_shape=[jax.ShapeDtypeStruct((nb, length, C_WIDTH), F32)] * 2,
        scratch_shapes=[
            pltpu.VMEM((2 * C_HEADS, C_DK, C_DK), F32),
            pltpu.VMEM((2 * C_HEADS, 1, C_DK), F32),
            pltpu.VMEM((2 * C_HEADS, 1, 1), F32),
        ],
        compiler_params=_cparams(("arbitrary", "arbitrary")),
        name="mlstm_scan",
    )(u_c, u_c, u_c, u_s, u_c, u_c, u_c, u_s, gate_bias)


CONV_TM = 256
CONV_TN = 1024
HALO = 8


def _conv_kernel(prev_ref, cur_ref, next_ref, w_ref, b_ref, o_ref, win_s, *, tiles_ctx, tiles_all):
    i = pl.program_id(1)
    seg_start = (i == 0) | (i == tiles_ctx)
    seg_end = (i == tiles_ctx - 1) | (i == tiles_all - 1)
    win_s[0:HALO, :] = jnp.where(seg_start, 0.0, prev_ref[0])
    win_s[HALO:HALO + CONV_TM, :] = cur_ref[0]
    win_s[HALO + CONV_TM:2 * HALO + CONV_TM, :] = jnp.where(seg_end, 0.0, next_ref[0])
    acc = b_ref[...] + jnp.zeros((CONV_TM, CONV_TN), F32)
    for t in range(B_CONV):
        o = HALO + t - B_CONV // 2
        acc = acc + w_ref[t:t + 1, :] * win_s[o:o + CONV_TM, :]
    o_ref[0] = _silu(acc)


def ssd_conv(u_ab, conv_w, conv_b, n_ctx):
    nb, length, _ = u_ab.shape
    tiles_all = length // CONV_TM
    tiles_ctx = n_ctx // CONV_TM
    col0 = (U_AB - B_CONV_CH) // CONV_TN
    per = CONV_TM // HALO
    n_halo = length // HALO
    return pl.pallas_call(
        functools.partial(_conv_kernel, tiles_ctx=tiles_ctx, tiles_all=tiles_all),
        grid=(nb, tiles_all, B_CONV_CH // CONV_TN),
        in_specs=[
            pl.BlockSpec((1, HALO, CONV_TN), lambda b, i, j: (b, jnp.maximum(i * per - 1, 0), col0 + j)),
            pl.BlockSpec((1, CONV_TM, CONV_TN), lambda b, i, j: (b, i, col0 + j)),
            pl.BlockSpec((1, HALO, CONV_TN),
                         lambda b, i, j: (b, jnp.minimum((i + 1) * per, n_halo - 1), col0 + j)),
            pl.BlockSpec((B_CONV, CONV_TN), lambda b, i, j: (0, j)),
            pl.BlockSpec((1, CONV_TN), lambda b, i, j: (0, j)),
        ],
        out_specs=pl.BlockSpec((1, CONV_TM, CONV_TN), lambda b, i, j: (b, i, j)),
        out_shape=jax.ShapeDtypeStruct((nb, length, B_CONV_CH), F32),
        scratch_shapes=[pltpu.VMEM((CONV_TM + 2 * HALO, CONV_TN), F32)],
        compiler_params=_cparams(("arbitrary", "arbitrary", "arbitrary")),
        name="ssd_conv",
    )(u_ab, u_ab, u_ab, conv_w, conv_b.reshape(1, B_CONV_CH))


def _ssd_kernel(xc_ref, us_ref, bias_ref, an_ref, e_ref, y_ref, s_s, *, rev):
    c = pl.program_id(1)

    @pl.when(c == 0)
    def _():
        s_s[...] = jnp.zeros_like(s_s)

    d = 1 if rev else 0
    ng = B_GROUPS
    gw = B_RPG * B_HEADDIM
    mask = _tri(rev)
    last = 0 if rev else CHUNK - 1
    dt_n = _softplus(us_ref[0] + bias_ref[...])[:, 32 * d:32 * d + B_HEADS]
    cum_n = _dot_hi(mask.astype(F32), dt_n * an_ref[...])
    eye = (lax.broadcasted_iota(jnp.int32, (B_HEADS, B_HEADS), 0)
           == lax.broadcasted_iota(jnp.int32, (B_HEADS, B_HEADS), 1)).astype(F32)
    cum_t = _dot_nt_hi(eye, cum_n)
    decay_in = jnp.exp(cum_n)
    decay_out = jnp.exp(cum_n[last:last + 1, :] - cum_n)
    spread = _dot(jnp.concatenate([dt_n, decay_in, decay_out], axis=0), e_ref[...])
    groups = lambda a: jnp.stack([a[:, g * gw:(g + 1) * gw] for g in range(ng)])
    dt_x = groups(spread[0:CHUNK])
    in_x = groups(spread[CHUNK:2 * CHUNK])
    out_x = groups(spread[2 * CHUNK:3 * CHUNK])
    xdt = jnp.stack([xc_ref[0, :, g * gw:(g + 1) * gw] for g in range(ng)]) * dt_x
    bm = jnp.stack([xc_ref[0, :, B_WIDTH + g * B_STATE:B_WIDTH + (g + 1) * B_STATE]
                    for g in range(ng)])
    c0 = B_WIDTH + B_GROUPS * B_STATE
    cm = jnp.stack([xc_ref[0, :, c0 + g * B_STATE:c0 + (g + 1) * B_STATE] for g in range(ng)])
    state = s_s[...]
    cb = _bdot(cm, bm, 2, 2)
    y = in_x * _bdot(cm, state, 2, 1)
    head_of_lane = lax.broadcasted_iota(jnp.int32, (1, 1, gw), 2) // B_HEADDIM
    for r in range(B_RPG):
        ccol = jnp.stack([cum_n[:, g * B_RPG + r:g * B_RPG + r + 1] for g in range(ng)])
        crow = jnp.stack([cum_t[g * B_RPG + r:g * B_RPG + r + 1, :] for g in range(ng)])
        lmat = cb * jnp.exp(jnp.where(mask[None], ccol - crow, NEG_INF))
        y = y + _bdot(lmat, jnp.where(head_of_lane == r, xdt, 0.0), 2, 1)
    s_s[...] = in_x[:, last:last + 1, :] * state + _bdot(bm, xdt * out_x, 1, 1)
    for g in range(ng):
        y_ref[0, :, g * gw:(g + 1) * gw] = y[g]


def ssd_scan(xc, u_s, dt_bias, a_n, expand, n_ctx, rev):
    nb, length, _ = xc.shape
    n_chunks = length // CHUNK
    n_ctx_chunks = n_ctx // CHUNK
    blk = functools.partial(_chunk_block, n_ctx_chunks=n_ctx_chunks,
                            n_lat_chunks=n_chunks - n_ctx_chunks, rev=rev)
    const = lambda shape: pl.BlockSpec(shape, lambda b, c: (0, 0))
    return pl.pallas_call(
        functools.partial(_ssd_kernel, rev=rev),
        grid=(nb, n_chunks),
        in_specs=[pl.BlockSpec((1, CHUNK, B_CONV_CH), lambda b, c: (b, blk(c), 0)),
                  pl.BlockSpec((1, CHUNK, U_SMALL), lambda b, c: (b, blk(c), 0)),
                  const((1, U_SMALL)), const((1, B_HEADS)), const((B_HEADS, B_WIDTH))],
        out_specs=pl.BlockSpec((1, CHUNK, B_WIDTH), lambda b, c: (b, blk(c), 0)),
        out_shape=jax.ShapeDtypeStruct((nb, length, B_WIDTH), F32),
        scratch_shapes=[pltpu.VMEM((B_GROUPS, B_STATE, B_RPG * B_HEADDIM), F32)],
        compiler_params=_cparams(("arbitrary", "arbitrary")),
        name="ssd_bwd" if rev else "ssd_fwd",
    )(xc, u_s, dt_bias, a_n, expand)


def _group_rmsnorm(y, width):
    out = []
    for g in range(y.shape[1] // width):
        yg = y[:, g * width:(g + 1) * width]
        out.append(yg * lax.rsqrt(jnp.mean(yg * yg, axis=1, keepdims=True) + EPS))
    return jnp.concatenate(out, axis=1)


def _mix_out_kernel(af_ref, ab_ref, ag_ref, bf_ref, bb_ref, bx_ref, bz0_ref, bz1_ref,
                    cf_ref, cb_ref, co_ref, wa_ref, wb_ref, wc_ref, dsk_ref, o_ref):
    ya = _group_rmsnorm(af_ref[...] + ab_ref[...], A_DK) * wa_ref[...] * _silu(ag_ref[...])
    o_ref[:, 0:A_WIDTH] = ya.astype(o_ref.dtype)
    z = jnp.concatenate([bz0_ref[...], bz1_ref[...]], axis=1)
    yb = (bf_ref[...] + bb_ref[...] + dsk_ref[...] * bx_ref[...]) * _silu(z)
    yb = _group_rmsnorm(yb, B_WIDTH // B_GROUPS) * wb_ref[...]
    o_ref[:, A_WIDTH:A_WIDTH + B_WIDTH] = yb.astype(o_ref.dtype)
    yc = (cf_ref[...] + cb_ref[...]) * jax.nn.sigmoid(co_ref[...])
    yc = _group_rmsnorm(yc, C_DK) * wc_ref[...]
    o_ref[:, A_WIDTH + B_WIDTH:] = yc.astype(o_ref.dtype)


def mix_out(ya_f, ya_b, yb_f, yb_b, yc_f, yc_b, u_ab, xc, u_c, wa, wb, wc, dskip_x, tm=128):
    t = ya_f.shape[0]
    row = lambda w, j=0: pl.BlockSpec((tm, w), lambda i: (i, j))
    vec = lambda w: pl.BlockSpec((1, w), lambda i: (0, 0))
    return pl.pallas_call(
        _mix_out_kernel,
        grid=(t // tm,),
        in_specs=[row(A_WIDTH), row(A_WIDTH), row(A_WIDTH, 4),
                  row(B_WIDTH), row(B_WIDTH), row(B_WIDTH), row(1024, 5), row(1024, 6),
                  row(C_WIDTH), row(C_WIDTH), row(C_WIDTH, 3),
                  vec(A_WIDTH), vec(B_WIDTH), vec(C_WIDTH), vec(B_WIDTH)],
        out_specs=pl.BlockSpec((tm, D_MODEL), lambda i: (i, 0)),
        out_shape=jax.ShapeDtypeStruct((t, D_MODEL), BF16),
        compiler_params=_cparams(("arbitrary",)),
        name="mix_out",
    )(ya_f, ya_b, u_ab, yb_f, yb_b, xc, u_ab, u_ab, yc_f, yc_b, u_c, wa, wb, wc, dskip_x)


def _first_argmax(vals, lane):
    m = jnp.max(vals, axis=1, keepdims=True)
    idx = jnp.min(jnp.where(vals == m, lane, 1 << 20), axis=1, keepdims=True)
    return m, idx


def _route_kernel(lg_ref, bias_ref, idx_ref, w_ref, rank_ref, cnt_ref, run_s):
    @pl.when(pl.program_id(0) == 0)
    def _():
        run_s[...] = jnp.zeros_like(run_s)

    logits = lg_ref[...]
    tm = logits.shape[0]
    lane = lax.broadcasted_iota(jnp.int32, (tm, 128), 1)
    is_expert = lane < N_EXPERTS
    scores = jax.nn.sigmoid(logits)
    biased = jnp.where(is_expert, scores + bias_ref[...], NEG_INF)
    per_group = N_EXPERTS // N_EXPERT_GROUPS
    grp = lane // per_group
    gscore = jnp.full((tm, 128), NEG_INF, F32)
    for g in range(N_EXPERT_GROUPS):
        vals = jnp.where(grp == g, biased, NEG_INF)
        m1, i1 = _first_argmax(vals, lane)
        m2 = jnp.max(jnp.where(lane == i1, NEG_INF, vals), axis=1, keepdims=True)
        gscore = jnp.where(lane == g, m1 + m2, gscore)
    allowed = jnp.zeros((tm, 128), jnp.bool_)
    for _ in range(TOPK_GROUPS):
        _, gi = _first_argmax(gscore, lane)
        allowed = allowed | (grp == gi)
        gscore = jnp.where(lane == gi, NEG_INF, gscore)
    masked = jnp.where(allowed & is_expert, biased, NEG_INF)
    idx_out = jnp.zeros((tm, 128), jnp.int32)
    w_out = jnp.zeros((tm, 128), F32)
    picks = []
    chosen = jnp.zeros((tm, 128), F32)
    for kk in range(TOP_K):
        _, ei = _first_argmax(masked, lane)
        sel = lane == ei
        picks.append(sel)
        chosen = chosen + sel.astype(F32)
        wk = jnp.sum(jnp.where(sel, scores, 0.0), axis=1, keepdims=True)
        idx_out = jnp.where(lane == kk, ei, idx_out)
        w_out = jnp.where(lane == kk, wk, w_out)
        masked = jnp.where(sel, NEG_INF, masked)
    w_out = w_out / jnp.sum(w_out, axis=1, keepdims=True) * ROUTED_SCALE
    idx_ref[...] = idx_out
    w_ref[...] = w_out
    strict = (lax.broadcasted_iota(jnp.int32, (tm, tm), 1)
              < lax.broadcasted_iota(jnp.int32, (tm, tm), 0)).astype(F32)
    before = _dot(strict, chosen) + run_s[...]
    rank_out = jnp.zeros((tm, 128), jnp.int32)
    for kk in range(TOP_K):
        rk = jnp.sum(jnp.where(picks[kk], before, 0.0), axis=1, keepdims=True)
        rank_out = jnp.where(lane == kk, rk.astype(jnp.int32), rank_out)
    rank_ref[...] = rank_out
    total = run_s[...] + jnp.sum(chosen, axis=0, keepdims=True)
    run_s[...] = total
    cnt_ref[...] = total.astype(jnp.int32)


def route(logits, router_bias, tm=1024):
    t = logits.shape[0]
    bias = jnp.concatenate([router_bias.astype(F32), jnp.zeros((128 - N_EXPERTS,), F32)])[None]
    row = pl.BlockSpec((tm, 128), lambda i: (i, 0))
    one = pl.BlockSpec((1, 128), lambda i: (0, 0))
    return pl.pallas_call(
        _route_kernel,
        grid=(t // tm,),
        in_specs=[row, one],
        out_specs=[row, row, row, one],
        out_shape=[jax.ShapeDtypeStruct((t, 128), jnp.int32), jax.ShapeDtypeStruct((t, 128), F32),
                   jax.ShapeDtypeStruct((t, 128), jnp.int32), jax.ShapeDtypeStruct((1, 128), jnp.int32)],
        scratch_shapes=[pltpu.VMEM((1, 128), F32)],
        compiler_params=_cparams(("arbitrary",)),
        name="route",
    )(logits, bias)


EXP_TM = 256
SCT_TM = 256


def _scatter_kernel(lt_ref, pos_ref, hp_ref, xs_hbm, zero_s, sem, zsem):
    i = pl.program_id(0)

    @pl.when(i == 0)
    def _():
        zero_s[...] = jnp.zeros_like(zero_s)
        for e in range(2 * N_EXPERTS):
            @pl.when(lt_ref[e] >= 0)
            def _():
                pltpu.make_async_copy(zero_s, xs_hbm.at[pl.ds(lt_ref[e] * EXP_TM, EXP_TM), :], zsem).start()
        for e in range(2 * N_EXPERTS):
            @pl.when(lt_ref[e] >= 0)
            def _():
                pltpu.make_async_copy(zero_s, xs_hbm.at[pl.ds(lt_ref[e] * EXP_TM, EXP_TM), :], zsem).wait()

    def body(r, carry):
        for kk in range(TOP_K):
            p = pos_ref[0, r * TOP_K + kk]
            pltpu.make_async_copy(hp_ref.at[pl.ds(r, 1), :], xs_hbm.at[pl.ds(p, 1), :], sem).start(
                priority=kk % 2)
        return carry
    lax.fori_loop(0, SCT_TM, body, 0)
    for kk in range(TOP_K):
        pltpu.make_async_copy(hp_ref, xs_hbm.at[pl.ds(0, SCT_TM), :], sem).wait()


def scatter_rows(hp, pos, clear_tiles, n_rows):
    t, half = hp.shape
    n = t // SCT_TM
    grid_spec = pltpu.PrefetchScalarGridSpec(
        num_scalar_prefetch=1,
        grid=(n,),
        in_specs=[
            pl.BlockSpec((None, 1, SCT_TM * TOP_K), lambda i, lt: (i, 0, 0), memory_space=pltpu.SMEM),
            pl.BlockSpec((SCT_TM, half), lambda i, lt: (i, 0)),
        ],
        out_specs=pl.BlockSpec(memory_space=pl.ANY),
        scratch_shapes=[
            pltpu.VMEM((EXP_TM, half), jnp.uint32),
            pltpu.SemaphoreType.DMA(()),
            pltpu.SemaphoreType.DMA(()),
        ],
    )
    return pl.pallas_call(
        _scatter_kernel,
        grid_spec=grid_spec,
        out_shape=jax.ShapeDtypeStruct((n_rows, half), jnp.uint32),
        compiler_params=_cparams(("arbitrary",)),
        name="scatter_rows",
    )(clear_tiles, pos.reshape(n, 1, SCT_TM * TOP_K), hp)


def _expert_weight_copies(e, layer, wg_hbm, wu_hbm, wd_hbm, wg_buf, wu_buf, wd_buf, sem, slot):
    return [pltpu.make_async_copy(wg_hbm.at[layer, e], wg_buf.at[slot], sem.at[slot]),
            pltpu.make_async_copy(wu_hbm.at[layer, e], wu_buf.at[slot], sem.at[slot]),
            pltpu.make_async_copy(wd_hbm.at[layer, e], wd_buf.at[slot], sem.at[slot])]


def _expert_kernel(te_ref, nu_ref, nx_ref, x_ref, wg_hbm, wu_hbm, wd_hbm, y_ref,
                   wg_buf, wu_buf, wd_buf, sem, wgu_s, wd_s, grp_s, *, layer):
    i = pl.program_id(0)
    n_used = nu_ref[0]
    half = x_ref.shape[1]
    copies = functools.partial(_expert_weight_copies, layer=layer, wg_hbm=wg_hbm, wu_hbm=wu_hbm,
                               wd_hbm=wd_hbm, wg_buf=wg_buf, wu_buf=wu_buf, wd_buf=wd_buf, sem=sem)

    @pl.when(i == 0)
    def _():
        grp_s[0] = 0
        for cp in copies(te_ref[0], slot=0):
            cp.start()

    new_expert = (i == 0) | (te_ref[i] != te_ref[jnp.maximum(i - 1, 0)])

    @pl.when(new_expert & (i < n_used))
    def _():
        slot = grp_s[0] % 2
        for cp in copies(te_ref[i], slot=slot):
            cp.wait()
        wgu_s[:, 0:D_EXPERT] = wg_buf[slot].astype(BF16)
        wgu_s[:, D_EXPERT:2 * D_EXPERT] = wu_buf[slot].astype(BF16)
        wd_s[...] = wd_buf[slot].astype(BF16)

        @pl.when(nx_ref[i] >= 0)
        def _():
            for cp in copies(nx_ref[i], slot=1 - slot):
                cp.start()
        grp_s[0] = grp_s[0] + 1

    @pl.when(i < n_used)
    def _():
        x_hi, x_lo = _unpack_bf16_pair(x_ref[...])
        h = (jnp.dot(x_hi.astype(BF16), wgu_s[0:half, :], preferred_element_type=F32)
             + jnp.dot(x_lo.astype(BF16), wgu_s[half:2 * half, :], preferred_element_type=F32))
        act = _silu(h[:, 0:D_EXPERT]) * h[:, D_EXPERT:2 * D_EXPERT]
        y = jnp.dot(act.astype(BF16), wd_s[...], preferred_element_type=F32)
        y_ref[...] = _pack_bf16_pair(y[:, 0:half], y[:, half:2 * half])

    @pl.when(i >= n_used)
    def _():
        y_ref[...] = jnp.zeros_like(y_ref)


def routed_experts(x_sorted, tile_expert, n_used, next_expert, w_gate, w_up, w_down, layer):
    n_rows, half = x_sorted.shape
    n_tiles = n_rows // EXP_TM
    d = 2 * half
    grid_spec = pltpu.PrefetchScalarGridSpec(
        num_scalar_prefetch=3,
        grid=(n_tiles,),
        in_specs=[
            pl.BlockSpec((EXP_TM, half), lambda i, te, nu, nx: (jnp.minimum(i, nu[0] - 1), 0)),
            pl.BlockSpec(memory_space=pl.ANY),
            pl.BlockSpec(memory_space=pl.ANY),
            pl.BlockSpec(memory_space=pl.ANY),
        ],
        out_specs=pl.BlockSpec((EXP_TM, half), lambda i, te, nu, nx: (i, 0)),
        scratch_shapes=[
            pltpu.VMEM((2, d, D_EXPERT), F32),
            pltpu.VMEM((2, d, D_EXPERT), F32),
            pltpu.VMEM((2, D_EXPERT, d), F32),
            pltpu.SemaphoreType.DMA((2,)),
            pltpu.VMEM((d, 2 * D_EXPERT), BF16),
            pltpu.VMEM((D_EXPERT, d), BF16),
            pltpu.SMEM((1,), jnp.int32),
        ],
    )
    return pl.pallas_call(
        functools.partial(_expert_kernel, layer=layer),
        grid_spec=grid_spec,
        out_shape=jax.ShapeDtypeStruct((n_rows, half), jnp.uint32),
        compiler_params=_cparams(("arbitrary",)),
        name="routed_experts",
    )(tile_expert, n_used, next_expert, x_sorted, w_gate, w_up, w_down)


CMB_TM = 128


def _start_combine_gather(pos_ref, y_hbm, dst, sem, first_row=0, n_rows=CMB_TM):
    def body(r, carry):
        for kk in range(TOP_K):
            p = pos_ref[0, r * TOP_K + kk]
            pltpu.make_async_copy(y_hbm.at[pl.ds(p, 1), :], dst.at[kk, pl.ds(r, 1), :], sem).start(
                priority=kk % 2)
        return carry
    lax.fori_loop(first_row, first_row + n_rows, body, 0)


def _combine_kernel(pos_ref, posn_ref, y_hbm, w_ref, hp_ref, x_ref, gate_ref, sgu_ref, sd_ref,
                    norm_ref, o_ref, ybuf, sem, *, final_norm):
    i = pl.program_id(0)
    n = pl.num_programs(0)
    slot = i % 2
    half = hp_ref.shape[1]

    @pl.when(i == 0)
    def _():
        _start_combine_gather(pos_ref, y_hbm, ybuf.at[0], sem.at[0])

    x_hi, x_lo = _unpack_bf16_pair(hp_ref[...])
    h = (jnp.dot(x_hi.astype(BF16), sgu_ref[0:half, :], preferred_element_type=F32)
         + jnp.dot(x_lo.astype(BF16), sgu_ref[half:2 * half, :], preferred_element_type=F32))
    act = _silu(h[:, 0:D_SHARED]) * h[:, D_SHARED:2 * D_SHARED]
    shared = jnp.dot(act.astype(BF16), sd_ref[...], preferred_element_type=F32)

    for kk in range(TOP_K):
        pltpu.make_async_copy(y_hbm.at[pl.ds(0, CMB_TM), :], ybuf.at[slot, kk], sem.at[slot]).wait()
    acc_hi = shared[:, 0:half]
    acc_lo = shared[:, half:2 * half]
    w = w_ref[...]
    rows_per_burst = CMB_TM // TOP_K
    for kk in range(TOP_K):
        @pl.when(i + 1 < n)
        def _():
            _start_combine_gather(posn_ref, y_hbm, ybuf.at[1 - slot], sem.at[1 - slot],
                                  first_row=kk * rows_per_burst, n_rows=rows_per_burst)

        y_hi, y_lo = _unpack_bf16_pair(ybuf[slot, kk])
        wk = w[:, kk:kk + 1]
        acc_hi = acc_hi + wk * y_hi
        acc_lo = acc_lo + wk * y_lo
    g = gate_ref[...]
    out_hi = x_ref[:, 0:half] + g[:, 0:half] * acc_hi
    out_lo = x_ref[:, half:2 * half] + g[:, half:2 * half] * acc_lo
    if final_norm:
        ms = (jnp.sum(out_hi * out_hi, axis=1, keepdims=True)
              + jnp.sum(out_lo * out_lo, axis=1, keepdims=True)) / (2 * half)
        inv = lax.rsqrt(ms + EPS)
        out_hi = out_hi * inv * norm_ref[:, 0:half]
        out_lo = out_lo * inv * norm_ref[:, half:2 * half]
    o_ref[:, 0:half] = out_hi
    o_ref[:, half:2 * half] = out_lo


def moe_combine(pos, y_sorted, w, hp, x, gate, seg_of_tile, sh_gate_up, sh_down, norm_w, final_norm):
    t, d = x.shape
    half = d // 2
    n = t // CMB_TM
    return pl.pallas_call(
        functools.partial(_combine_kernel, final_norm=final_norm),
        grid=(n,),
        in_specs=[
            pl.BlockSpec((None, 1, CMB_TM * TOP_K), lambda i: (i, 0, 0), memory_space=pltpu.SMEM),
            pl.BlockSpec((None, 1, CMB_TM * TOP_K), lambda i: (jnp.minimum(i + 1, n - 1), 0, 0),
                         memory_space=pltpu.SMEM),
            pl.BlockSpec(memory_space=pl.ANY),
            pl.BlockSpec((CMB_TM, 128), lambda i: (i, 0)),
            pl.BlockSpec((CMB_TM, half), lambda i: (i, 0)),
            pl.BlockSpec((CMB_TM, d), lambda i: (i, 0)),
            pl.BlockSpec((None, 1, d), lambda i: (seg_of_tile(i), 0, 0)),
            pl.BlockSpec((d, 2 * D_SHARED), lambda i: (0, 0)),
            pl.BlockSpec((D_SHARED, d), lambda i: (0, 0)),
            pl.BlockSpec((1, d), lambda i: (0, 0)),
        ],
        out_specs=pl.BlockSpec((CMB_TM, d), lambda i: (i, 0)),
        out_shape=jax.ShapeDtypeStruct((t, d), F32),
        scratch_shapes=[
            pltpu.VMEM((2, TOP_K, CMB_TM, half), jnp.uint32),
            pltpu.SemaphoreType.DMA((2,)),
        ],
        compiler_params=_cparams(("arbitrary",)),
        name="moe_combine",
    )(pos, pos, y_sorted, w, hp, x, gate, sh_gate_up, sh_down, norm_w)


def moe_plan(eidx, rank, counts, n_tiles):
    tiles_per = (counts + EXP_TM - 1) // EXP_TM
    tile_end = jnp.cumsum(tiles_per)
    base = (tile_end - tiles_per) * EXP_TM
    experts = jnp.arange(N_EXPERTS, dtype=jnp.int32)
    pos = rank + jnp.sum(jnp.where(eidx[..., None] == experts, base, 0), axis=-1)
    tile_expert = jnp.sum((tile_end[None, :] <= jnp.arange(n_tiles, dtype=jnp.int32)[:, None])
                          .astype(jnp.int32), axis=1)
    tile_expert = jnp.minimum(tile_expert, N_EXPERTS - 1)
    last_tile = jnp.where(tiles_per > 0, tile_end - 1, -1)
    unused = tile_end[-1] + experts
    clear_tiles = jnp.concatenate([last_tile, jnp.where(unused < n_tiles, unused, -1)])
    later = (experts[None, :] > experts[:, None]) & (tiles_per[None, :] > 0)
    nxt = jnp.min(jnp.where(later, experts[None, :], N_EXPERTS), axis=1)
    nxt = jnp.where(nxt < N_EXPERTS, nxt, -1)
    next_expert = jnp.sum(jnp.where(tile_expert[:, None] == experts[None, :], nxt[None, :], 0), axis=1)
    return tile_expert, tile_end[-1:], clear_tiles, next_expert, pos


def lb_table(lb):
    lb = lb.astype(F32)
    return jnp.stack([jnp.log(lb), jnp.log1p(-lb), 1.0 - lb])


def mlstm_gate_bias(i_bias, f_bias):
    z = jnp.zeros((64,), F32)
    return jnp.concatenate([z, i_bias.astype(F32).reshape(-1), f_bias.astype(F32).reshape(-1),
                            jnp.zeros((32,), F32)])[None]


def ssd_consts(dt_bias, a_log):
    bias = jnp.concatenate([dt_bias.astype(F32).reshape(-1), jnp.zeros((64,), F32)])[None]
    a = -jnp.exp(a_log.astype(F32))
    expand = (jnp.arange(B_HEADS)[:, None] == jnp.arange(B_WIDTH)[None, :] // B_HEADDIM).astype(BF16)
    return [(bias, a[d][None], expand) for d in range(2)]


def _mod_tables(mod, nb):
    d = mod.shape[1] // 6
    parts = mod.reshape(mod.shape[0], 6, d)
    lat = parts[:nb]
    ctx = jnp.broadcast_to(parts[nb][None], (nb, 6, d))
    tab = jnp.stack([ctx, lat], axis=1).reshape(2 * nb, 6, d)
    return jnp.transpose(tab, (1, 0, 2))[:, :, None, :]


def _seg_fn(rows_per_batch, ctx_rows, tm):
    tiles_per_batch = rows_per_batch // tm
    ctx_tiles = ctx_rows // tm

    def seg(i):
        return 2 * (i // tiles_per_batch) + jnp.where(i % tiles_per_batch >= ctx_tiles, 1, 0)
    return seg


def _to_scan_order(a, rows):
    nb, _, d = a.shape
    return a.reshape(nb, rows, GRID_W, d).transpose(0, 2, 1, 3).reshape(nb, rows * GRID_W, d)


def _from_scan_order(a, rows):
    nb, _, d = a.shape
    return a.reshape(nb, GRID_W, rows, d).transpose(0, 2, 1, 3).reshape(nb, rows * GRID_W, d)


def _moe_block(x2, tab, seg_mod, seg_cmb, out_norm_w, final_norm, layer, norm_w, router_w, router_bias,
               w_gate, w_up, w_down, sh_gate, sh_up, sh_down):
    t, d = x2.shape
    rw = jnp.concatenate([router_w.astype(F32), jnp.zeros((d, 128 - N_EXPERTS), F32)], axis=1)
    rw_hi = rw.astype(BF16)
    rw = jnp.concatenate([rw_hi, (rw - rw_hi.astype(F32)).astype(BF16)], axis=1)
    hp, logits = modulate(x2, norm_w, tab[3], tab[4], seg_mod, 256, router_w=rw)
    eidx, ew, rank, counts = route(logits, router_bias)
    n_tiles = t * TOP_K // EXP_TM + N_EXPERTS
    tile_expert, n_used, clear_tiles, next_expert, pos = moe_plan(
        eidx[:, :TOP_K], rank[:, :TOP_K], counts[0, :N_EXPERTS], n_tiles)
    x_sorted = scatter_rows(hp, pos, clear_tiles, n_tiles * EXP_TM)
    y_sorted = routed_experts(x_sorted, tile_expert, n_used, next_expert, w_gate, w_up, w_down, layer)
    sgu = jnp.concatenate([sh_gate, sh_up], axis=1).astype(BF16)
    return moe_combine(pos.reshape(t // CMB_TM, 1, CMB_TM * TOP_K), y_sorted, ew, hp, x2, tab[5],
                       seg_cmb, sgu, sh_down.astype(BF16), out_norm_w.reshape(1, d), final_norm)


def kernel(x, c, ctx, c_ctx, ada_w, ada_b, norm_mix, norm_ffn, norm_final, w_in, w_out, hgrn_lb,
           hgrn_norm, ssm_conv_w, ssm_conv_b, ssm_dt_bias, ssm_a_log, ssm_d, ssm_norm, mlstm_i_bias,
           mlstm_f_bias, mlstm_norm, router_w, router_bias, moe_w_gate, moe_w_up, moe_w_down,
           shared_w_gate, shared_w_up, shared_w_down):
    nb, seq, d = x.shape
    n_ctx = ctx.shape[1]
    length = n_ctx + seq
    rows = seq // GRID_W
    depth = ada_w.shape[0]
    t_all = nb * length
    t_lat = nb * seq

    lb_all = jnp.cumsum(jax.nn.softmax(hgrn_lb.astype(F32), axis=0), axis=0)
    lb_all = lb_all - lb_all[0]
    cond = jnp.concatenate([c, c_ctx[None], jnp.zeros((8 - nb - 1, d), F32)], axis=0)
    mod = ada_modulation(cond, ada_w, ada_b)

    xs = jnp.concatenate([ctx, x], axis=1)
    w_in_t = jnp.swapaxes(w_in, 1, 2)
    for l in range(depth):
        last = l == depth - 1
        tab = _mod_tables(mod[l], nb)
        h = modulate(xs.reshape(t_all, d), norm_mix[l], tab[0], tab[1], _seg_fn(length, n_ctx, 256), 256)
        if l % 2 == 1:
            h3 = h.reshape(nb, length, d)
            h = jnp.concatenate([h3[:, :n_ctx], _to_scan_order(h3[:, n_ctx:], rows)], axis=1)
            h = h.reshape(t_all, d)
        w_c = w_in_t[l:l + 1, U_C0:U_G0]
        w_small = jnp.concatenate([w_in_t[l:l + 1, U_DT0:U_C0], w_in_t[l:l + 1, U_G0:],
                                   jnp.zeros((1, U_SMALL - 96, d), F32)], axis=1)
        u_ab = matmul_nt(h, w_in_t, l, tm=1024, tn=512, n_tiles=U_AB // 512, name="in_proj_ab")
        u_c = matmul_nt(h, w_c, 0, tm=1024, tn=512, name="in_proj_c")
        u_s = matmul_nt(h, w_small, 0, tm=1024, tn=U_SMALL, name="in_proj_small")
        u_ab3 = u_ab.reshape(nb, length, U_AB)
        u_c3 = u_c.reshape(nb, length, 4 * C_WIDTH)
        u_s3 = u_s.reshape(nb, length, U_SMALL)

        xc = ssd_conv(u_ab3, ssm_conv_w[l], ssm_conv_b[l], n_ctx)
        consts = ssd_consts(ssm_dt_bias[l], ssm_a_log[l])
        gate_bias = mlstm_gate_bias(mlstm_i_bias[l], mlstm_f_bias[l])
        ya = [hgrn2_scan(u_ab3, lb_table(lb_all[l][dd]), n_ctx, rev=bool(dd)) for dd in range(2)]
        yb = [ssd_scan(xc, u_s3, *consts[dd], n_ctx, rev=bool(dd)) for dd in range(2)]
        yc = mlstm_scan(u_c3, u_s3, gate_bias, n_ctx)
        flat = lambda a: a.reshape(t_all, a.shape[-1])
        ymix = mix_out(flat(ya[0]), flat(ya[1]), flat(yb[0]), flat(yb[1]), flat(yc[0]), flat(yc[1]),
                       u_ab, flat(xc), u_c, hgrn_norm[l][None], ssm_norm[l][None], mlstm_norm[l][None],
                       jnp.repeat(ssm_d[l].astype(F32), B_HEADDIM)[None])
        if l % 2 == 1:
            y3 = ymix.reshape(nb, length, d)
            ymix = jnp.concatenate([y3[:, :n_ctx], _from_scan_order(y3[:, n_ctx:], rows)], axis=1)
            ymix = ymix.reshape(t_all, d)

        moe_w = (l, norm_ffn[l], router_w[l], router_bias[l], moe_w_gate, moe_w_up, moe_w_down,
                 shared_w_gate[l], shared_w_up[l], shared_w_down[l])
        ymix = ymix.reshape(nb, length, d)
        w_o = w_out[l].astype(BF16)
        if not last:
            x2 = matmul_residual(ymix, w_o, xs, tab[2], first_row=0, n_rows=length,
                                 ctx_rows=n_ctx, tm=256, tn=1024).reshape(t_all, d)
            x2 = _moe_block(x2, tab, _seg_fn(length, n_ctx, 256), _seg_fn(length, n_ctx, CMB_TM),
                            norm_final, False, *moe_w)
            xs = x2.reshape(nb, length, d)
        else:
            x2 = matmul_residual(ymix, w_o, xs, tab[2], first_row=n_ctx, n_rows=seq,
                                 ctx_rows=n_ctx, tm=256, tn=1024).reshape(t_lat, d)
            out = _moe_block(x2, tab, _seg_fn(seq, 0, 256), _seg_fn(seq, 0, CMB_TM),
                             norm_final, True, *moe_w)
            return out.reshape(nb, seq, d)
```
